```python
import math
import jax, jax.numpy as jnp
from jax import lax
import numpy as np

D_MODEL = 1024
BATCH = 8
SEQ = 4096
DEPTH = 1
DEC_BATCH = 8
DEC_SEQ = 16
PAST_LEN = 4096

CHUNK = 64
QBLOCK = 128
DA_HEADS = 4
DA_DV = D_MODEL // 2 // DA_HEADS
DA_DH = DA_DV // 2
DA_QK = DA_HEADS * 2 * DA_DH
DA_WIDTH = DA_HEADS * DA_DV
GDN_HEADS = 4
GDN_DV = D_MODEL // 2 // GDN_HEADS
GDN_DK = GDN_DV
GDN_WIDTH = GDN_HEADS * GDN_DV
GDN_QKV = GDN_HEADS * (2 * GDN_DK + GDN_DV)
CONV_W = 4
IN_COLS = 2 * DA_QK + DA_WIDTH + GDN_QKV + 2 * GDN_HEADS + GDN_WIDTH
MIX_WIDTH = DA_WIDTH + GDN_WIDTH
X_HEADS = 4
X_DH = D_MODEL // X_HEADS
N_MEM = 256
N_GROUPS = 4
E_PER_GROUP = 8
N_EXPERTS = N_GROUPS * E_PER_GROUP
TOP_K = 2
D_EXPERT = 256
NORM_EPS = 1e-6
NEG_INF = -1e30

kernel_name = 'hybrid_diffattn_gdn_hmoe_stream_step'


def rmsnorm(x, g):
    xf = x.astype(jnp.float32)
    y = xf * lax.rsqrt(jnp.mean(xf * xf, axis=-1, keepdims=True) + NORM_EPS)
    return (y * g.astype(jnp.float32)).astype(x.dtype)


def l2norm(x):
    xf = x.astype(jnp.float32)
    return xf * lax.rsqrt(jnp.sum(xf * xf, axis=-1, keepdims=True) + NORM_EPS)


def diff_attn_core(q, k, v, q_pos, k_pos, lam):
    s = jnp.einsum('bqhmd,bkhmd->bhmqk', q, k).astype(jnp.float32) * (DA_DH ** -0.5)
    visible = (q_pos[:, None] // CHUNK) >= (k_pos[None, :] // CHUNK)
    p = jax.nn.softmax(jnp.where(visible, s, NEG_INF), axis=-1)
    a = p[:, :, 0] - lam * p[:, :, 1]
    return jnp.einsum('bhqk,bkhe->bqhe', a.astype(v.dtype), v)


def diff_attn_prompt(q, k, v, lam):
    B, T = q.shape[0], q.shape[1]
    nb = T // QBLOCK
    qb = jnp.moveaxis(q.reshape((B, nb, QBLOCK) + q.shape[2:]), 1, 0)
    k_pos = jnp.arange(T)

    def one_block(args):
        q_blk, start = args
        return diff_attn_core(q_blk, k, v, start + jnp.arange(QBLOCK), k_pos, lam)

    o = lax.map(one_block, (qb, jnp.arange(nb) * QBLOCK))
    return jnp.moveaxis(o, 0, 1).reshape((B, T) + o.shape[3:])


def causal_short_conv(x, prev, conv_w):
    T = x.shape[1]
    xp = jnp.concatenate([prev.astype(x.dtype), x], axis=1)
    y = sum(xp[:, j:j + T] * conv_w[j] for j in range(CONV_W))
    return jax.nn.silu(y), xp[:, xp.shape[1] - (CONV_W - 1):]


def gated_delta_rule(q, k, v, g, beta, s0):
    f32 = jnp.float32
    B, T, H, DK = k.shape
    DV = v.shape[-1]
    n = -(-T // CHUNK)
    pad = n * CHUNK - T

    def to_blocks(a):
        a = a.astype(f32)
        a = jnp.pad(a, [(0, 0), (0, pad)] + [(0, 0)] * (a.ndim - 2))
        a = a.reshape((B, n, CHUNK) + a.shape[2:])
        return jnp.moveaxis(a, (1, 2), (0, 3))

    qc = to_blocks(q) * (DK ** -0.5)
    kc, vc, bc = to_blocks(k), to_blocks(v), to_blocks(beta)
    gc = jnp.cumsum(to_blocks(g), axis=-1)
    idx = jnp.arange(CHUNK)
    causal = idx[:, None] >= idx[None, :]
    strict = idx[:, None] > idx[None, :]
    decay = jnp.exp(jnp.where(causal, gc[..., :, None] - gc[..., None, :], NEG_INF))
    kb = kc * bc[..., None]
    lower = jnp.where(strict, jnp.einsum('nbhid,nbhjd->nbhij', kb, kc) * decay, 0.0)
    eye = jnp.eye(CHUNK, dtype=f32)
    t_inv = lax.linalg.triangular_solve(eye + lower, jnp.broadcast_to(eye, lower.shape),
                                        left_side=True, lower=True, unit_diagonal=True)
    u = jnp.einsum('nbhij,nbhjd->nbhid', t_inv, vc * bc[..., None])
    w = jnp.einsum('nbhij,nbhjd->nbhid', t_inv, kb * jnp.exp(gc)[..., None])
    a_intra = jnp.einsum('nbhid,nbhjd->nbhij', qc, kc) * decay
    q_dec = qc * jnp.exp(gc)[..., None]
    k_tail = kc * jnp.exp(gc[..., -1:] - gc)[..., None]
    g_last = jnp.exp(gc[..., -1])

    def block_step(s, xs):
        u_i, w_i, a_i, q_i, kt_i, gl_i = xs
        v_new = u_i - jnp.einsum('bhck,bhkv->bhcv', w_i, s)
        o_i = jnp.einsum('bhck,bhkv->bhcv', q_i, s) + jnp.einsum('bhij,bhjv->bhiv', a_i, v_new)
        s = s * gl_i[..., None, None] + jnp.einsum('bhck,bhcv->bhkv', kt_i, v_new)
        return s, o_i

    s_fin, o = lax.scan(block_step, s0.astype(f32), (u, w, a_intra, q_dec, k_tail, g_last))
    o = jnp.moveaxis(o, (0, 3), (1, 2)).reshape(B, n * CHUNK, H, DV)[:, :T]
    return o, s_fin


def memory_kv(mem, g_mem, w_ck, w_cv):
    B, M = mem.shape[0], mem.shape[1]
    m = rmsnorm(mem, g_mem)
    return ((m @ w_ck).reshape(B, M, X_HEADS, X_DH), (m @ w_cv).reshape(B, M, X_HEADS, X_DH))


def cross_attn(h, mem_k, mem_v, w_cq, w_co):
    B, T, _ = h.shape
    q = (h @ w_cq).reshape(B, T, X_HEADS, X_DH)
    s = jnp.einsum('bqhd,bmhd->bhqm', q, mem_k).astype(jnp.float32) * (X_DH ** -0.5)
    p = jax.nn.softmax(s, axis=-1).astype(h.dtype)
    o = jnp.einsum('bhqm,bmhd->bqhd', p, mem_v)
    return o.reshape(B, T, X_HEADS * X_DH) @ w_co


def hier_moe(x, w_grp, b_grp, w_rt, b_rt, w_gate, w_up, w_down):
    N = x.shape[0]
    grp_prob = jax.nn.softmax((x @ w_grp).astype(jnp.float32) + b_grp.astype(jnp.float32), axis=-1)
    g_top, g_idx = lax.top_k(grp_prob, 1)
    e_logits = ((x @ w_rt).astype(jnp.float32) + b_rt.astype(jnp.float32)).reshape(N, N_GROUPS, E_PER_GROUP)
    in_grp = jnp.take_along_axis(e_logits, g_idx[:, :, None], axis=1)[:, 0]
    e_top, e_idx = lax.top_k(in_grp, TOP_K)
    e_w = jax.nn.softmax(e_top, axis=-1) * g_top
    flat = g_idx * E_PER_GROUP + e_idx
    comb = jnp.sum(jax.nn.one_hot(flat, N_EXPERTS, dtype=jnp.float32) * e_w[..., None], axis=1).astype(x.dtype)
    out = jnp.zeros_like(x)
    for gi in range(N_GROUPS):
        sl = slice(gi * E_PER_GROUP, (gi + 1) * E_PER_GROUP)
        hg = jax.nn.silu(jnp.einsum('nd,edf->nef', x, w_gate[sl])) * jnp.einsum('nd,edf->nef', x, w_up[sl])
        out = out + jnp.einsum('nef,efd->nd', hg * comb[:, sl, None], w_down[sl])
    return out


def trunk_layer(x, mem_k, mem_v, past_k, past_v, conv_prev, s_prev, lam_init,
                g_mix, w_in, lam_q1, lam_k1, lam_q2, lam_k2, g_subln, conv_w, a_log, dt_bias,
                g_gdn_out, w_out, g_cross, w_cq, w_co, g_ffn, w_grp, b_grp, w_rt, b_rt,
                w_gate, w_up, w_down):
    f32 = jnp.float32
    B, T, _ = x.shape
    h = rmsnorm(x, g_mix)
    sizes = (DA_QK, DA_QK, DA_WIDTH, GDN_QKV, GDN_HEADS, GDN_HEADS, GDN_WIDTH)
    cuts = [int(c) for c in np.cumsum(sizes)[:-1]]
    dq, dk, dv, gqkv, ga, gb, gz = jnp.split(h @ w_in, cuts, axis=-1)

    dq = dq.reshape(B, T, DA_HEADS, 2, DA_DH)
    dk = dk.reshape(B, T, DA_HEADS, 2 * DA_DH)
    dv = dv.reshape(B, T, DA_HEADS, DA_DV)
    lam = (jnp.exp(jnp.sum(lam_q1.astype(f32) * lam_k1.astype(f32)))
           - jnp.exp(jnp.sum(lam_q2.astype(f32) * lam_k2.astype(f32))) + lam_init)
    if past_k is None:
        o_da = diff_attn_prompt(dq, dk.reshape(B, T, DA_HEADS, 2, DA_DH), dv, lam)
    else:
        P = past_k.shape[1]
        k_all = jnp.concatenate([past_k.astype(dk.dtype), dk], axis=1).reshape(B, P + T, DA_HEADS, 2, DA_DH)
        v_all = jnp.concatenate([past_v.astype(dv.dtype), dv], axis=1)
        o_da = diff_attn_core(dq, k_all, v_all, P + jnp.arange(T), jnp.arange(P + T), lam)
    o_da = rmsnorm(o_da, g_subln) * (1.0 - lam_init)

    if conv_prev is None:
        conv_prev = jnp.zeros((B, CONV_W - 1, GDN_QKV), x.dtype)
    if s_prev is None:
        s_prev = jnp.zeros((B, GDN_HEADS, GDN_DK, GDN_DV), f32)
    c, conv_new = causal_short_conv(gqkv, conv_prev, conv_w)
    cq, ck, cv = jnp.split(c, [GDN_HEADS * GDN_DK, 2 * GDN_HEADS * GDN_DK], axis=-1)
    cq = l2norm(cq.reshape(B, T, GDN_HEADS, GDN_DK))
    ck = l2norm(ck.reshape(B, T, GDN_HEADS, GDN_DK))
    cv = cv.reshape(B, T, GDN_HEADS, GDN_DV)
    g = -jnp.exp(a_log.astype(f32)) * jax.nn.softplus(ga.astype(f32) + dt_bias.astype(f32))
    beta = jax.nn.sigmoid(gb.astype(f32))
    o_g, s_new = gated_delta_rule(cq, ck, cv, g, beta, s_prev)
    o_g = rmsnorm(o_g.astype(x.dtype), g_gdn_out) * jax.nn.silu(gz.reshape(B, T, GDN_HEADS, GDN_DV))

    mix = jnp.concatenate([o_da.reshape(B, T, DA_WIDTH), o_g.reshape(B, T, GDN_WIDTH)], axis=-1)
    x = x + mix @ w_out
    x = x + cross_attn(rmsnorm(x, g_cross), mem_k, mem_v, w_cq, w_co)
    x = x + hier_moe(rmsnorm(x, g_ffn).reshape(B * T, D_MODEL), w_grp, b_grp, w_rt, b_rt,
                     w_gate, w_up, w_down).reshape(B, T, D_MODEL)
    return x, (dk, dv, conv_new, s_new.astype(x.dtype))


def setup_inputs(seed: int = 0) -> dict:
    key = jax.random.key(seed)
    ks = iter(jax.random.split(key, 48))
    f32 = jnp.float32
    L = DEPTH

    def nrm(shape, scale=1.0):
        return jax.random.normal(next(ks), shape, f32) * scale

    def gain(shape):
        return 1.0 + 0.01 * jax.random.normal(next(ks), shape, f32)

    dt = jnp.exp(jax.random.uniform(next(ks), (L, GDN_HEADS), f32, math.log(1e-3), math.log(1e-1)))
    a_log = jnp.log(jax.random.uniform(next(ks), (L, GDN_HEADS), f32, 1.0, 16.0))
    return {
        'x_prompt': nrm((BATCH, SEQ, D_MODEL)),
        'x_sample': nrm((DEC_BATCH, DEC_SEQ, D_MODEL)),
        'cache_diff_k': nrm((L, DEC_BATCH, PAST_LEN, DA_HEADS, 2 * DA_DH)),
        'cache_diff_v': nrm((L, DEC_BATCH, PAST_LEN, DA_HEADS, DA_DV)),
        'state_gdn_conv': nrm((L, DEC_BATCH, CONV_W - 1, GDN_QKV)),
        'state_gdn': nrm((L, DEC_BATCH, GDN_HEADS, GDN_DK, GDN_DV), 0.1),
        'cache_mem_k': nrm((L, DEC_BATCH, N_MEM, X_HEADS, X_DH)),
        'cache_mem_v': nrm((L, DEC_BATCH, N_MEM, X_HEADS, X_DH)),
        'mem_prompt': nrm((BATCH, N_MEM, D_MODEL)),
        'g_mix': gain((L, D_MODEL)),
        'w_in': nrm((L, D_MODEL, IN_COLS), D_MODEL ** -0.5),
        'lam_q1': nrm((L, DA_DH), 0.1),
        'lam_k1': nrm((L, DA_DH), 0.1),
        'lam_q2': nrm((L, DA_DH), 0.1),
        'lam_k2': nrm((L, DA_DH), 0.1),
        'g_subln': gain((L, DA_DV)),
        'conv_w': nrm((L, CONV_W, GDN_QKV), CONV_W ** -0.5),
        'a_log': a_log,
        'dt_bias': dt + jnp.log(-jnp.expm1(-dt)),
        'g_gdn_out': gain((L, GDN_DV)),
        'w_out': nrm((L, MIX_WIDTH, D_MODEL), MIX_WIDTH ** -0.5),
        'g_cross': gain((L, D_MODEL)),
        'g_mem': gain((L, D_MODEL)),
        'w_ck': nrm((L, D_MODEL, X_HEADS * X_DH), D_MODEL ** -0.5),
        'w_cv': nrm((L, D_MODEL, X_HEADS * X_DH), D_MODEL ** -0.5),
        'w_cq': nrm((L, D_MODEL, X_HEADS * X_DH), D_MODEL ** -0.5),
        'w_co': nrm((L, X_HEADS * X_DH, D_MODEL), (X_HEADS * X_DH) ** -0.5),
        'g_ffn': gain((L, D_MODEL)),
        'w_grp': nrm((L, D_MODEL, N_GROUPS), D_MODEL ** -0.5),
        'b_grp': nrm((L, N_GROUPS), 0.01),
        'w_rt': nrm((L, D_MODEL, N_EXPERTS), D_MODEL ** -0.5),
        'b_rt': nrm((L, N_EXPERTS), 0.01),
        'w_gate': nrm((L, N_EXPERTS, D_MODEL, D_EXPERT), D_MODEL ** -0.5),
        'w_up': nrm((L, N_EXPERTS, D_MODEL, D_EXPERT), D_MODEL ** -0.5),
        'w_down': nrm((L, N_EXPERTS, D_EXPERT, D_MODEL), D_EXPERT ** -0.5),
        'g_final': gain((D_MODEL,)),
    }


def reference(x_prompt, x_sample, cache_diff_k, cache_diff_v, state_gdn_conv, state_gdn,
              cache_mem_k, cache_mem_v, mem_prompt,
              g_mix, w_in, lam_q1, lam_k1, lam_q2, lam_k2, g_subln, conv_w, a_log, dt_bias,
              g_gdn_out, w_out, g_cross, g_mem, w_ck, w_cv, w_cq, w_co, g_ffn,
              w_grp, b_grp, w_rt, b_rt, w_gate, w_up, w_down, g_final):
    xp, xs = x_prompt, x_sample
    pk, pv, pc, ps, pmk, pmv, sk, sv, sc, ss = ([] for _ in range(10))
    for l in range(DEPTH):
        lam_init = 0.8 - 0.6 * math.exp(-0.3 * l)
        lw = (g_mix[l], w_in[l], lam_q1[l], lam_k1[l], lam_q2[l], lam_k2[l], g_subln[l], conv_w[l],
              a_log[l], dt_bias[l], g_gdn_out[l], w_out[l], g_cross[l], w_cq[l], w_co[l], g_ffn[l],
              w_grp[l], b_grp[l], w_rt[l], b_rt[l], w_gate[l], w_up[l], w_down[l])
        mk, mv = memory_kv(mem_prompt, g_mem[l], w_ck[l], w_cv[l])
        xp, (k_p, v_p, c_p, s_p) = trunk_layer(xp, mk, mv, None, None, None, None, lam_init, *lw)
        pk.append(k_p); pv.append(v_p); pc.append(c_p); ps.append(s_p); pmk.append(mk); pmv.append(mv)
        xs, (k_s, v_s, c_s, s_s) = trunk_layer(xs, cache_mem_k[l], cache_mem_v[l], cache_diff_k[l],
                                               cache_diff_v[l], state_gdn_conv[l], state_gdn[l],
                                               lam_init, *lw)
        sk.append(k_s); sv.append(v_s); sc.append(c_s); ss.append(s_s)
    y_prompt = rmsnorm(xp, g_final)
    y_sample = rmsnorm(xs, g_final)
    return (y_prompt, y_sample,
            jnp.stack(pk), jnp.stack(pv), jnp.stack(pc), jnp.stack(ps), jnp.stack(pmk), jnp.stack(pmv),
            jnp.stack(sk), jnp.stack(sv), jnp.stack(sc), jnp.stack(ss))
```

```python
import functools
import math

import jax
import jax.numpy as jnp
import numpy as np
from jax import lax
from jax.experimental import pallas as pl
from jax.experimental.pallas import tpu as pltpu

F32 = jnp.float32
BF16 = jnp.bfloat16

D_MODEL = 1024
CHUNK = 64
HEADS = 4
HEAD_W = 128
DA_DH = 64
MIX_HALF = HEADS * HEAD_W
GDN_QKV = 3 * MIX_HALF
CONV_W = 4
X_HEADS = 4
X_DH = 256
N_GROUPS = 4
E_PER_GROUP = 8
N_EXPERTS = 32
D_EXPERT = 256
NORM_EPS = 1e-6
NEG_INF = -1e30
LANES = 128
SUBLANES = 8
VMEM_LIMIT = 56 * 1024 * 1024


def _dot(a, b):
    return jnp.dot(a, b, preferred_element_type=F32)


def _dot_nt(a, b):
    return lax.dot_general(a, b, (((1,), (1,)), ((), ())), preferred_element_type=F32)


def _rms(x, g):
    return x * lax.rsqrt(jnp.mean(x * x, axis=-1, keepdims=True) + NORM_EPS) * g


def _sigmoid(x):
    return 1.0 / (1.0 + jnp.exp(-x))


def _silu(x):
    return x * _sigmoid(x)


def _split_bf16(a):
    hi = a.astype(BF16)
    lo = (a - hi.astype(F32)).astype(BF16)
    return hi, lo


def _dot3(a, b):
    ah, al = _split_bf16(a)
    bh, bl = _split_bf16(b)
    return _dot(ah, bh) + (_dot(ah, bl) + _dot(al, bh))


def _params(sem):
    return pltpu.CompilerParams(dimension_semantics=sem, vmem_limit_bytes=VMEM_LIMIT)


def _inproj_kernel(x_ref, prev_ref, gmix_ref, wm_ref, wg_ref, cw_ref, gp_ref,
                   dq_ref, dkf_ref, dkb_ref, dvf_ref, dvb_ref, cq_ref, ck_ref, cv_ref, gz_ref, gate_ref,
                   cnew_ref, xp_scr, *, tm):
    t = pl.program_id(1)
    h = _rms(x_ref[0], gmix_ref[...]).astype(BF16)

    dq = _dot(h, wm_ref[:, 0:MIX_HALF])
    dq_ref[0] = (dq * (DA_DH ** -0.5)).astype(BF16)
    dk = _dot(h, wm_ref[:, MIX_HALF:2 * MIX_HALF])
    dkf_ref[0] = dk
    dkb_ref[0] = dk.astype(BF16)
    dv = _dot(h, wm_ref[:, 2 * MIX_HALF:3 * MIX_HALF])
    dvf_ref[0] = dv
    dvb_ref[0] = dv.astype(BF16)
    gz_ref[0] = _dot(h, wm_ref[:, 3 * MIX_HALF + GDN_QKV:4 * MIX_HALF + GDN_QKV])

    gates = _dot(h, wg_ref[...])
    z = gates + gp_ref[1:2, :]
    softplus = jnp.maximum(z, 0.0) + jnp.log1p(jnp.exp(-jnp.abs(z)))
    g_all = -jnp.exp(gp_ref[0:1, :]) * softplus
    lane = lax.broadcasted_iota(jnp.int32, gates.shape, 1)
    gate_ref[0] = jnp.where(lane < HEADS, g_all, jnp.where(lane < 2 * HEADS, _sigmoid(gates), 0.0))

    @pl.when(t == 0)
    def _():
        xp_scr[0:SUBLANES, :] = prev_ref[0]

    xp_scr[SUBLANES:SUBLANES + tm, :] = _dot(h, wm_ref[:, 3 * MIX_HALF:3 * MIX_HALF + GDN_QKV])
    outs = (cq_ref, ck_ref, cv_ref)
    for part in range(3):
        cs = slice(part * MIX_HALF, (part + 1) * MIX_HALF)
        y = xp_scr[5:5 + tm, cs] * cw_ref[0:1, cs]
        for j in range(1, CONV_W):
            y = y + xp_scr[5 + j:5 + j + tm, cs] * cw_ref[j:j + 1, cs]
        c = _silu(y)
        if part == 2:
            outs[part][0] = c
        else:
            scale = (HEAD_W ** -0.5) if part == 0 else 1.0
            for hh in range(HEADS):
                ch = c[:, hh * HEAD_W:(hh + 1) * HEAD_W]
                n = ch * lax.rsqrt(jnp.sum(ch * ch, axis=-1, keepdims=True) + NORM_EPS)
                outs[part][0, :, hh * HEAD_W:(hh + 1) * HEAD_W] = n * scale if part == 0 else n
    carry = xp_scr[tm:tm + SUBLANES, :]
    xp_scr[0:SUBLANES, :] = carry
    cnew_ref[0] = carry


def _inproj(x, prev8, g_mix, w_main, w_gate, conv_w8, gate_par, tm):
    B, T, D = x.shape
    nt = T // tm
    row = lambda w, dt: jax.ShapeDtypeStruct((B, T, w), dt)
    blk = lambda w: pl.BlockSpec((1, tm, w), lambda b, t: (b, t, 0))
    full = lambda a: pl.BlockSpec(a.shape, lambda b, t: (0,) * a.ndim)
    return pl.pallas_call(
        functools.partial(_inproj_kernel, tm=tm),
        grid=(B, nt),
        in_specs=[blk(D), pl.BlockSpec((1, SUBLANES, GDN_QKV), lambda b, t: (b, 0, 0)),
                  full(g_mix), full(w_main), full(w_gate), full(conv_w8), full(gate_par)],
        out_specs=[blk(MIX_HALF)] * 9 + [blk(LANES), pl.BlockSpec((1, SUBLANES, GDN_QKV), lambda b, t: (b, 0, 0))],
        out_shape=[row(MIX_HALF, BF16), row(MIX_HALF, F32), row(MIX_HALF, BF16), row(MIX_HALF, F32),
                   row(MIX_HALF, BF16), row(MIX_HALF, F32), row(MIX_HALF, F32), row(MIX_HALF, F32),
                   row(MIX_HALF, F32), row(LANES, F32), jax.ShapeDtypeStruct((B, SUBLANES, GDN_QKV), F32)],
        scratch_shapes=[pltpu.VMEM((tm + SUBLANES, GDN_QKV), F32)],
        compiler_params=_params(("parallel", "arbitrary")),
        name="inproj",
    )(x, prev8, g_mix, w_main, w_gate, conv_w8, gate_par)


def _lam(lam_ref, lam_init):
    s1 = jnp.sum(lam_ref[0:1, :] * lam_ref[1:2, :], axis=-1, keepdims=True)
    s2 = jnp.sum(lam_ref[2:3, :] * lam_ref[3:4, :], axis=-1, keepdims=True)
    return jnp.exp(s1) - jnp.exp(s2) + lam_init


def _stack_q(q):
    lane = lax.broadcasted_iota(jnp.int32, q.shape, 1)
    zero = jnp.zeros_like(q)
    return jnp.concatenate([jnp.where(lane < DA_DH, q, zero), jnp.where(lane >= DA_DH, q, zero)], axis=0)


def _da_finish(acc, l, lam, gs, lam_init, tq):
    o = acc[:tq] / l[:tq] - lam * (acc[tq:] / l[tq:])
    return (_rms(o, gs) * (1.0 - lam_init)).astype(BF16)


def _da_prompt_kernel(lam_ref, gs_ref, q_ref, k_ref, v_ref, o_ref, m_scr, l_scr, acc_scr, *, tq, lam_init):
    i = pl.program_id(2)
    qz = _stack_q(q_ref[0])
    m_scr[...] = jnp.full(m_scr.shape, NEG_INF, F32)
    l_scr[...] = jnp.zeros(l_scr.shape, F32)
    acc_scr[...] = jnp.zeros(acc_scr.shape, F32)

    def step(j, masked):
        start = pl.multiple_of(j * tq, tq)
        s = _dot_nt(qz, k_ref[0, pl.ds(start, tq), :])
        if masked:
            r = lax.broadcasted_iota(jnp.int32, s.shape, 0)
            c = lax.broadcasted_iota(jnp.int32, s.shape, 1)
            r = jnp.where(r >= tq, r - tq, r)
            s = jnp.where((r // CHUNK) >= (c // CHUNK), s, NEG_INF)
        m_prev = m_scr[...]
        m_new = jnp.maximum(m_prev, jnp.max(s, axis=-1, keepdims=True))
        alpha = jnp.exp(m_prev - m_new)
        p = jnp.exp(s - m_new)
        l_scr[...] = alpha * l_scr[...] + jnp.sum(p, axis=-1, keepdims=True)
        acc_scr[...] = alpha * acc_scr[...] + _dot(p.astype(BF16), v_ref[0, pl.ds(start, tq), :])
        m_scr[...] = m_new

    def body(j, carry):
        step(j, False)
        return carry

    lax.fori_loop(0, i, body, 0)
    step(i, True)
    o_ref[0] = _da_finish(acc_scr[...], l_scr[...], _lam(lam_ref, lam_init), gs_ref[...], lam_init, tq)


def _da_prompt(lam_par, g_subln, q, k, v, tq, lam_init):
    B, T, _ = q.shape
    kv_spec = pl.BlockSpec((1, T, HEAD_W), lambda b, h, i: (b, 0, h))
    q_spec = pl.BlockSpec((1, tq, HEAD_W), lambda b, h, i: (b, i, h))
    full = lambda a: pl.BlockSpec(a.shape, lambda b, h, i: (0,) * a.ndim)
    return pl.pallas_call(
        functools.partial(_da_prompt_kernel, tq=tq, lam_init=lam_init),
        grid=(B, HEADS, T // tq),
        in_specs=[full(lam_par), full(g_subln), q_spec, kv_spec, kv_spec],
        out_specs=q_spec,
        out_shape=jax.ShapeDtypeStruct((B, T, MIX_HALF), BF16),
        scratch_shapes=[pltpu.VMEM((2 * tq, 1), F32), pltpu.VMEM((2 * tq, 1), F32),
                        pltpu.VMEM((2 * tq, HEAD_W), F32)],
        compiler_params=_params(("parallel", "parallel", "arbitrary")),
        name="diffattn_prompt",
    )(lam_par, g_subln, q, k, v)


def _da_sample_kernel(lam_ref, gs_ref, q_ref, pk_ref, pv_ref, k_ref, v_ref, o_ref, *, tq, past, lam_init):
    qz = _stack_q(q_ref[0])
    s_old = _dot_nt(qz, pk_ref[0].astype(BF16))
    s_new = _dot_nt(qz, k_ref[0])

    def visible(shape, k_off):
        r = lax.broadcasted_iota(jnp.int32, shape, 0)
        c = lax.broadcasted_iota(jnp.int32, shape, 1) + k_off
        r = jnp.where(r >= tq, r - tq, r) + past
        return (r // CHUNK) >= (c // CHUNK)

    s_old = jnp.where(visible(s_old.shape, 0), s_old, NEG_INF)
    s_new = jnp.where(visible(s_new.shape, past), s_new, NEG_INF)
    m = jnp.maximum(jnp.max(s_old, axis=-1, keepdims=True), jnp.max(s_new, axis=-1, keepdims=True))
    p_old = jnp.exp(s_old - m)
    p_new = jnp.exp(s_new - m)
    l = jnp.sum(p_old, axis=-1, keepdims=True) + jnp.sum(p_new, axis=-1, keepdims=True)
    acc = _dot(p_old.astype(BF16), pv_ref[0].astype(BF16)) + _dot(p_new.astype(BF16), v_ref[0])
    o_ref[0] = _da_finish(acc, l, _lam(lam_ref, lam_init), gs_ref[...], lam_init, tq)


def _da_sample(lam_par, g_subln, q, past_k, past_v, k, v, lam_init):
    B, T, _ = q.shape
    P = past_k.shape[1]
    new_spec = pl.BlockSpec((1, T, HEAD_W), lambda b, h: (b, 0, h))
    past_spec = pl.BlockSpec((1, P, HEAD_W), lambda b, h: (b, 0, h))
    full = lambda a: pl.BlockSpec(a.shape, lambda b, h: (0,) * a.ndim)
    return pl.pallas_call(
        functools.partial(_da_sample_kernel, tq=T, past=P, lam_init=lam_init),
        grid=(B, HEADS),
        in_specs=[full(lam_par), full(g_subln), new_spec, past_spec, past_spec, new_spec, new_spec],
        out_specs=new_spec,
        out_shape=jax.ShapeDtypeStruct((B, T, MIX_HALF), BF16),
        compiler_params=_params(("parallel", "parallel")),
        name="diffattn_sample",
    )(lam_par, g_subln, q, past_k, past_v, k, v)


def _gdn_kernel(q_ref, k_ref, v_ref, gz_ref, gate_ref, s0_ref, gout_ref, o_ref, sfin_ref, s_scr, *, C, n_chunks):
    c = pl.program_id(1)

    @pl.when(c == 0)
    def _():
        s_scr[...] = s0_ref[0]

    gates = gate_ref[0]
    row = lax.broadcasted_iota(jnp.int32, (C, C), 0)
    col = lax.broadcasted_iota(jnp.int32, (C, C), 1)
    causal = row >= col
    strict = row > col
    eye = (row == col).astype(F32)
    prow = lax.broadcasted_iota(jnp.int32, (LANES, C), 0)
    pcol = lax.broadcasted_iota(jnp.int32, (LANES, C), 1)
    tri = jnp.where((prow >= pcol) & (prow < C), 1.0, 0.0).astype(F32)
    gc_pad = jnp.dot(tri, gates, preferred_element_type=F32, precision=lax.Precision.HIGHEST)
    gc_t = gc_pad.T
    gc = gc_pad[0:C, :]

    for h in range(HEADS):
        sl = slice(h * HEAD_W, (h + 1) * HEAD_W)
        q = q_ref[0, :, sl]
        k = k_ref[0, :, sl]
        v = v_ref[0, :, sl]
        gcol = gc[:, h:h + 1]
        grow = gc_t[h:h + 1, 0:C]
        bcol = gates[:, HEADS + h:HEADS + h + 1]
        glast = gc[C - 1:C, h:h + 1]
        decay = jnp.exp(jnp.where(causal, gcol - grow, NEG_INF))
        egc = jnp.exp(gcol)
        kb = k * bcol
        k16 = k.astype(BF16)
        a = jnp.where(strict, -(_dot_nt(kb.astype(BF16), k16) * decay), 0.0)
        x = eye + a
        p = a
        for _ in range(int(math.log2(C)) - 1):
            p = _dot3(p, p)
            x = x + _dot3(x, p)
        t16 = x.astype(BF16)
        uw = _dot(t16, jnp.concatenate([v * bcol, kb * egc], axis=1).astype(BF16))
        u = uw[:, 0:HEAD_W]
        w = uw[:, HEAD_W:2 * HEAD_W]
        a_intra = _dot_nt(q.astype(BF16), k16) * decay
        k_tail = k * jnp.exp(glast - gcol)
        s = s_scr[h]
        s16 = s.astype(BF16)
        ws_qs = _dot(jnp.concatenate([w, q * egc], axis=0).astype(BF16), s16)
        v_new = (u - ws_qs[0:C]).astype(BF16)
        o = ws_qs[C:2 * C] + _dot(a_intra.astype(BF16), v_new)
        s_scr[h] = s * jnp.exp(glast) + lax.dot_general(
            k_tail.astype(BF16), v_new, (((0,), (0,)), ((), ())), preferred_element_type=F32)
        o_ref[0, :, sl] = (_rms(o, gout_ref[...]) * _silu(gz_ref[0, :, sl])).astype(BF16)

    @pl.when(c == n_chunks - 1)
    def _():
        sfin_ref[0] = s_scr[...]


def _gdn(q, k, v, gz, gates, s0, g_out, C):
    B, T, _ = q.shape
    nc = T // C
    blk = lambda w: pl.BlockSpec((1, C, w), lambda b, c: (b, c, 0))
    st = pl.BlockSpec((1, HEADS, HEAD_W, HEAD_W), lambda b, c: (b, 0, 0, 0))
    return pl.pallas_call(
        functools.partial(_gdn_kernel, C=C, n_chunks=nc),
        grid=(B, nc),
        in_specs=[blk(MIX_HALF)] * 4 + [blk(LANES), st, pl.BlockSpec(g_out.shape, lambda b, c: (0, 0))],
        out_specs=[blk(MIX_HALF), st],
        out_shape=[jax.ShapeDtypeStruct((B, T, MIX_HALF), BF16),
                   jax.ShapeDtypeStruct((B, HEADS, HEAD_W, HEAD_W), F32)],
        scratch_shapes=[pltpu.VMEM((HEADS, HEAD_W, HEAD_W), F32)],
        compiler_params=_params(("parallel", "arbitrary")),
        name="gdn",
    )(q, k, v, gz, gates, s0, g_out)


def _memkv_kernel(m_ref, g_ref, wk_ref, wv_ref, kf_ref, kb_ref, vf_ref, vb_ref):
    m = _rms(m_ref[...], g_ref[...]).astype(BF16)
    mk = _dot(m, wk_ref[...])
    mv = _dot(m, wv_ref[...])
    kf_ref[...] = mk
    kb_ref[...] = mk.astype(BF16)
    vf_ref[...] = mv
    vb_ref[...] = mv.astype(BF16)


def _memkv(mem, g_mem, w_ck, w_cv, tm):
    N, D = mem.shape
    blk = pl.BlockSpec((tm, D), lambda i: (i, 0))
    full = lambda a: pl.BlockSpec(a.shape, lambda i: (0,) * a.ndim)
    return pl.pallas_call(
        _memkv_kernel,
        grid=(N // tm,),
        in_specs=[blk, full(g_mem), full(w_ck), full(w_cv)],
        out_specs=[blk] * 4,
        out_shape=[jax.ShapeDtypeStruct((N, D), F32), jax.ShapeDtypeStruct((N, D), BF16)] * 2,
        compiler_params=_params(("parallel",)),
        name="memkv",
    )(mem, g_mem, w_ck, w_cv)


def _mid_kernel(x_ref, da_ref, gd_ref, wo_ref, gc_ref, wq_ref, mk_ref, mv_ref, wco_ref, gf_ref, wr_ref, br_ref,
                x2_ref, hf_ref, comb_ref):
    x1 = x_ref[0] + _dot(da_ref[0], wo_ref[0:MIX_HALF, :]) + _dot(gd_ref[0], wo_ref[MIX_HALF:2 * MIX_HALF, :])
    hc = _rms(x1, gc_ref[...]).astype(BF16)
    q = (_dot(hc, wq_ref[...]) * (X_DH ** -0.5)).astype(BF16)
    heads = []
    for h in range(X_HEADS):
        sl = slice(h * X_DH, (h + 1) * X_DH)
        s = _dot_nt(q[:, sl], mk_ref[0, :, sl])
        p = jnp.exp(s - jnp.max(s, axis=-1, keepdims=True))
        l = jnp.sum(p, axis=-1, keepdims=True)
        heads.append((_dot(p.astype(BF16), mv_ref[0, :, sl]) / l).astype(BF16))
    x2 = x1 + _dot(jnp.concatenate(heads, axis=1), wco_ref[...])
    x2_ref[0] = x2
    hf = _rms(x2, gf_ref[...])
    hf_ref[0] = hf.astype(BF16)

    logits = jnp.dot(hf, wr_ref[...], preferred_element_type=F32, precision=lax.Precision.HIGHEST) + br_ref[...]
    lane_i = lax.broadcasted_iota(jnp.int32, logits.shape, 1)
    lane = lane_i.astype(F32)
    big = float(LANES)
    is_grp = (lane_i >= N_EXPERTS) & (lane_i < N_EXPERTS + N_GROUPS)
    gl = jnp.where(is_grp, logits, NEG_INF)
    gmax = jnp.max(gl, axis=-1, keepdims=True)
    g_top = 1.0 / jnp.sum(jnp.exp(gl - gmax), axis=-1, keepdims=True)
    g_idx = jnp.min(jnp.where(gl == gmax, lane, big), axis=-1, keepdims=True) - float(N_EXPERTS)
    in_grp = (lane_i < N_EXPERTS) & (jnp.floor(lane * (1.0 / E_PER_GROUP)) == g_idx)
    el = jnp.where(in_grp, logits, NEG_INF)
    e1 = jnp.max(el, axis=-1, keepdims=True)
    i1 = jnp.min(jnp.where(el == e1, lane, big), axis=-1, keepdims=True)
    el2 = jnp.where(lane == i1, NEG_INF, el)
    e2 = jnp.max(el2, axis=-1, keepdims=True)
    i2 = jnp.min(jnp.where(el2 == e2, lane, big), axis=-1, keepdims=True)
    r = jnp.exp(e2 - e1)
    w1 = g_top / (1.0 + r)
    w2 = g_top * r / (1.0 + r)
    comb_ref[0] = jnp.where(lane == i1, w1, jnp.where(lane == i2, w2, 0.0))


def _mid(x, mix_da, mix_gd, w_out, g_cross, w_cq, mk, mv, w_co, g_ffn, w_router, b_router, tm):
    B, T, D = x.shape
    M = mk.shape[1]
    blk = lambda w: pl.BlockSpec((1, tm, w), lambda b, t: (b, t, 0))
    mem = pl.BlockSpec((1, M, D), lambda b, t: (b, 0, 0))
    full = lambda a: pl.BlockSpec(a.shape, lambda b, t: (0,) * a.ndim)
    return pl.pallas_call(
        _mid_kernel,
        grid=(B, T // tm),
        in_specs=[blk(D), blk(MIX_HALF), blk(MIX_HALF), full(w_out), full(g_cross), full(w_cq), mem, mem,
                  full(w_co), full(g_ffn), full(w_router), full(b_router)],
        out_specs=[blk(D), blk(D), blk(LANES)],
        out_shape=[jax.ShapeDtypeStruct((B, T, D), F32), jax.ShapeDtypeStruct((B, T, D), BF16),
                   jax.ShapeDtypeStruct((B, T, LANES), F32)],
        compiler_params=_params(("parallel", "parallel")),
        name="mid",
    )(x, mix_da, mix_gd, w_out, g_cross, w_cq, mk, mv, w_co, g_ffn, w_router, b_router)


def _moe_kernel(x2_ref, hf_ref, comb_ref, wg_ref, wu_ref, wd_ref, gfin_ref, y_ref, acc_scr):
    e = pl.program_id(1)

    @pl.when(e == 0)
    def _():
        acc_scr[...] = x2_ref[...]

    hf = hf_ref[...]
    hg = _silu(_dot(hf, wg_ref[0])) * _dot(hf, wu_ref[0])
    comb = comb_ref[...]
    lane = lax.broadcasted_iota(jnp.int32, comb.shape, 1)
    ccol = jnp.sum(jnp.where(lane == e, comb, 0.0), axis=-1, keepdims=True)
    acc_scr[...] += _dot((hg * ccol).astype(BF16), wd_ref[0])

    @pl.when(e == N_EXPERTS - 1)
    def _():
        y_ref[...] = _rms(acc_scr[...], gfin_ref[...])


def _moe(x2, hf, comb, w_gate, w_up, w_down, g_final, tm):
    N, D = x2.shape
    blk = lambda w: pl.BlockSpec((tm, w), lambda i, e: (i, 0))
    return pl.pallas_call(
        _moe_kernel,
        grid=(N // tm, N_EXPERTS),
        in_specs=[blk(D), blk(D), blk(LANES),
                  pl.BlockSpec((1, D, D_EXPERT), lambda i, e: (e, 0, 0)),
                  pl.BlockSpec((1, D, D_EXPERT), lambda i, e: (e, 0, 0)),
                  pl.BlockSpec((1, D_EXPERT, D), lambda i, e: (e, 0, 0)),
                  pl.BlockSpec(g_final.shape, lambda i, e: (0, 0))],
        out_specs=blk(D),
        out_shape=jax.ShapeDtypeStruct((N, D), F32),
        scratch_shapes=[pltpu.VMEM((tm, D), F32)],
        compiler_params=_params(("parallel", "arbitrary")),
        name="moe",
    )(x2, hf, comb, w_gate, w_up, w_down, g_final)


def _pad_rows(a, rows):
    return jnp.pad(a, ((0, rows - a.shape[0]), (0, LANES - a.shape[1])))


def _trunk(x, mk16, mv16, past_k, past_v, conv_prev, s_prev, lam_init, wts):
    B, T, D = x.shape
    tm = min(T, 512)
    if conv_prev is None:
        conv_prev = jnp.zeros((B, CONV_W - 1, GDN_QKV), F32)
    if s_prev is None:
        s_prev = jnp.zeros((B, HEADS, HEAD_W, HEAD_W), F32)
    prev8 = jnp.pad(conv_prev, ((0, 0), (SUBLANES - (CONV_W - 1), 0), (0, 0)))
    (dq, dkf, dk16, dvf, dv16, cq, ck, cv, gz, gates, cnew) = _inproj(
        x, prev8, wts["g_mix"], wts["w_main"], wts["w_gate8"], wts["conv_w8"], wts["gate_par"], tm)
    if past_k is None:
        mix_da = _da_prompt(wts["lam_par"], wts["g_subln"], dq, dk16, dv16, min(T, 256), lam_init)
    else:
        P = past_k.shape[1]
        mix_da = _da_sample(wts["lam_par"], wts["g_subln"], dq, past_k.reshape(B, P, MIX_HALF),
                            past_v.reshape(B, P, MIX_HALF), dk16, dv16, lam_init)
    C = min(T, CHUNK)
    mix_gd, s_new = _gdn(cq, ck, cv, gz, gates, s_prev, wts["g_gdn_out"], C)
    x2, hf, comb = _mid(x, mix_da, mix_gd, wts["w_out"], wts["g_cross"], wts["w_cq"], mk16, mv16, wts["w_co"],
                        wts["g_ffn"], wts["w_router"], wts["b_router"], min(T, 256))
    N = B * T
    y = _moe(x2.reshape(N, D), hf.reshape(N, D), comb.reshape(N, LANES), wts["w_gate"], wts["w_up"],
             wts["w_down"], wts["g_final"], min(N, 1024))
    new_k = dkf.reshape(1, B, T, HEADS, HEAD_W)
    new_v = dvf.reshape(1, B, T, HEADS, HEAD_W)
    new_conv = cnew[:, SUBLANES - (CONV_W - 1):, :].reshape(1, B, CONV_W - 1, GDN_QKV)
    return y.reshape(B, T, D), new_k, new_v, new_conv, s_new[None]


def kernel(x_prompt, x_sample, cache_diff_k, cache_diff_v, state_gdn_conv, state_gdn, cache_mem_k, cache_mem_v,
           mem_prompt, g_mix, w_in, lam_q1, lam_k1, lam_q2, lam_k2, g_subln, conv_w, a_log, dt_bias, g_gdn_out,
           w_out, g_cross, g_mem, w_ck, w_cv, w_cq, w_co, g_ffn, w_grp, b_grp, w_rt, b_rt, w_gate, w_up, w_down,
           g_final):
    assert g_mix.shape[0] == 1, "single-layer model"
    l = 0
    lam_init = 0.8 - 0.6 * math.exp(-0.3 * l)
    gate_lo = 3 * MIX_HALF + GDN_QKV
    gate_hi = gate_lo + 2 * HEADS
    wi = w_in[l]
    row = lambda a: a.reshape(1, -1).astype(F32)
    wts = dict(
        g_mix=row(g_mix[l]),
        w_main=jnp.concatenate([wi[:, :gate_lo], wi[:, gate_hi:]], axis=1).astype(BF16),
        w_gate8=jnp.pad(wi[:, gate_lo:gate_hi], ((0, 0), (0, LANES - 2 * HEADS))).astype(BF16),
        conv_w8=jnp.pad(conv_w[l], ((0, SUBLANES - CONV_W), (0, 0))),
        gate_par=_pad_rows(jnp.stack([a_log[l], dt_bias[l]]), SUBLANES),
        lam_par=_pad_rows(jnp.stack([lam_q1[l], lam_k1[l], lam_q2[l], lam_k2[l]]), SUBLANES),
        g_subln=row(g_subln[l]), g_gdn_out=row(g_gdn_out[l]),
        w_out=w_out[l].astype(BF16), g_cross=row(g_cross[l]), w_cq=w_cq[l].astype(BF16), w_co=w_co[l].astype(BF16),
        g_ffn=row(g_ffn[l]),
        w_router=jnp.pad(jnp.concatenate([w_rt[l], w_grp[l]], axis=1),
                         ((0, 0), (0, LANES - N_EXPERTS - N_GROUPS))),
        b_router=jnp.pad(jnp.concatenate([b_rt[l], b_grp[l]]), (0, LANES - N_EXPERTS - N_GROUPS)).reshape(1, LANES),
        w_gate=w_gate[l].astype(BF16), w_up=w_up[l].astype(BF16), w_down=w_down[l].astype(BF16),
        g_final=row(g_final),
    )
    B, M, D = mem_prompt.shape
    mkf, mk16, mvf, mv16 = _memkv(mem_prompt.reshape(B * M, D), row(g_mem[l]), w_ck[l].astype(BF16),
                                  w_cv[l].astype(BF16), min(B * M, 512))
    yp, pk, pv, pc, ps = _trunk(x_prompt, mk16.reshape(B, M, D), mv16.reshape(B, M, D), None, None, None, None,
                                lam_init, wts)
    Bs = x_sample.shape[0]
    ys, sk, sv, sc, ss = _trunk(x_sample, cache_mem_k[l].reshape(Bs, M, D).astype(BF16),
                                cache_mem_v[l].reshape(Bs, M, D).astype(BF16), cache_diff_k[l], cache_diff_v[l],
                                state_gdn_conv[l], state_gdn[l], lam_init, wts)
    mem_shape = (1, B, M, X_HEADS, X_DH)
    return (yp, ys, pk, pv, pc, ps, mkf.reshape(mem_shape), mvf.reshape(mem_shape), sk, sv, sc, ss)
```

```python
import functools
import math

import jax
import jax.numpy as jnp
import numpy as np
from jax import lax
from jax.experimental import pallas as pl
from jax.experimental.pallas import tpu as pltpu

F32 = jnp.float32
BF16 = jnp.bfloat16

D_MODEL = 1024
CHUNK = 64
HEADS = 4
HEAD_W = 128
DA_DH = 64
MIX_HALF = HEADS * HEAD_W
GDN_QKV = 3 * MIX_HALF
CONV_W = 4
X_HEADS = 4
X_DH = 256
N_GROUPS = 4
E_PER_GROUP = 8
N_EXPERTS = 32
D_EXPERT = 256
NORM_EPS = 1e-6
NEG_INF = -1e30
LANES = 128
SUBLANES = 8
VMEM_LIMIT = 56 * 1024 * 1024


def _dot(a, b):
    return jnp.dot(a, b, preferred_element_type=F32)


def _dot_nt(a, b):
    return lax.dot_general(a, b, (((1,), (1,)), ((), ())), preferred_element_type=F32)


def _rms(x, g):
    return x * lax.rsqrt(jnp.mean(x * x, axis=-1, keepdims=True) + NORM_EPS) * g


def _sigmoid(x):
    return 1.0 / (1.0 + jnp.exp(-x))


def _silu(x):
    return x * _sigmoid(x)


def _split_bf16(a):
    hi = a.astype(BF16)
    lo = (a - hi.astype(F32)).astype(BF16)
    return hi, lo


def _dot3(a, b):
    ah, al = _split_bf16(a)
    bh, bl = _split_bf16(b)
    return _dot(ah, bh) + (_dot(ah, bl) + _dot(al, bh))


def _params(sem):
    return pltpu.CompilerParams(dimension_semantics=sem, vmem_limit_bytes=VMEM_LIMIT)


def _inproj_kernel(x_ref, prev_ref, gmix_ref, wm_ref, wg_ref, cw_ref, gp_ref,
                   dq_ref, dkf_ref, dkb_ref, dvf_ref, dvb_ref, cq_ref, ck_ref, cv_ref, gz_ref, gate_ref,
                   cnew_ref, xp_scr, *, tm):
    t = pl.program_id(1)
    h = _rms(x_ref[0], gmix_ref[...]).astype(BF16)

    dq = _dot(h, wm_ref[:, 0:MIX_HALF])
    dq_ref[0] = (dq * (DA_DH ** -0.5)).astype(BF16)
    dk = _dot(h, wm_ref[:, MIX_HALF:2 * MIX_HALF])
    dkf_ref[0] = dk
    dkb_ref[0] = dk.astype(BF16)
    dv = _dot(h, wm_ref[:, 2 * MIX_HALF:3 * MIX_HALF])
    dvf_ref[0] = dv
    dvb_ref[0] = dv.astype(BF16)
    gz_ref[0] = _dot(h, wm_ref[:, 3 * MIX_HALF + GDN_QKV:4 * MIX_HALF + GDN_QKV])

    gates = _dot(h, wg_ref[...])
    z = gates + gp_ref[1:2, :]
    softplus = jnp.maximum(z, 0.0) + jnp.log1p(jnp.exp(-jnp.abs(z)))
    g_all = -jnp.exp(gp_ref[0:1, :]) * softplus
    lane = lax.broadcasted_iota(jnp.int32, gates.shape, 1)
    gate_ref[0] = jnp.where(lane < HEADS, g_all, jnp.where(lane < 2 * HEADS, _sigmoid(gates), 0.0))

    @pl.when(t == 0)
    def _():
        xp_scr[0:SUBLANES, :] = prev_ref[0]

    xp_scr[SUBLANES:SUBLANES + tm, :] = _dot(h, wm_ref[:, 3 * MIX_HALF:3 * MIX_HALF + GDN_QKV])
    outs = (cq_ref, ck_ref, cv_ref)
    for part in range(3):
        cs = slice(part * MIX_HALF, (part + 1) * MIX_HALF)
        y = xp_scr[5:5 + tm, cs] * cw_ref[0:1, cs]
        for j in range(1, CONV_W):
            y = y + xp_scr[5 + j:5 + j + tm, cs] * cw_ref[j:j + 1, cs]
        c = _silu(y)
        if part == 2:
            outs[part][0] = c
        else:
            scale = (HEAD_W ** -0.5) if part == 0 else 1.0
            for hh in range(HEADS):
                ch = c[:, hh * HEAD_W:(hh + 1) * HEAD_W]
                n = ch * lax.rsqrt(jnp.sum(ch * ch, axis=-1, keepdims=True) + NORM_EPS)
                outs[part][0, :, hh * HEAD_W:(hh + 1) * HEAD_W] = n * scale if part == 0 else n
    carry = xp_scr[tm:tm + SUBLANES, :]
    xp_scr[0:SUBLANES, :] = carry
    cnew_ref[0] = carry


def _inproj(x, prev8, g_mix, w_main, w_gate, conv_w8, gate_par, tm):
    B, T, D = x.shape
    nt = T // tm
    row = lambda w, dt: jax.ShapeDtypeStruct((B, T, w), dt)
    blk = lambda w: pl.BlockSpec((1, tm, w), lambda b, t: (b, t, 0))
    full = lambda a: pl.BlockSpec(a.shape, lambda b, t: (0,) * a.ndim)
    return pl.pallas_call(
        functools.partial(_inproj_kernel, tm=tm),
        grid=(B, nt),
        in_specs=[blk(D), pl.BlockSpec((1, SUBLANES, GDN_QKV), lambda b, t: (b, 0, 0)),
                  full(g_mix), full(w_main), full(w_gate), full(conv_w8), full(gate_par)],
        out_specs=[blk(MIX_HALF)] * 9 + [blk(LANES), pl.BlockSpec((1, SUBLANES, GDN_QKV), lambda b, t: (b, 0, 0))],
        out_shape=[row(MIX_HALF, BF16), row(MIX_HALF, F32), row(MIX_HALF, BF16), row(MIX_HALF, F32),
                   row(MIX_HALF, BF16), row(MIX_HALF, F32), row(MIX_HALF, F32), row(MIX_HALF, F32),
                   row(MIX_HALF, F32), row(LANES, F32), jax.ShapeDtypeStruct((B, SUBLANES, GDN_QKV), F32)],
        scratch_shapes=[pltpu.VMEM((tm + SUBLANES, GDN_QKV), F32)],
        compiler_params=_params(("parallel", "arbitrary")),
        name="inproj",
    )(x, prev8, g_mix, w_main, w_gate, conv_w8, gate_par)


def _lam(lam_ref, lam_init):
    s1 = jnp.sum(lam_ref[0:1, :] * lam_ref[1:2, :], axis=-1, keepdims=True)
    s2 = jnp.sum(lam_ref[2:3, :] * lam_ref[3:4, :], axis=-1, keepdims=True)
    return jnp.exp(s1) - jnp.exp(s2) + lam_init


def _stack_q(q):
    lane = lax.broadcasted_iota(jnp.int32, q.shape, 1)
    zero = jnp.zeros_like(q)
    return jnp.concatenate([jnp.where(lane < DA_DH, q, zero), jnp.where(lane >= DA_DH, q, zero)], axis=0)


def _da_finish(acc, l, lam, gs, lam_init, tq):
    o = acc[:tq] / l[:tq] - lam * (acc[tq:] / l[tq:])
    return (_rms(o, gs) * (1.0 - lam_init)).astype(BF16)


def _da_prompt_kernel(lam_ref, gs_ref, q_ref, k_ref, v_ref, o_ref, vx_scr, m_scr, acc_scr, p_scr, al_scr, *,
                      tq, rows, lam_init):
    i = pl.program_id(2)

    @pl.when(i == 0)
    def _():
        vx_scr[:, 0:HEAD_W] = v_ref[0]
        vx_scr[:, HEAD_W:2 * HEAD_W] = jnp.ones((vx_scr.shape[0], HEAD_W), BF16)

    qz = _stack_q(q_ref[0])
    m_scr[...] = jnp.full(m_scr.shape, NEG_INF, F32)
    acc_scr[...] = jnp.zeros(acc_scr.shape, F32)
    n_lane_blocks = tq // LANES

    def scores(j, masked):
        slot = j % 2
        kj = k_ref[0, pl.ds(pl.multiple_of(j * tq, tq), tq), :]
        for r0 in range(0, 2 * tq, rows):
            s = _dot_nt(qz[r0:r0 + rows], kj)
            if masked:
                r = lax.broadcasted_iota(jnp.int32, s.shape, 0) + (r0 % tq)
                c = lax.broadcasted_iota(jnp.int32, s.shape, 1)
                s = jnp.where((r // CHUNK) >= (c // CHUNK), s, NEG_INF)
            m_prev = m_scr[r0:r0 + rows, :]
            m_new = jnp.maximum(m_prev, jnp.max(s, axis=-1, keepdims=True))
            al_scr[slot, r0:r0 + rows, :] = jnp.exp(m_prev - m_new)
            for c in range(n_lane_blocks):
                cs = slice(c * LANES, (c + 1) * LANES)
                p_scr[slot, r0:r0 + rows, cs] = jnp.exp(s[:, cs] - m_new).astype(BF16)
            m_scr[r0:r0 + rows, :] = m_new

    def accumulate(j):
        slot = j % 2
        vj = vx_scr[pl.ds(pl.multiple_of(j * tq, tq), tq), :]
        for r0 in range(0, 2 * tq, rows):
            alpha = al_scr[slot, r0:r0 + rows, :]
            acc_scr[r0:r0 + rows, :] = (jnp.concatenate([alpha, alpha], axis=1) * acc_scr[r0:r0 + rows, :]
                                        + _dot(p_scr[slot, r0:r0 + rows, :], vj))

    @pl.when(i == 0)
    def _():
        scores(0, True)

    @pl.when(i > 0)
    def _():
        scores(0, False)

        def body(j, carry):
            accumulate(j - 1)
            scores(j, False)
            return carry

        lax.fori_loop(1, i, body, 0)
        accumulate(i - 1)
        scores(i, True)

    accumulate(i)
    acc = acc_scr[...]
    o = acc[:, 0:HEAD_W] / acc[:, HEAD_W:2 * HEAD_W]
    o = o[0:tq] - _lam(lam_ref, lam_init) * o[tq:2 * tq]
    o_ref[0] = (_rms(o, gs_ref[...]) * (1.0 - lam_init)).astype(BF16)


def _da_prompt(lam_par, g_subln, q, k, v, tq, lam_init):
    B, T, _ = q.shape
    kv_spec = pl.BlockSpec((1, T, HEAD_W), lambda b, h, i: (b, 0, h))
    q_spec = pl.BlockSpec((1, tq, HEAD_W), lambda b, h, i: (b, i, h))
    full = lambda a: pl.BlockSpec(a.shape, lambda b, h, i: (0,) * a.ndim)
    return pl.pallas_call(
        functools.partial(_da_prompt_kernel, tq=tq, rows=min(2 * tq, 128), lam_init=lam_init),
        grid=(B, HEADS, T // tq),
        in_specs=[full(lam_par), full(g_subln), q_spec, kv_spec, kv_spec],
        out_specs=q_spec,
        out_shape=jax.ShapeDtypeStruct((B, T, MIX_HALF), BF16),
        scratch_shapes=[pltpu.VMEM((T, 2 * HEAD_W), BF16), pltpu.VMEM((2 * tq, LANES), F32),
                        pltpu.VMEM((2 * tq, 2 * HEAD_W), F32), pltpu.VMEM((2, 2 * tq, tq), BF16),
                        pltpu.VMEM((2, 2 * tq, LANES), F32)],
        compiler_params=_params(("parallel", "parallel", "arbitrary")),
        name="diffattn_prompt",
    )(lam_par, g_subln, q, k, v)


def _da_sample_kernel(lam_ref, gs_ref, q_ref, pk_ref, pv_ref, k_ref, v_ref, o_ref, *, tq, past, lam_init):
    qz = _stack_q(q_ref[0])
    s_old = _dot_nt(qz, pk_ref[0].astype(BF16))
    s_new = _dot_nt(qz, k_ref[0])

    def visible(shape, k_off):
        r = lax.broadcasted_iota(jnp.int32, shape, 0)
        c = lax.broadcasted_iota(jnp.int32, shape, 1) + k_off
        r = jnp.where(r >= tq, r - tq, r) + past
        return (r // CHUNK) >= (c // CHUNK)

    s_old = jnp.where(visible(s_old.shape, 0), s_old, NEG_INF)
    s_new = jnp.where(visible(s_new.shape, past), s_new, NEG_INF)
    m = jnp.maximum(jnp.max(s_old, axis=-1, keepdims=True), jnp.max(s_new, axis=-1, keepdims=True))
    p_old = jnp.exp(s_old - m)
    p_new = jnp.exp(s_new - m)
    l = jnp.sum(p_old, axis=-1, keepdims=True) + jnp.sum(p_new, axis=-1, keepdims=True)
    acc = _dot(p_old.astype(BF16), pv_ref[0].astype(BF16)) + _dot(p_new.astype(BF16), v_ref[0])
    o_ref[0] = _da_finish(acc, l, _lam(lam_ref, lam_init), gs_ref[...], lam_init, tq)


def _da_sample(lam_par, g_subln, q, past_k, past_v, k, v, lam_init):
    B, T, _ = q.shape
    P = past_k.shape[1]
    new_spec = pl.BlockSpec((1, T, HEAD_W), lambda b, h: (b, 0, h))
    past_spec = pl.BlockSpec((1, P, HEAD_W), lambda b, h: (b, 0, h))
    full = lambda a: pl.BlockSpec(a.shape, lambda b, h: (0,) * a.ndim)
    return pl.pallas_call(
        functools.partial(_da_sample_kernel, tq=T, past=P, lam_init=lam_init),
        grid=(B, HEADS),
        in_specs=[full(lam_par), full(g_subln), new_spec, past_spec, past_spec, new_spec, new_spec],
        out_specs=new_spec,
        out_shape=jax.ShapeDtypeStruct((B, T, MIX_HALF), BF16),
        compiler_params=_params(("parallel", "parallel")),
        name="diffattn_sample",
    )(lam_par, g_subln, q, past_k, past_v, k, v)


def _gdn_prep_kernel(q_ref, k_ref, v_ref, gate_ref, u_ref, w_ref, qd_ref, kt_ref, ai_ref, eg_ref, *, C, G):
    HC = HEADS * C
    row = lax.broadcasted_iota(jnp.int32, (C, C), 0)
    col = lax.broadcasted_iota(jnp.int32, (C, C), 1)
    causal = row >= col
    strict = row > col
    arow = lax.broadcasted_iota(jnp.int32, (C, HC), 0)
    acol = lax.broadcasted_iota(jnp.int32, (C, HC), 1)
    eye_all = (arow == (acol % C)).astype(F32)
    brow = lax.broadcasted_iota(jnp.int32, (HC, HC), 0)
    bcolm = lax.broadcasted_iota(jnp.int32, (HC, HC), 1)
    same_head = (brow // C) == (bcolm // C)
    prow = lax.broadcasted_iota(jnp.int32, (LANES, C), 0)
    pcol = lax.broadcasted_iota(jnp.int32, (LANES, C), 1)
    tri = jnp.where((prow >= pcol) & (prow < C), 1.0, 0.0).astype(F32)

    def block_diag(m_all):
        return jnp.where(same_head, jnp.concatenate([m_all] * HEADS, axis=0), jnp.zeros((), m_all.dtype))

    gcs, gcts, ps, xs = [], [], [], []
    for g in range(G):
        rs = slice(g * C, (g + 1) * C)
        gates = gate_ref[0, rs, :]
        gc_pad = jnp.dot(tri, gates, preferred_element_type=F32, precision=lax.Precision.HIGHEST)
        gcts.append(gc_pad.T)
        gcs.append(gc_pad[0:C, :])
        eg_ref[0, g * SUBLANES:(g + 1) * SUBLANES, :] = jnp.broadcast_to(
            jnp.exp(gc_pad[C - 1:C, :]), (SUBLANES, LANES))

    def decay_of(g, h):
        return jnp.exp(jnp.where(causal, gcs[g][:, h:h + 1] - gcts[g][h:h + 1, 0:C], NEG_INF))

    for g in range(G):
        rs = slice(g * C, (g + 1) * C)
        a_heads = []
        for h in range(HEADS):
            k = k_ref[0, rs, h * HEAD_W:(h + 1) * HEAD_W]
            kb = k * gate_ref[0, rs, HEADS + h:HEADS + h + 1]
            a_heads.append(jnp.where(strict, -(_dot_nt(kb.astype(BF16), k.astype(BF16)) * decay_of(g, h)), 0.0))
        ps.append(jnp.concatenate(a_heads, axis=1))
        xs.append(eye_all + ps[g])
    def dot3_split(a, b_diag):
        (ah, al), (bh, bl) = a, b_diag
        return _dot(ah, bh) + (_dot(ah, bl) + _dot(al, bh))

    p_split = [_split_bf16(ps[g]) for g in range(G)]
    p_diag = [tuple(block_diag(part) for part in p_split[g]) for g in range(G)]
    for _ in range(int(math.log2(C)) - 1):
        for g in range(G):
            p_split[g] = _split_bf16(dot3_split(p_split[g], p_diag[g]))
            p_diag[g] = tuple(block_diag(part) for part in p_split[g])
        for g in range(G):
            xs[g] = xs[g] + dot3_split(_split_bf16(xs[g]), p_diag[g])

    for g in range(G):
        rs = slice(g * C, (g + 1) * C)
        for h in range(HEADS):
            sl = slice(h * HEAD_W, (h + 1) * HEAD_W)
            q = q_ref[0, rs, sl]
            v = v_ref[0, rs, sl]
            k = k_ref[0, rs, sl]
            gcol = gcs[g][:, h:h + 1]
            bcol = gate_ref[0, rs, HEADS + h:HEADS + h + 1]
            glast = gcs[g][C - 1:C, h:h + 1]
            egc = jnp.exp(gcol)
            t16 = xs[g][:, h * C:(h + 1) * C].astype(BF16)
            uw = _dot(t16, jnp.concatenate([v * bcol, k * bcol * egc], axis=1).astype(BF16))
            u_ref[0, rs, sl] = uw[:, 0:HEAD_W]
            w_ref[0, rs, sl] = uw[:, HEAD_W:2 * HEAD_W].astype(BF16)
            qd_ref[0, rs, sl] = (q * egc).astype(BF16)
            kt_ref[0, rs, sl] = (k * jnp.exp(glast - gcol)).astype(BF16)
            ai_ref[0, rs, h * C:(h + 1) * C] = (_dot_nt(q.astype(BF16), k.astype(BF16)) * decay_of(g, h)).astype(BF16)


def _gdn_scan_kernel(u_ref, w_ref, qd_ref, kt_ref, ai_ref, eg_ref, gz_ref, s0_ref, gout_ref, o_ref, sfin_ref, s_scr,
                     *, C, GB, n_chunks):
    c = pl.program_id(1)

    @pl.when(c == 0)
    def _():
        s_scr[...] = s0_ref[...]

    chains = [(b, h) for b in range(GB) for h in range(HEADS)]
    cols = lambda h: slice(h * HEAD_W, (h + 1) * HEAD_W)
    s16 = {bh: s_scr[bh[0], bh[1]].astype(BF16) for bh in chains}
    v_new = {(b, h): (u_ref[b, :, cols(h)] - _dot(w_ref[b, :, cols(h)], s16[b, h])).astype(BF16) for b, h in chains}
    o = {(b, h): _dot(qd_ref[b, :, cols(h)], s16[b, h]) + _dot(ai_ref[b, :, h * C:(h + 1) * C], v_new[b, h])
         for b, h in chains}
    for b, h in chains:
        s_scr[b, h] = s_scr[b, h] * eg_ref[b, 0:1, h:h + 1] + lax.dot_general(
            kt_ref[b, :, cols(h)], v_new[b, h], (((0,), (0,)), ((), ())), preferred_element_type=F32)
    for b, h in chains:
        o_ref[b, :, cols(h)] = (_rms(o[b, h], gout_ref[...]) * _silu(gz_ref[b, :, cols(h)])).astype(BF16)

    @pl.when(c == n_chunks - 1)
    def _():
        sfin_ref[...] = s_scr[...]


def _gdn(q, k, v, gz, gates, s0, g_out, C):
    B, T, _ = q.shape
    nc = T // C
    G = min(nc, 4)
    GB = 2 if B % 2 == 0 else 1
    HC = HEADS * C
    pblk = lambda w: pl.BlockSpec((1, G * C, w), lambda b, c: (b, c, 0))
    row = lambda w, dt: jax.ShapeDtypeStruct((B, T, w), dt)
    u, w, qd, kt, ai, eg = pl.pallas_call(
        functools.partial(_gdn_prep_kernel, C=C, G=G),
        grid=(B, nc // G),
        in_specs=[pblk(MIX_HALF)] * 3 + [pblk(LANES)],
        out_specs=[pblk(MIX_HALF)] * 4 + [pblk(HC), pl.BlockSpec((1, G * SUBLANES, LANES), lambda b, c: (b, c, 0))],
        out_shape=[row(MIX_HALF, F32), row(MIX_HALF, BF16), row(MIX_HALF, BF16), row(MIX_HALF, BF16), row(HC, BF16),
                   jax.ShapeDtypeStruct((B, nc * SUBLANES, LANES), F32)],
        compiler_params=_params(("parallel", "parallel")),
        name="gdn_prep",
    )(q, k, v, gates)
    sblk = lambda w: pl.BlockSpec((GB, C, w), lambda b, c: (b, c, 0))
    st = pl.BlockSpec((GB, HEADS, HEAD_W, HEAD_W), lambda b, c: (b, 0, 0, 0))
    return pl.pallas_call(
        functools.partial(_gdn_scan_kernel, C=C, GB=GB, n_chunks=nc),
        grid=(B // GB, nc),
        in_specs=[sblk(MIX_HALF)] * 4 + [sblk(HC), pl.BlockSpec((GB, SUBLANES, LANES), lambda b, c: (b, c, 0)),
                                         sblk(MIX_HALF), st, pl.BlockSpec(g_out.shape, lambda b, c: (0, 0))],
        out_specs=[sblk(MIX_HALF), st],
        out_shape=[jax.ShapeDtypeStruct((B, T, MIX_HALF), BF16),
                   jax.ShapeDtypeStruct((B, HEADS, HEAD_W, HEAD_W), F32)],
        scratch_shapes=[pltpu.VMEM((GB, HEADS, HEAD_W, HEAD_W), F32)],
        compiler_params=_params(("parallel", "arbitrary")),
        name="gdn_scan",
    )(u, w, qd, kt, ai, eg, gz, s0, g_out)


def _memkv_kernel(m_ref, g_ref, wk_ref, wv_ref, kf_ref, kb_ref, vf_ref, vb_ref):
    m = _rms(m_ref[...], g_ref[...]).astype(BF16)
    mk = _dot(m, wk_ref[...])
    mv = _dot(m, wv_ref[...])
    kf_ref[...] = mk
    kb_ref[...] = mk.astype(BF16)
    vf_ref[...] = mv
    vb_ref[...] = mv.astype(BF16)


def _memkv(mem, g_mem, w_ck, w_cv, tm):
    N, D = mem.shape
    blk = pl.BlockSpec((tm, D), lambda i: (i, 0))
    full = lambda a: pl.BlockSpec(a.shape, lambda i: (0,) * a.ndim)
    return pl.pallas_call(
        _memkv_kernel,
        grid=(N // tm,),
        in_specs=[blk, full(g_mem), full(w_ck), full(w_cv)],
        out_specs=[blk] * 4,
        out_shape=[jax.ShapeDtypeStruct((N, D), F32), jax.ShapeDtypeStruct((N, D), BF16)] * 2,
        compiler_params=_params(("parallel",)),
        name="memkv",
    )(mem, g_mem, w_ck, w_cv)


def _mid_kernel(x_ref, da_ref, gd_ref, wo_ref, gc_ref, wq_ref, mk_ref, mv_ref, wco_ref, gf_ref, wr_ref, br_ref,
                x2_ref, hf_ref, comb_ref):
    x1 = x_ref[0] + _dot(da_ref[0], wo_ref[0:MIX_HALF, :]) + _dot(gd_ref[0], wo_ref[MIX_HALF:2 * MIX_HALF, :])
    hc = _rms(x1, gc_ref[...]).astype(BF16)
    q = (_dot(hc, wq_ref[...]) * (X_DH ** -0.5)).astype(BF16)
    heads = []
    for h in range(X_HEADS):
        sl = slice(h * X_DH, (h + 1) * X_DH)
        s = _dot_nt(q[:, sl], mk_ref[0, :, sl])
        p = jnp.exp(s - jnp.max(s, axis=-1, keepdims=True))
        l = jnp.sum(p, axis=-1, keepdims=True)
        heads.append((_dot(p.astype(BF16), mv_ref[0, :, sl]) / l).astype(BF16))
    x2 = x1 + _dot(jnp.concatenate(heads, axis=1), wco_ref[...])
    x2_ref[0] = x2
    hf = _rms(x2, gf_ref[...])
    hf_ref[0] = hf.astype(BF16)

    logits = jnp.dot(hf, wr_ref[...], preferred_element_type=F32, precision=lax.Precision.HIGHEST) + br_ref[...]
    lane_i = lax.broadcasted_iota(jnp.int32, logits.shape, 1)
    lane = lane_i.astype(F32)
    big = float(LANES)
    is_grp = (lane_i >= N_EXPERTS) & (lane_i < N_EXPERTS + N_GROUPS)
    gl = jnp.where(is_grp, logits, NEG_INF)
    gmax = jnp.max(gl, axis=-1, keepdims=True)
    g_top = 1.0 / jnp.sum(jnp.exp(gl - gmax), axis=-1, keepdims=True)
    g_idx = jnp.min(jnp.where(gl == gmax, lane, big), axis=-1, keepdims=True) - float(N_EXPERTS)
    in_grp = (lane_i < N_EXPERTS) & (jnp.floor(lane * (1.0 / E_PER_GROUP)) == g_idx)
    el = jnp.where(in_grp, logits, NEG_INF)
    e1 = jnp.max(el, axis=-1, keepdims=True)
    i1 = jnp.min(jnp.where(el == e1, lane, big), axis=-1, keepdims=True)
    el2 = jnp.where(lane == i1, NEG_INF, el)
    e2 = jnp.max(el2, axis=-1, keepdims=True)
    i2 = jnp.min(jnp.where(el2 == e2, lane, big), axis=-1, keepdims=True)
    r = jnp.exp(e2 - e1)
    w1 = g_top / (1.0 + r)
    w2 = g_top * r / (1.0 + r)
    comb_ref[0] = jnp.where(lane == i1, w1, jnp.where(lane == i2, w2, 0.0))


def _mid(x, mix_da, mix_gd, w_out, g_cross, w_cq, mk, mv, w_co, g_ffn, w_router, b_router, tm):
    B, T, D = x.shape
    M = mk.shape[1]
    blk = lambda w: pl.BlockSpec((1, tm, w), lambda b, t: (b, t, 0))
    mem = pl.BlockSpec((1, M, D), lambda b, t: (b, 0, 0))
    full = lambda a: pl.BlockSpec(a.shape, lambda b, t: (0,) * a.ndim)
    return pl.pallas_call(
        _mid_kernel,
        grid=(B, T // tm),
        in_specs=[blk(D), blk(MIX_HALF), blk(MIX_HALF), full(w_out), full(g_cross), full(w_cq), mem, mem,
                  full(w_co), full(g_ffn), full(w_router), full(b_router)],
        out_specs=[blk(D), blk(D), blk(LANES)],
        out_shape=[jax.ShapeDtypeStruct((B, T, D), F32), jax.ShapeDtypeStruct((B, T, D), BF16),
                   jax.ShapeDtypeStruct((B, T, LANES), F32)],
        compiler_params=_params(("parallel", "parallel")),
        name="mid",
    )(x, mix_da, mix_gd, w_out, g_cross, w_cq, mk, mv, w_co, g_ffn, w_router, b_router)


def _moe_kernel(x2_ref, hf_ref, comb_ref, wg_ref, wu_ref, wd_ref, gfin_ref, y_ref, acc_scr):
    e = pl.program_id(1)

    @pl.when(e == 0)
    def _():
        acc_scr[...] = x2_ref[...]

    hf = hf_ref[...]
    hg = _silu(_dot(hf, wg_ref[0])) * _dot(hf, wu_ref[0])
    comb = comb_ref[...]
    lane = lax.broadcasted_iota(jnp.int32, comb.shape, 1)
    ccol = jnp.sum(jnp.where(lane == e, comb, 0.0), axis=-1, keepdims=True)
    acc_scr[...] += _dot((hg * ccol).astype(BF16), wd_ref[0])

    @pl.when(e == N_EXPERTS - 1)
    def _():
        y_ref[...] = _rms(acc_scr[...], gfin_ref[...])


def _moe(x2, hf, comb, w_gate, w_up, w_down, g_final, tm):
    N, D = x2.shape
    blk = lambda w: pl.BlockSpec((tm, w), lambda i, e: (i, 0))
    return pl.pallas_call(
        _moe_kernel,
        grid=(N // tm, N_EXPERTS),
        in_specs=[blk(D), blk(D), blk(LANES),
                  pl.BlockSpec((1, D, D_EXPERT), lambda i, e: (e, 0, 0)),
                  pl.BlockSpec((1, D, D_EXPERT), lambda i, e: (e, 0, 0)),
                  pl.BlockSpec((1, D_EXPERT, D), lambda i, e: (e, 0, 0)),
                  pl.BlockSpec(g_final.shape, lambda i, e: (0, 0))],
        out_specs=blk(D),
        out_shape=jax.ShapeDtypeStruct((N, D), F32),
        scratch_shapes=[pltpu.VMEM((tm, D), F32)],
        compiler_params=_params(("parallel", "arbitrary")),
        name="moe",
    )(x2, hf, comb, w_gate, w_up, w_down, g_final)


def _pad_rows(a, rows):
    return jnp.pad(a, ((0, rows - a.shape[0]), (0, LANES - a.shape[1])))


def _trunk(x, mk16, mv16, past_k, past_v, conv_prev, s_prev, lam_init, wts):
    B, T, D = x.shape
    tm = min(T, 512)
    if conv_prev is None:
        conv_prev = jnp.zeros((B, CONV_W - 1, GDN_QKV), F32)
    if s_prev is None:
        s_prev = jnp.zeros((B, HEADS, HEAD_W, HEAD_W), F32)
    prev8 = jnp.pad(conv_prev, ((0, 0), (SUBLANES - (CONV_W - 1), 0), (0, 0)))
    (dq, dkf, dk16, dvf, dv16, cq, ck, cv, gz, gates, cnew) = _inproj(
        x, prev8, wts["g_mix"], wts["w_main"], wts["w_gate8"], wts["conv_w8"], wts["gate_par"], tm)
    if past_k is None:
        mix_da = _da_prompt(wts["lam_par"], wts["g_subln"], dq, dk16, dv16, min(T, 512), lam_init)
    else:
        P = past_k.shape[1]
        mix_da = _da_sample(wts["lam_par"], wts["g_subln"], dq, past_k.reshape(B, P, MIX_HALF),
                            past_v.reshape(B, P, MIX_HALF), dk16, dv16, lam_init)
    C = min(T, CHUNK)
    mix_gd, s_new = _gdn(cq, ck, cv, gz, gates, s_prev, wts["g_gdn_out"], C)
    x2, hf, comb = _mid(x, mix_da, mix_gd, wts["w_out"], wts["g_cross"], wts["w_cq"], mk16, mv16, wts["w_co"],
                        wts["g_ffn"], wts["w_router"], wts["b_router"], min(T, 256))
    N = B * T
    y = _moe(x2.reshape(N, D), hf.reshape(N, D), comb.reshape(N, LANES), wts["w_gate"], wts["w_up"],
             wts["w_down"], wts["g_final"], min(N, 1024))
    new_k = dkf.reshape(1, B, T, HEADS, HEAD_W)
    new_v = dvf.reshape(1, B, T, HEADS, HEAD_W)
    new_conv = cnew[:, SUBLANES - (CONV_W - 1):, :].reshape(1, B, CONV_W - 1, GDN_QKV)
    return y.reshape(B, T, D), new_k, new_v, new_conv, s_new[None]


def kernel(x_prompt, x_sample, cache_diff_k, cache_diff_v, state_gdn_conv, state_gdn, cache_mem_k, cache_mem_v,
           mem_prompt, g_mix, w_in, lam_q1, lam_k1, lam_q2, lam_k2, g_subln, conv_w, a_log, dt_bias, g_gdn_out,
           w_out, g_cross, g_mem, w_ck, w_cv, w_cq, w_co, g_ffn, w_grp, b_grp, w_rt, b_rt, w_gate, w_up, w_down,
           g_final):
    assert g_mix.shape[0] == 1, "single-layer model"
    l = 0
    lam_init = 0.8 - 0.6 * math.exp(-0.3 * l)
    gate_lo = 3 * MIX_HALF + GDN_QKV
    gate_hi = gate_lo + 2 * HEADS
    wi = w_in[l]
    row = lambda a: a.reshape(1, -1).astype(F32)
    wts = dict(
        g_mix=row(g_mix[l]),
        w_main=jnp.concatenate([wi[:, :gate_lo], wi[:, gate_hi:]], axis=1).astype(BF16),
        w_gate8=jnp.pad(wi[:, gate_lo:gate_hi], ((0, 0), (0, LANES - 2 * HEADS))).astype(BF16),
        conv_w8=jnp.pad(conv_w[l], ((0, SUBLANES - CONV_W), (0, 0))),
        gate_par=_pad_rows(jnp.stack([a_log[l], dt_bias[l]]), SUBLANES),
        lam_par=_pad_rows(jnp.stack([lam_q1[l], lam_k1[l], lam_q2[l], lam_k2[l]]), SUBLANES),
        g_subln=row(g_subln[l]), g_gdn_out=row(g_gdn_out[l]),
        w_out=w_out[l].astype(BF16), g_cross=row(g_cross[l]), w_cq=w_cq[l].astype(BF16), w_co=w_co[l].astype(BF16),
        g_ffn=row(g_ffn[l]),
        w_router=jnp.pad(jnp.concatenate([w_rt[l], w_grp[l]], axis=1),
                         ((0, 0), (0, LANES - N_EXPERTS - N_GROUPS))),
        b_router=jnp.pad(jnp.concatenate([b_rt[l], b_grp[l]]), (0, LANES - N_EXPERTS - N_GROUPS)).reshape(1, LANES),
        w_gate=w_gate[l].astype(BF16), w_up=w_up[l].astype(BF16), w_down=w_down[l].astype(BF16),
        g_final=row(g_final),
    )
    B, M, D = mem_prompt.shape
    mkf, mk16, mvf, mv16 = _memkv(mem_prompt.reshape(B * M, D), row(g_mem[l]), w_ck[l].astype(BF16),
                                  w_cv[l].astype(BF16), min(B * M, 512))
    yp, pk, pv, pc, ps = _trunk(x_prompt, mk16.reshape(B, M, D), mv16.reshape(B, M, D), None, None, None, None,
                                lam_init, wts)
    Bs = x_sample.shape[0]
    ys, sk, sv, sc, ss = _trunk(x_sample, cache_mem_k[l].reshape(Bs, M, D).astype(BF16),
                                cache_mem_v[l].reshape(Bs, M, D).astype(BF16), cache_diff_k[l], cache_diff_v[l],
                                state_gdn_conv[l], state_gdn[l], lam_init, wts)
    mem_shape = (1, B, M, X_HEADS, X_DH)
    return (yp, ys, pk, pv, pc, ps, mkf.reshape(mem_shape), mvf.reshape(mem_shape), sk, sv, sc, ss)
```

```python
import functools
import math

import jax
import jax.numpy as jnp
import numpy as np
from jax import lax
from jax.experimental import pallas as pl
from jax.experimental.pallas import tpu as pltpu

F32 = jnp.float32
BF16 = jnp.bfloat16

D_MODEL = 1024
CHUNK = 64
HEADS = 4
HEAD_W = 128
DA_DH = 64
MIX_HALF = HEADS * HEAD_W
GDN_QKV = 3 * MIX_HALF
CONV_W = 4
X_HEADS = 4
X_DH = 256
N_GROUPS = 4
E_PER_GROUP = 8
N_EXPERTS = 32
D_EXPERT = 256
NORM_EPS = 1e-6
NEG_INF = -1e30
LANES = 128
SUBLANES = 8
VMEM_LIMIT = 56 * 1024 * 1024


def _dot(a, b):
    return jnp.dot(a, b, preferred_element_type=F32)


def _dot_nt(a, b):
    return lax.dot_general(a, b, (((1,), (1,)), ((), ())), preferred_element_type=F32)


def _rms(x, g):
    return x * lax.rsqrt(jnp.mean(x * x, axis=-1, keepdims=True) + NORM_EPS) * g


def _sigmoid(x):
    return 1.0 / (1.0 + jnp.exp(-x))


def _silu(x):
    return x * _sigmoid(x)


def _split_bf16(a):
    hi = a.astype(BF16)
    lo = (a - hi.astype(F32)).astype(BF16)
    return hi, lo


def _dot3(a, b):
    ah, al = _split_bf16(a)
    bh, bl = _split_bf16(b)
    return _dot(ah, bh) + (_dot(ah, bl) + _dot(al, bh))


def _params(sem):
    return pltpu.CompilerParams(dimension_semantics=sem, vmem_limit_bytes=VMEM_LIMIT)


def _inproj_kernel(x_ref, prev_ref, gmix_ref, wm_ref, wg_ref, cw_ref, gp_ref,
                   dq_ref, dkf_ref, dkb_ref, dvf_ref, dvb_ref, cq_ref, ck_ref, cv_ref, gz_ref, gate_ref,
                   cnew_ref, xp_scr, *, tm):
    t = pl.program_id(1)
    h = _rms(x_ref[0], gmix_ref[...]).astype(BF16)

    dq = _dot(h, wm_ref[:, 0:MIX_HALF])
    dq_ref[0] = (dq * (DA_DH ** -0.5)).astype(BF16)
    dk = _dot(h, wm_ref[:, MIX_HALF:2 * MIX_HALF])
    dkf_ref[0] = dk
    dkb_ref[0] = dk.astype(BF16)
    dv = _dot(h, wm_ref[:, 2 * MIX_HALF:3 * MIX_HALF])
    dvf_ref[0] = dv
    dvb_ref[0] = dv.astype(BF16)
    gz_ref[0] = _dot(h, wm_ref[:, 3 * MIX_HALF + GDN_QKV:4 * MIX_HALF + GDN_QKV])

    gates = _dot(h, wg_ref[...])
    z = gates + gp_ref[1:2, :]
    softplus = jnp.maximum(z, 0.0) + jnp.log1p(jnp.exp(-jnp.abs(z)))
    g_all = -jnp.exp(gp_ref[0:1, :]) * softplus
    lane = lax.broadcasted_iota(jnp.int32, gates.shape, 1)
    gate_ref[0] = jnp.where(lane < HEADS, g_all, jnp.where(lane < 2 * HEADS, _sigmoid(gates), 0.0))

    @pl.when(t == 0)
    def _():
        xp_scr[0:SUBLANES, :] = prev_ref[0]

    xp_scr[SUBLANES:SUBLANES + tm, :] = _dot(h, wm_ref[:, 3 * MIX_HALF:3 * MIX_HALF + GDN_QKV])
    outs = (cq_ref, ck_ref, cv_ref)
    for part in range(3):
        cs = slice(part * MIX_HALF, (part + 1) * MIX_HALF)
        y = xp_scr[5:5 + tm, cs] * cw_ref[0:1, cs]
        for j in range(1, CONV_W):
            y = y + xp_scr[5 + j:5 + j + tm, cs] * cw_ref[j:j + 1, cs]
        c = _silu(y)
        if part == 2:
            outs[part][0] = c
        else:
            scale = (HEAD_W ** -0.5) if part == 0 else 1.0
            for hh in range(HEADS):
                ch = c[:, hh * HEAD_W:(hh + 1) * HEAD_W]
                n = ch * lax.rsqrt(jnp.sum(ch * ch, axis=-1, keepdims=True) + NORM_EPS)
                outs[part][0, :, hh * HEAD_W:(hh + 1) * HEAD_W] = n * scale if part == 0 else n
    carry = xp_scr[tm:tm + SUBLANES, :]
    xp_scr[0:SUBLANES, :] = carry
    cnew_ref[0] = carry


def _inproj(x, prev8, g_mix, w_main, w_gate, conv_w8, gate_par, tm):
    B, T, D = x.shape
    nt = T // tm
    row = lambda w, dt: jax.ShapeDtypeStruct((B, T, w), dt)
    blk = lambda w: pl.BlockSpec((1, tm, w), lambda b, t: (b, t, 0))
    full = lambda a: pl.BlockSpec(a.shape, lambda b, t: (0,) * a.ndim)
    return pl.pallas_call(
        functools.partial(_inproj_kernel, tm=tm),
        grid=(B, nt),
        in_specs=[blk(D), pl.BlockSpec((1, SUBLANES, GDN_QKV), lambda b, t: (b, 0, 0)),
                  full(g_mix), full(w_main), full(w_gate), full(conv_w8), full(gate_par)],
        out_specs=[blk(MIX_HALF)] * 9 + [blk(LANES), pl.BlockSpec((1, SUBLANES, GDN_QKV), lambda b, t: (b, 0, 0))],
        out_shape=[row(MIX_HALF, BF16), row(MIX_HALF, F32), row(MIX_HALF, BF16), row(MIX_HALF, F32),
                   row(MIX_HALF, BF16), row(MIX_HALF, F32), row(MIX_HALF, F32), row(MIX_HALF, F32),
                   row(MIX_HALF, F32), row(LANES, F32), jax.ShapeDtypeStruct((B, SUBLANES, GDN_QKV), F32)],
        scratch_shapes=[pltpu.VMEM((tm + SUBLANES, GDN_QKV), F32)],
        compiler_params=_params(("parallel", "arbitrary")),
        name="inproj",
    )(x, prev8, g_mix, w_main, w_gate, conv_w8, gate_par)


def _lam(lam_ref, lam_init):
    s1 = jnp.sum(lam_ref[0:1, :] * lam_ref[1:2, :], axis=-1, keepdims=True)
    s2 = jnp.sum(lam_ref[2:3, :] * lam_ref[3:4, :], axis=-1, keepdims=True)
    return jnp.exp(s1) - jnp.exp(s2) + lam_init


def _stack_q(q):
    lane = lax.broadcasted_iota(jnp.int32, q.shape, 1)
    zero = jnp.zeros_like(q)
    return jnp.concatenate([jnp.where(lane < DA_DH, q, zero), jnp.where(lane >= DA_DH, q, zero)], axis=0)


def _da_finish(acc, l, lam, gs, lam_init, tq):
    o = acc[:tq] / l[:tq] - lam * (acc[tq:] / l[tq:])
    return (_rms(o, gs) * (1.0 - lam_init)).astype(BF16)


def _da_prompt_kernel(lam_ref, gs_ref, q_ref, k_ref, v_ref, o_ref, vx_scr, m_scr, acc_scr, p_scr, al_scr, *,
                      tq, rows, lam_init):
    i = pl.program_id(2)

    @pl.when(i == 0)
    def _():
        vx_scr[:, 0:HEAD_W] = v_ref[0]
        vx_scr[:, HEAD_W:2 * HEAD_W] = jnp.ones((vx_scr.shape[0], HEAD_W), BF16)

    qz = _stack_q(q_ref[0])
    m_scr[...] = jnp.full(m_scr.shape, NEG_INF, F32)
    acc_scr[...] = jnp.zeros(acc_scr.shape, F32)
    n_lane_blocks = tq // LANES

    def scores(j, r0, masked):
        kj = k_ref[0, pl.ds(pl.multiple_of(j * tq, tq), tq), :]
        s = _dot_nt(qz[r0:r0 + rows], kj)
        if masked:
            r = lax.broadcasted_iota(jnp.int32, s.shape, 0) + (r0 % tq)
            c = lax.broadcasted_iota(jnp.int32, s.shape, 1)
            s = jnp.where((r // CHUNK) >= (c // CHUNK), s, NEG_INF)
        m_prev = m_scr[r0:r0 + rows, :]
        m_new = jnp.maximum(m_prev, jnp.max(s, axis=-1, keepdims=True))
        al_scr[r0:r0 + rows, :] = jnp.exp(m_prev - m_new)
        for c in range(n_lane_blocks):
            cs = slice(c * LANES, (c + 1) * LANES)
            p_scr[r0:r0 + rows, cs] = jnp.exp(s[:, cs] - m_new).astype(BF16)
        m_scr[r0:r0 + rows, :] = m_new

    def accumulate(j, r0):
        vj = vx_scr[pl.ds(pl.multiple_of(j * tq, tq), tq), :]
        alpha = al_scr[r0:r0 + rows, :]
        acc_scr[r0:r0 + rows, :] = (jnp.concatenate([alpha, alpha], axis=1) * acc_scr[r0:r0 + rows, :]
                                    + _dot(p_scr[r0:r0 + rows, :], vj))

    def stages(acc_j=None, score_j=None, masked=False):
        for r0 in range(0, 2 * tq, rows):
            if acc_j is not None:
                accumulate(acc_j, r0)
            if score_j is not None:
                scores(score_j, r0, masked)

    @pl.when(i == 0)
    def _():
        stages(score_j=0, masked=True)

    @pl.when(i > 0)
    def _():
        stages(score_j=0)

        def body(t, carry):
            stages(acc_j=t - 1, score_j=t)
            return carry

        lax.fori_loop(1, i, body, 0)
        stages(acc_j=i - 1, score_j=i, masked=True)

    stages(acc_j=i)
    acc = acc_scr[...]
    o = acc[:, 0:HEAD_W] / acc[:, HEAD_W:2 * HEAD_W]
    o = o[0:tq] - _lam(lam_ref, lam_init) * o[tq:2 * tq]
    o_ref[0] = (_rms(o, gs_ref[...]) * (1.0 - lam_init)).astype(BF16)


def _da_prompt(lam_par, g_subln, q, k, v, tq, lam_init):
    B, T, _ = q.shape
    kv_spec = pl.BlockSpec((1, T, HEAD_W), lambda b, h, i: (b, 0, h))
    q_spec = pl.BlockSpec((1, tq, HEAD_W), lambda b, h, i: (b, i, h))
    full = lambda a: pl.BlockSpec(a.shape, lambda b, h, i: (0,) * a.ndim)
    return pl.pallas_call(
        functools.partial(_da_prompt_kernel, tq=tq, rows=min(2 * tq, 128), lam_init=lam_init),
        grid=(B, HEADS, T // tq),
        in_specs=[full(lam_par), full(g_subln), q_spec, kv_spec, kv_spec],
        out_specs=q_spec,
        out_shape=jax.ShapeDtypeStruct((B, T, MIX_HALF), BF16),
        scratch_shapes=[pltpu.VMEM((T, 2 * HEAD_W), BF16), pltpu.VMEM((2 * tq, LANES), F32),
                        pltpu.VMEM((2 * tq, 2 * HEAD_W), F32), pltpu.VMEM((2 * tq, tq), BF16),
                        pltpu.VMEM((2 * tq, LANES), F32)],
        compiler_params=_params(("parallel", "parallel", "arbitrary")),
        name="diffattn_prompt",
    )(lam_par, g_subln, q, k, v)


def _da_sample_kernel(lam_ref, gs_ref, q_ref, pk_ref, pv_ref, k_ref, v_ref, o_ref, *, tq, past, lam_init):
    qz = _stack_q(q_ref[0])
    s_old = _dot_nt(qz, pk_ref[0].astype(BF16))
    s_new = _dot_nt(qz, k_ref[0])

    def visible(shape, k_off):
        r = lax.broadcasted_iota(jnp.int32, shape, 0)
        c = lax.broadcasted_iota(jnp.int32, shape, 1) + k_off
        r = jnp.where(r >= tq, r - tq, r) + past
        return (r // CHUNK) >= (c // CHUNK)

    s_old = jnp.where(visible(s_old.shape, 0), s_old, NEG_INF)
    s_new = jnp.where(visible(s_new.shape, past), s_new, NEG_INF)
    m = jnp.maximum(jnp.max(s_old, axis=-1, keepdims=True), jnp.max(s_new, axis=-1, keepdims=True))
    p_old = jnp.exp(s_old - m)
    p_new = jnp.exp(s_new - m)
    l = jnp.sum(p_old, axis=-1, keepdims=True) + jnp.sum(p_new, axis=-1, keepdims=True)
    acc = _dot(p_old.astype(BF16), pv_ref[0].astype(BF16)) + _dot(p_new.astype(BF16), v_ref[0])
    o_ref[0] = _da_finish(acc, l, _lam(lam_ref, lam_init), gs_ref[...], lam_init, tq)


def _da_sample(lam_par, g_subln, q, past_k, past_v, k, v, lam_init):
    B, T, _ = q.shape
    P = past_k.shape[1]
    new_spec = pl.BlockSpec((1, T, HEAD_W), lambda b, h: (b, 0, h))
    past_spec = pl.BlockSpec((1, P, HEAD_W), lambda b, h: (b, 0, h))
    full = lambda a: pl.BlockSpec(a.shape, lambda b, h: (0,) * a.ndim)
    return pl.pallas_call(
        functools.partial(_da_sample_kernel, tq=T, past=P, lam_init=lam_init),
        grid=(B, HEADS),
        in_specs=[full(lam_par), full(g_subln), new_spec, past_spec, past_spec, new_spec, new_spec],
        out_specs=new_spec,
        out_shape=jax.ShapeDtypeStruct((B, T, MIX_HALF), BF16),
        compiler_params=_params(("parallel", "parallel")),
        name="diffattn_sample",
    )(lam_par, g_subln, q, past_k, past_v, k, v)


def _gdn_prep_kernel(q_ref, k_ref, v_ref, gate_ref, u_ref, w_ref, qd_ref, kt_ref, ai_ref, eg_ref, *, C, G):
    HC = HEADS * C
    row = lax.broadcasted_iota(jnp.int32, (C, C), 0)
    col = lax.broadcasted_iota(jnp.int32, (C, C), 1)
    causal = row >= col
    strict = row > col
    arow = lax.broadcasted_iota(jnp.int32, (C, HC), 0)
    acol = lax.broadcasted_iota(jnp.int32, (C, HC), 1)
    eye_all = (arow == (acol % C)).astype(F32)
    brow = lax.broadcasted_iota(jnp.int32, (HC, HC), 0)
    bcolm = lax.broadcasted_iota(jnp.int32, (HC, HC), 1)
    same_head = (brow // C) == (bcolm // C)
    prow = lax.broadcasted_iota(jnp.int32, (LANES, C), 0)
    pcol = lax.broadcasted_iota(jnp.int32, (LANES, C), 1)
    tri = jnp.where((prow >= pcol) & (prow < C), 1.0, 0.0).astype(F32)

    def block_diag(m_all):
        return jnp.where(same_head, jnp.concatenate([m_all] * HEADS, axis=0), jnp.zeros((), m_all.dtype))

    gcs, gcts, ps, xs = [], [], [], []
    for g in range(G):
        rs = slice(g * C, (g + 1) * C)
        gates = gate_ref[0, rs, :]
        gc_pad = jnp.dot(tri, gates, preferred_element_type=F32, precision=lax.Precision.HIGHEST)
        gcts.append(gc_pad.T)
        gcs.append(gc_pad[0:C, :])
        eg_ref[0, g * SUBLANES:(g + 1) * SUBLANES, :] = jnp.broadcast_to(
            jnp.exp(gc_pad[C - 1:C, :]), (SUBLANES, LANES))

    def decay_of(g, h):
        return jnp.exp(jnp.where(causal, gcs[g][:, h:h + 1] - gcts[g][h:h + 1, 0:C], NEG_INF))

    for g in range(G):
        rs = slice(g * C, (g + 1) * C)
        a_heads = []
        for h in range(HEADS):
            k = k_ref[0, rs, h * HEAD_W:(h + 1) * HEAD_W]
            kb = k * gate_ref[0, rs, HEADS + h:HEADS + h + 1]
            a_heads.append(jnp.where(strict, -(_dot_nt(kb.astype(BF16), k.astype(BF16)) * decay_of(g, h)), 0.0))
        ps.append(jnp.concatenate(a_heads, axis=1))
        xs.append(eye_all + ps[g])
    def dot3_split(a, b_diag):
        (ah, al), (bh, bl) = a, b_diag
        return _dot(ah, bh) + (_dot(ah, bl) + _dot(al, bh))

    p_split = [_split_bf16(ps[g]) for g in range(G)]
    p_diag = [tuple(block_diag(part) for part in p_split[g]) for g in range(G)]
    for _ in range(int(math.log2(C)) - 1):
        for g in range(G):
            p_split[g] = _split_bf16(dot3_split(p_split[g], p_diag[g]))
            p_diag[g] = tuple(block_diag(part) for part in p_split[g])
        for g in range(G):
            xs[g] = xs[g] + dot3_split(_split_bf16(xs[g]), p_diag[g])

    for g in range(G):
        rs = slice(g * C, (g + 1) * C)
        for h in range(HEADS):
            sl = slice(h * HEAD_W, (h + 1) * HEAD_W)
            q = q_ref[0, rs, sl]
            v = v_ref[0, rs, sl]
            k = k_ref[0, rs, sl]
            gcol = gcs[g][:, h:h + 1]
            bcol = gate_ref[0, rs, HEADS + h:HEADS + h + 1]
            glast = gcs[g][C - 1:C, h:h + 1]
            egc = jnp.exp(gcol)
            t16 = xs[g][:, h * C:(h + 1) * C].astype(BF16)
            uw = _dot(t16, jnp.concatenate([v * bcol, k * bcol * egc], axis=1).astype(BF16))
            u_ref[0, rs, sl] = uw[:, 0:HEAD_W]
            w_ref[0, rs, sl] = uw[:, HEAD_W:2 * HEAD_W].astype(BF16)
            qd_ref[0, rs, sl] = (q * egc).astype(BF16)
            kt_ref[0, rs, sl] = (k * jnp.exp(glast - gcol)).astype(BF16)
            ai_ref[0, rs, h * C:(h + 1) * C] = (_dot_nt(q.astype(BF16), k.astype(BF16)) * decay_of(g, h)).astype(BF16)


def _gdn_scan_kernel(u_ref, w_ref, qd_ref, kt_ref, ai_ref, eg_ref, gz_ref, s0_ref, gout_ref, o_ref, sfin_ref, s_scr,
                     *, C, GB, n_chunks):
    c = pl.program_id(1)

    @pl.when(c == 0)
    def _():
        s_scr[...] = s0_ref[...]

    chains = [(b, h) for b in range(GB) for h in range(HEADS)]
    cols = lambda h: slice(h * HEAD_W, (h + 1) * HEAD_W)
    s16 = {bh: s_scr[bh[0], bh[1]].astype(BF16) for bh in chains}
    v_new = {(b, h): (u_ref[b, :, cols(h)] - _dot(w_ref[b, :, cols(h)], s16[b, h])).astype(BF16) for b, h in chains}
    o = {(b, h): _dot(qd_ref[b, :, cols(h)], s16[b, h]) + _dot(ai_ref[b, :, h * C:(h + 1) * C], v_new[b, h])
         for b, h in chains}
    for b, h in chains:
        s_scr[b, h] = s_scr[b, h] * eg_ref[b, 0:1, h:h + 1] + lax.dot_general(
            kt_ref[b, :, cols(h)], v_new[b, h], (((0,), (0,)), ((), ())), preferred_element_type=F32)
    for b, h in chains:
        o_ref[b, :, cols(h)] = (_rms(o[b, h], gout_ref[...]) * _silu(gz_ref[b, :, cols(h)])).astype(BF16)

    @pl.when(c == n_chunks - 1)
    def _():
        sfin_ref[...] = s_scr[...]


def _gdn(q, k, v, gz, gates, s0, g_out, C):
    B, T, _ = q.shape
    nc = T // C
    G = min(nc, 4)
    GB = 2 if B % 2 == 0 else 1
    HC = HEADS * C
    pblk = lambda w: pl.BlockSpec((1, G * C, w), lambda b, c: (b, c, 0))
    row = lambda w, dt: jax.ShapeDtypeStruct((B, T, w), dt)
    u, w, qd, kt, ai, eg = pl.pallas_call(
        functools.partial(_gdn_prep_kernel, C=C, G=G),
        grid=(B, nc // G),
        in_specs=[pblk(MIX_HALF)] * 3 + [pblk(LANES)],
        out_specs=[pblk(MIX_HALF)] * 4 + [pblk(HC), pl.BlockSpec((1, G * SUBLANES, LANES), lambda b, c: (b, c, 0))],
        out_shape=[row(MIX_HALF, F32), row(MIX_HALF, BF16), row(MIX_HALF, BF16), row(MIX_HALF, BF16), row(HC, BF16),
                   jax.ShapeDtypeStruct((B, nc * SUBLANES, LANES), F32)],
        compiler_params=_params(("parallel", "parallel")),
        name="gdn_prep",
    )(q, k, v, gates)
    sblk = lambda w: pl.BlockSpec((GB, C, w), lambda b, c: (b, c, 0))
    st = pl.BlockSpec((GB, HEADS, HEAD_W, HEAD_W), lambda b, c: (b, 0, 0, 0))
    return pl.pallas_call(
        functools.partial(_gdn_scan_kernel, C=C, GB=GB, n_chunks=nc),
        grid=(B // GB, nc),
        in_specs=[sblk(MIX_HALF)] * 4 + [sblk(HC), pl.BlockSpec((GB, SUBLANES, LANES), lambda b, c: (b, c, 0)),
                                         sblk(MIX_HALF), st, pl.BlockSpec(g_out.shape, lambda b, c: (0, 0))],
        out_specs=[sblk(MIX_HALF), st],
        out_shape=[jax.ShapeDtypeStruct((B, T, MIX_HALF), BF16),
                   jax.ShapeDtypeStruct((B, HEADS, HEAD_W, HEAD_W), F32)],
        scratch_shapes=[pltpu.VMEM((GB, HEADS, HEAD_W, HEAD_W), F32)],
        compiler_params=_params(("parallel", "arbitrary")),
        name="gdn_scan",
    )(u, w, qd, kt, ai, eg, gz, s0, g_out)


def _memkv_kernel(m_ref, g_ref, wk_ref, wv_ref, kf_ref, kb_ref, vf_ref, vb_ref):
    m = _rms(m_ref[...], g_ref[...]).astype(BF16)
    mk = _dot(m, wk_ref[...])
    mv = _dot(m, wv_ref[...])
    kf_ref[...] = mk
    kb_ref[...] = mk.astype(BF16)
    vf_ref[...] = mv
    vb_ref[...] = mv.astype(BF16)


def _memkv(mem, g_mem, w_ck, w_cv, tm):
    N, D = mem.shape
    blk = pl.BlockSpec((tm, D), lambda i: (i, 0))
    full = lambda a: pl.BlockSpec(a.shape, lambda i: (0,) * a.ndim)
    return pl.pallas_call(
        _memkv_kernel,
        grid=(N // tm,),
        in_specs=[blk, full(g_mem), full(w_ck), full(w_cv)],
        out_specs=[blk] * 4,
        out_shape=[jax.ShapeDtypeStruct((N, D), F32), jax.ShapeDtypeStruct((N, D), BF16)] * 2,
        compiler_params=_params(("parallel",)),
        name="memkv",
    )(mem, g_mem, w_ck, w_cv)


def _mid_kernel(x_ref, da_ref, gd_ref, wo_ref, gc_ref, wq_ref, mk_ref, mv_ref, wco_ref, gf_ref, wr_ref, br_ref,
                x2_ref, hf_ref, comb_ref):
    x1 = x_ref[0] + _dot(da_ref[0], wo_ref[0:MIX_HALF, :]) + _dot(gd_ref[0], wo_ref[MIX_HALF:2 * MIX_HALF, :])
    hc = _rms(x1, gc_ref[...]).astype(BF16)
    q = (_dot(hc, wq_ref[...]) * (X_DH ** -0.5)).astype(BF16)
    heads = []
    for h in range(X_HEADS):
        sl = slice(h * X_DH, (h + 1) * X_DH)
        s = _dot_nt(q[:, sl], mk_ref[0, :, sl])
        p = jnp.exp(s - jnp.max(s, axis=-1, keepdims=True))
        l = jnp.sum(p, axis=-1, keepdims=True)
        heads.append((_dot(p.astype(BF16), mv_ref[0, :, sl]) / l).astype(BF16))
    x2 = x1 + _dot(jnp.concatenate(heads, axis=1), wco_ref[...])
    x2_ref[0] = x2
    hf = _rms(x2, gf_ref[...])
    hf_ref[0] = hf.astype(BF16)

    logits = jnp.dot(hf, wr_ref[...], preferred_element_type=F32, precision=lax.Precision.HIGHEST) + br_ref[...]
    lane_i = lax.broadcasted_iota(jnp.int32, logits.shape, 1)
    lane = lane_i.astype(F32)
    big = float(LANES)
    is_grp = (lane_i >= N_EXPERTS) & (lane_i < N_EXPERTS + N_GROUPS)
    gl = jnp.where(is_grp, logits, NEG_INF)
    gmax = jnp.max(gl, axis=-1, keepdims=True)
    g_top = 1.0 / jnp.sum(jnp.exp(gl - gmax), axis=-1, keepdims=True)
    g_idx = jnp.min(jnp.where(gl == gmax, lane, big), axis=-1, keepdims=True) - float(N_EXPERTS)
    in_grp = (lane_i < N_EXPERTS) & (jnp.floor(lane * (1.0 / E_PER_GROUP)) == g_idx)
    el = jnp.where(in_grp, logits, NEG_INF)
    e1 = jnp.max(el, axis=-1, keepdims=True)
    i1 = jnp.min(jnp.where(el == e1, lane, big), axis=-1, keepdims=True)
    el2 = jnp.where(lane == i1, NEG_INF, el)
    e2 = jnp.max(el2, axis=-1, keepdims=True)
    i2 = jnp.min(jnp.where(el2 == e2, lane, big), axis=-1, keepdims=True)
    r = jnp.exp(e2 - e1)
    w1 = g_top / (1.0 + r)
    w2 = g_top * r / (1.0 + r)
    comb_ref[0] = jnp.where(lane == i1, w1, jnp.where(lane == i2, w2,
                                                      jnp.where(lane == g_idx + float(N_EXPERTS), 1.0, 0.0)))


def _mid(x, mix_da, mix_gd, w_out, g_cross, w_cq, mk, mv, w_co, g_ffn, w_router, b_router, tm):
    B, T, D = x.shape
    M = mk.shape[1]
    blk = lambda w: pl.BlockSpec((1, tm, w), lambda b, t: (b, t, 0))
    mem = pl.BlockSpec((1, M, D), lambda b, t: (b, 0, 0))
    full = lambda a: pl.BlockSpec(a.shape, lambda b, t: (0,) * a.ndim)
    return pl.pallas_call(
        _mid_kernel,
        grid=(B, T // tm),
        in_specs=[blk(D), blk(MIX_HALF), blk(MIX_HALF), full(w_out), full(g_cross), full(w_cq), mem, mem,
                  full(w_co), full(g_ffn), full(w_router), full(b_router)],
        out_specs=[blk(D), blk(D), blk(LANES)],
        out_shape=[jax.ShapeDtypeStruct((B, T, D), F32), jax.ShapeDtypeStruct((B, T, D), BF16),
                   jax.ShapeDtypeStruct((B, T, LANES), F32)],
        compiler_params=_params(("parallel", "parallel")),
        name="mid",
    )(x, mix_da, mix_gd, w_out, g_cross, w_cq, mk, mv, w_co, g_ffn, w_router, b_router)


def _moe_kernel(x2_ref, hf_ref, comb_ref, wg_ref, wu_ref, wd_ref, gfin_ref, y_ref,
                xg_scr, yg_scr, cs_scr, pt_scr, seg_smem, *, tm, sb):
    e = pl.program_id(1)
    lane = lax.broadcasted_iota(jnp.int32, (sb, LANES), 1)

    @pl.when(e == 0)
    def _():
        comb = comb_ref[...]
        lane_t = lax.broadcasted_iota(jnp.int32, comb.shape, 1)
        goh = jnp.where((lane_t >= N_EXPERTS) & (lane_t < N_EXPERTS + N_GROUPS), comb, 0.0)
        r = lax.broadcasted_iota(jnp.int32, (tm, tm), 0)
        c = lax.broadcasted_iota(jnp.int32, (tm, tm), 1)
        rank = _dot(jnp.where(r > c, 1.0, 0.0).astype(BF16), goh.astype(BF16))
        cnt = jnp.sum(goh, axis=0, keepdims=True)
        ur = lax.broadcasted_iota(jnp.int32, (LANES, LANES), 0)
        uc = lax.broadcasted_iota(jnp.int32, (LANES, LANES), 1)
        gbase = jnp.dot(jnp.broadcast_to(cnt, (SUBLANES, LANES)), jnp.where(ur < uc, 1.0, 0.0).astype(F32),
                        preferred_element_type=F32, precision=lax.Precision.HIGHEST)[0:1]
        for g in range(N_GROUPS + 1):
            seg_smem[g] = jnp.sum(jnp.where(lane_t[0:1] == N_EXPERTS + g, gbase, 0.0)).astype(jnp.int32)
        slot = jnp.sum(goh * (gbase + rank), axis=-1, keepdims=True)
        slot_row = jnp.broadcast_to(slot, (tm, LANES)).T[0:1, :]
        perm = jnp.where(r.astype(F32) == slot_row, 1.0, 0.0).astype(BF16)
        xg_scr[...] = _dot(perm, hf_ref[...]).astype(BF16)
        c_hi = comb.astype(BF16)
        c_mid = (comb - c_hi.astype(F32)).astype(BF16)
        c_lo = (comb - c_hi.astype(F32) - c_mid.astype(F32)).astype(BF16)
        cs_scr[...] = _dot(perm, c_hi) + (_dot(perm, c_mid) + _dot(perm, c_lo))
        pt_scr[...] = jnp.where(slot == c.astype(F32), 1.0, 0.0).astype(BF16)
        yg_scr[...] = jnp.zeros(yg_scr.shape, F32)

    g = lax.shift_right_logical(e, 3)
    start = seg_smem[g]
    end = seg_smem[g + 1]
    b0 = start // sb
    b1 = jnp.where(end > start, (end + sb - 1) // sb, b0)

    def body(i, carry):
        rows = pl.ds(pl.multiple_of(i * sb, sb), sb)
        xb = xg_scr[rows, :]
        hg = _silu(_dot(xb, wg_ref[0])) * _dot(xb, wu_ref[0])
        ccol = jnp.sum(jnp.where(lane == e, cs_scr[rows, :], 0.0), axis=-1, keepdims=True)
        yg_scr[rows, :] += _dot((hg * ccol).astype(BF16), wd_ref[0])
        return carry

    lax.fori_loop(b0, b1, body, 0)

    @pl.when(e == N_EXPERTS - 1)
    def _():
        yg = yg_scr[...]
        y_hi = yg.astype(BF16)
        y_lo = (yg - y_hi.astype(F32)).astype(BF16)
        y = x2_ref[...] + (_dot(pt_scr[...], y_hi) + _dot(pt_scr[...], y_lo))
        y_ref[...] = _rms(y, gfin_ref[...])


def _moe(x2, hf, comb, w_gate, w_up, w_down, g_final, tm):
    assert E_PER_GROUP == 8 and N_EXPERTS == N_GROUPS * E_PER_GROUP
    N, D = x2.shape
    blk = lambda w: pl.BlockSpec((tm, w), lambda i, e: (i, 0))
    return pl.pallas_call(
        functools.partial(_moe_kernel, tm=tm, sb=min(tm, 128)),
        grid=(N // tm, N_EXPERTS),
        in_specs=[blk(D), blk(D), blk(LANES),
                  pl.BlockSpec((1, D, D_EXPERT), lambda i, e: (e, 0, 0)),
                  pl.BlockSpec((1, D, D_EXPERT), lambda i, e: (e, 0, 0)),
                  pl.BlockSpec((1, D_EXPERT, D), lambda i, e: (e, 0, 0)),
                  pl.BlockSpec(g_final.shape, lambda i, e: (0, 0))],
        out_specs=blk(D),
        out_shape=jax.ShapeDtypeStruct((N, D), F32),
        scratch_shapes=[pltpu.VMEM((tm, D), BF16), pltpu.VMEM((tm, D), F32), pltpu.VMEM((tm, LANES), F32),
                        pltpu.VMEM((tm, tm), BF16), pltpu.SMEM((SUBLANES,), jnp.int32)],
        compiler_params=_params(("parallel", "arbitrary")),
        name="moe",
    )(x2, hf, comb, w_gate, w_up, w_down, g_final)


def _pad_rows(a, rows):
    return jnp.pad(a, ((0, rows - a.shape[0]), (0, LANES - a.shape[1])))


def _trunk(x, mk16, mv16, past_k, past_v, conv_prev, s_prev, lam_init, wts):
    B, T, D = x.shape
    tm = min(T, 512)
    if conv_prev is None:
        conv_prev = jnp.zeros((B, CONV_W - 1, GDN_QKV), F32)
    if s_prev is None:
        s_prev = jnp.zeros((B, HEADS, HEAD_W, HEAD_W), F32)
    prev8 = jnp.pad(conv_prev, ((0, 0), (SUBLANES - (CONV_W - 1), 0), (0, 0)))
    (dq, dkf, dk16, dvf, dv16, cq, ck, cv, gz, gates, cnew) = _inproj(
        x, prev8, wts["g_mix"], wts["w_main"], wts["w_gate8"], wts["conv_w8"], wts["gate_par"], tm)
    if past_k is None:
        mix_da = _da_prompt(wts["lam_par"], wts["g_subln"], dq, dk16, dv16, min(T, 512), lam_init)
    else:
        P = past_k.shape[1]
        mix_da = _da_sample(wts["lam_par"], wts["g_subln"], dq, past_k.reshape(B, P, MIX_HALF),
                            past_v.reshape(B, P, MIX_HALF), dk16, dv16, lam_init)
    C = min(T, CHUNK)
    mix_gd, s_new = _gdn(cq, ck, cv, gz, gates, s_prev, wts["g_gdn_out"], C)
    x2, hf, comb = _mid(x, mix_da, mix_gd, wts["w_out"], wts["g_cross"], wts["w_cq"], mk16, mv16, wts["w_co"],
                        wts["g_ffn"], wts["w_router"], wts["b_router"], min(T, 256))
    N = B * T
    y = _moe(x2.reshape(N, D), hf.reshape(N, D), comb.reshape(N, LANES), wts["w_gate"], wts["w_up"],
             wts["w_down"], wts["g_final"], min(N, 1024))
    new_k = dkf.reshape(1, B, T, HEADS, HEAD_W)
    new_v = dvf.reshape(1, B, T, HEADS, HEAD_W)
    new_conv = cnew[:, SUBLANES - (CONV_W - 1):, :].reshape(1, B, CONV_W - 1, GDN_QKV)
    return y.reshape(B, T, D), new_k, new_v, new_conv, s_new[None]


def kernel(x_prompt, x_sample, cache_diff_k, cache_diff_v, state_gdn_conv, state_gdn, cache_mem_k, cache_mem_v,
           mem_prompt, g_mix, w_in, lam_q1, lam_k1, lam_q2, lam_k2, g_subln, conv_w, a_log, dt_bias, g_gdn_out,
           w_out, g_cross, g_mem, w_ck, w_cv, w_cq, w_co, g_ffn, w_grp, b_grp, w_rt, b_rt, w_gate, w_up, w_down,
           g_final):
    assert g_mix.shape[0] == 1, "single-layer model"
    l = 0
    lam_init = 0.8 - 0.6 * math.exp(-0.3 * l)
    gate_lo = 3 * MIX_HALF + GDN_QKV
    gate_hi = gate_lo + 2 * HEADS
    wi = w_in[l]
    row = lambda a: a.reshape(1, -1).astype(F32)
    wts = dict(
        g_mix=row(g_mix[l]),
        w_main=jnp.concatenate([wi[:, :gate_lo], wi[:, gate_hi:]], axis=1).astype(BF16),
        w_gate8=jnp.pad(wi[:, gate_lo:gate_hi], ((0, 0), (0, LANES - 2 * HEADS))).astype(BF16),
        conv_w8=jnp.pad(conv_w[l], ((0, SUBLANES - CONV_W), (0, 0))),
        gate_par=_pad_rows(jnp.stack([a_log[l], dt_bias[l]]), SUBLANES),
        lam_par=_pad_rows(jnp.stack([lam_q1[l], lam_k1[l], lam_q2[l], lam_k2[l]]), SUBLANES),
        g_subln=row(g_subln[l]), g_gdn_out=row(g_gdn_out[l]),
        w_out=w_out[l].astype(BF16), g_cross=row(g_cross[l]), w_cq=w_cq[l].astype(BF16), w_co=w_co[l].astype(BF16),
        g_ffn=row(g_ffn[l]),
        w_router=jnp.pad(jnp.concatenate([w_rt[l], w_grp[l]], axis=1),
                         ((0, 0), (0, LANES - N_EXPERTS - N_GROUPS))),
        b_router=jnp.pad(jnp.concatenate([b_rt[l], b_grp[l]]), (0, LANES - N_EXPERTS - N_GROUPS)).reshape(1, LANES),
        w_gate=w_gate[l].astype(BF16), w_up=w_up[l].astype(BF16), w_down=w_down[l].astype(BF16),
        g_final=row(g_final),
    )
    B, M, D = mem_prompt.shape
    mkf, mk16, mvf, mv16 = _memkv(mem_prompt.reshape(B * M, D), row(g_mem[l]), w_ck[l].astype(BF16),
                                  w_cv[l].astype(BF16), min(B * M, 512))
    yp, pk, pv, pc, ps = _trunk(x_prompt, mk16.reshape(B, M, D), mv16.reshape(B, M, D), None, None, None, None,
                                lam_init, wts)
    Bs = x_sample.shape[0]
    ys, sk, sv, sc, ss = _trunk(x_sample, cache_mem_k[l].reshape(Bs, M, D).astype(BF16),
                                cache_mem_v[l].reshape(Bs, M, D).astype(BF16), cache_diff_k[l], cache_diff_v[l],
                                state_gdn_conv[l], state_gdn[l], lam_init, wts)
    mem_shape = (1, B, M, X_HEADS, X_DH)
    return (yp, ys, pk, pv, pc, ps, mkf.reshape(mem_shape), mvf.reshape(mem_shape), sk, sv, sc, ss)
```

```python
import functools
import math

import jax
import jax.numpy as jnp
import numpy as np
from jax import lax
from jax.experimental import pallas as pl
from jax.experimental.pallas import tpu as pltpu

F32 = jnp.float32
BF16 = jnp.bfloat16

D_MODEL = 1024
CHUNK = 64
HEADS = 4
HEAD_W = 128
DA_DH = 64
MIX_HALF = HEADS * HEAD_W
GDN_QKV = 3 * MIX_HALF
CONV_W = 4
X_HEADS = 4
X_DH = 256
N_GROUPS = 4
E_PER_GROUP = 8
N_EXPERTS = 32
D_EXPERT = 256
NORM_EPS = 1e-6
NEG_INF = -1e30
LANES = 128
SUBLANES = 8
VMEM_LIMIT = 56 * 1024 * 1024


def _dot(a, b):
    return jnp.dot(a, b, preferred_element_type=F32)


def _dot_nt(a, b):
    return lax.dot_general(a, b, (((1,), (1,)), ((), ())), preferred_element_type=F32)


def _rms(x, g):
    return x * lax.rsqrt(jnp.mean(x * x, axis=-1, keepdims=True) + NORM_EPS) * g


def _sigmoid(x):
    return 1.0 / (1.0 + jnp.exp(-x))


def _silu(x):
    return x * _sigmoid(x)


def _split_bf16(a):
    hi = a.astype(BF16)
    lo = (a - hi.astype(F32)).astype(BF16)
    return hi, lo


def _dot3(a, b):
    ah, al = _split_bf16(a)
    bh, bl = _split_bf16(b)
    return _dot(ah, bh) + (_dot(ah, bl) + _dot(al, bh))


def _params(sem):
    return pltpu.CompilerParams(dimension_semantics=sem, vmem_limit_bytes=VMEM_LIMIT)


def _inproj_kernel(x_ref, prev_ref, gmix_ref, wm_ref, wg_ref, cw_ref, gp_ref,
                   dq_ref, dkf_ref, dkb_ref, dvf_ref, dvb_ref, cq_ref, ck_ref, cv_ref, gz_ref, gate_ref,
                   cnew_ref, xp_scr, *, tm):
    t = pl.program_id(1)
    h = _rms(x_ref[0], gmix_ref[...]).astype(BF16)

    dq = _dot(h, wm_ref[:, 0:MIX_HALF])
    dq_ref[0] = (dq * (DA_DH ** -0.5)).astype(BF16)
    dk = _dot(h, wm_ref[:, MIX_HALF:2 * MIX_HALF])
    dkf_ref[0] = dk
    dkb_ref[0] = dk.astype(BF16)
    dv = _dot(h, wm_ref[:, 2 * MIX_HALF:3 * MIX_HALF])
    dvf_ref[0] = dv
    dvb_ref[0] = dv.astype(BF16)
    gz_ref[0] = _dot(h, wm_ref[:, 3 * MIX_HALF + GDN_QKV:4 * MIX_HALF + GDN_QKV])

    gates = _dot(h, wg_ref[...])
    z = gates + gp_ref[1:2, :]
    softplus = jnp.maximum(z, 0.0) + jnp.log1p(jnp.exp(-jnp.abs(z)))
    g_all = -jnp.exp(gp_ref[0:1, :]) * softplus
    lane = lax.broadcasted_iota(jnp.int32, gates.shape, 1)
    gate_ref[0] = jnp.where(lane < HEADS, g_all, jnp.where(lane < 2 * HEADS, _sigmoid(gates), 0.0))

    @pl.when(t == 0)
    def _():
        xp_scr[0:SUBLANES, :] = prev_ref[0]

    xp_scr[SUBLANES:SUBLANES + tm, :] = _dot(h, wm_ref[:, 3 * MIX_HALF:3 * MIX_HALF + GDN_QKV])
    outs = (cq_ref, ck_ref, cv_ref)
    for part in range(3):
        cs = slice(part * MIX_HALF, (part + 1) * MIX_HALF)
        y = xp_scr[5:5 + tm, cs] * cw_ref[0:1, cs]
        for j in range(1, CONV_W):
            y = y + xp_scr[5 + j:5 + j + tm, cs] * cw_ref[j:j + 1, cs]
        c = _silu(y)
        if part == 2:
            outs[part][0] = c
        else:
            scale = (HEAD_W ** -0.5) if part == 0 else 1.0
            for hh in range(HEADS):
                ch = c[:, hh * HEAD_W:(hh + 1) * HEAD_W]
                n = ch * lax.rsqrt(jnp.sum(ch * ch, axis=-1, keepdims=True) + NORM_EPS)
                outs[part][0, :, hh * HEAD_W:(hh + 1) * HEAD_W] = n * scale if part == 0 else n
    carry = xp_scr[tm:tm + SUBLANES, :]
    xp_scr[0:SUBLANES, :] = carry
    cnew_ref[0] = carry


def _inproj(x, prev8, g_mix, w_main, w_gate, conv_w8, gate_par, tm):
    B, T, D = x.shape
    nt = T // tm
    row = lambda w, dt: jax.ShapeDtypeStruct((B, T, w), dt)
    blk = lambda w: pl.BlockSpec((1, tm, w), lambda b, t: (b, t, 0))
    full = lambda a: pl.BlockSpec(a.shape, lambda b, t: (0,) * a.ndim)
    return pl.pallas_call(
        functools.partial(_inproj_kernel, tm=tm),
        grid=(B, nt),
        in_specs=[blk(D), pl.BlockSpec((1, SUBLANES, GDN_QKV), lambda b, t: (b, 0, 0)),
                  full(g_mix), full(w_main), full(w_gate), full(conv_w8), full(gate_par)],
        out_specs=[blk(MIX_HALF)] * 9 + [blk(LANES), pl.BlockSpec((1, SUBLANES, GDN_QKV), lambda b, t: (b, 0, 0))],
        out_shape=[row(MIX_HALF, BF16), row(MIX_HALF, F32), row(MIX_HALF, BF16), row(MIX_HALF, F32),
                   row(MIX_HALF, BF16), row(MIX_HALF, F32), row(MIX_HALF, F32), row(MIX_HALF, F32),
                   row(MIX_HALF, F32), row(LANES, F32), jax.ShapeDtypeStruct((B, SUBLANES, GDN_QKV), F32)],
        scratch_shapes=[pltpu.VMEM((tm + SUBLANES, GDN_QKV), F32)],
        compiler_params=_params(("parallel", "arbitrary")),
        name="inproj",
    )(x, prev8, g_mix, w_main, w_gate, conv_w8, gate_par)


def _lam(lam_ref, lam_init):
    s1 = jnp.sum(lam_ref[0:1, :] * lam_ref[1:2, :], axis=-1, keepdims=True)
    s2 = jnp.sum(lam_ref[2:3, :] * lam_ref[3:4, :], axis=-1, keepdims=True)
    return jnp.exp(s1) - jnp.exp(s2) + lam_init


def _stack_q(q):
    lane = lax.broadcasted_iota(jnp.int32, q.shape, 1)
    zero = jnp.zeros_like(q)
    return jnp.concatenate([jnp.where(lane < DA_DH, q, zero), jnp.where(lane >= DA_DH, q, zero)], axis=0)


def _da_finish(acc, l, lam, gs, lam_init, tq):
    o = acc[:tq] / l[:tq] - lam * (acc[tq:] / l[tq:])
    return (_rms(o, gs) * (1.0 - lam_init)).astype(BF16)


def _da_prompt_kernel(lam_ref, gs_ref, q_ref, k_ref, v_ref, o_ref, vx_scr, m_scr, acc_scr, p_scr, al_scr, *,
                      tq, rows, lam_init):
    i = pl.program_id(2)

    @pl.when(i == 0)
    def _():
        vx_scr[:, 0:HEAD_W] = v_ref[0]
        vx_scr[:, HEAD_W:2 * HEAD_W] = jnp.ones((vx_scr.shape[0], HEAD_W), BF16)

    qz = _stack_q(q_ref[0])
    m_scr[...] = jnp.full(m_scr.shape, NEG_INF, F32)
    acc_scr[...] = jnp.zeros(acc_scr.shape, F32)
    n_lane_blocks = tq // LANES

    def scores(j, r0, masked):
        kj = k_ref[0, pl.ds(pl.multiple_of(j * tq, tq), tq), :]
        s = _dot_nt(qz[r0:r0 + rows], kj)
        if masked:
            r = lax.broadcasted_iota(jnp.int32, s.shape, 0) + (r0 % tq)
            c = lax.broadcasted_iota(jnp.int32, s.shape, 1)
            s = jnp.where((r // CHUNK) >= (c // CHUNK), s, NEG_INF)
        m_prev = m_scr[r0:r0 + rows, :]
        m_new = jnp.maximum(m_prev, jnp.max(s, axis=-1, keepdims=True))
        al_scr[r0:r0 + rows, :] = jnp.exp(m_prev - m_new)
        for c in range(n_lane_blocks):
            cs = slice(c * LANES, (c + 1) * LANES)
            p_scr[r0:r0 + rows, cs] = jnp.exp(s[:, cs] - m_new).astype(BF16)
        m_scr[r0:r0 + rows, :] = m_new

    def accumulate(j, r0):
        vj = vx_scr[pl.ds(pl.multiple_of(j * tq, tq), tq), :]
        alpha = al_scr[r0:r0 + rows, :]
        acc_scr[r0:r0 + rows, :] = (jnp.concatenate([alpha, alpha], axis=1) * acc_scr[r0:r0 + rows, :]
                                    + _dot(p_scr[r0:r0 + rows, :], vj))

    def stages(acc_j=None, score_j=None, masked=False):
        for r0 in range(0, 2 * tq, rows):
            if acc_j is not None:
                accumulate(acc_j, r0)
            if score_j is not None:
                scores(score_j, r0, masked)

    @pl.when(i == 0)
    def _():
        stages(score_j=0, masked=True)

    @pl.when(i > 0)
    def _():
        stages(score_j=0)

        def body(t, carry):
            stages(acc_j=t - 1, score_j=t)
            return carry

        lax.fori_loop(1, i, body, 0)
        stages(acc_j=i - 1, score_j=i, masked=True)

    stages(acc_j=i)
    acc = acc_scr[...]
    o = acc[:, 0:HEAD_W] / acc[:, HEAD_W:2 * HEAD_W]
    o = o[0:tq] - _lam(lam_ref, lam_init) * o[tq:2 * tq]
    o_ref[0] = (_rms(o, gs_ref[...]) * (1.0 - lam_init)).astype(BF16)


def _da_prompt(lam_par, g_subln, q, k, v, tq, lam_init):
    B, T, _ = q.shape
    kv_spec = pl.BlockSpec((1, T, HEAD_W), lambda b, h, i: (b, 0, h))
    q_spec = pl.BlockSpec((1, tq, HEAD_W), lambda b, h, i: (b, i, h))
    full = lambda a: pl.BlockSpec(a.shape, lambda b, h, i: (0,) * a.ndim)
    return pl.pallas_call(
        functools.partial(_da_prompt_kernel, tq=tq, rows=min(2 * tq, 128), lam_init=lam_init),
        grid=(B, HEADS, T // tq),
        in_specs=[full(lam_par), full(g_subln), q_spec, kv_spec, kv_spec],
        out_specs=q_spec,
        out_shape=jax.ShapeDtypeStruct((B, T, MIX_HALF), BF16),
        scratch_shapes=[pltpu.VMEM((T, 2 * HEAD_W), BF16), pltpu.VMEM((2 * tq, LANES), F32),
                        pltpu.VMEM((2 * tq, 2 * HEAD_W), F32), pltpu.VMEM((2 * tq, tq), BF16),
                        pltpu.VMEM((2 * tq, LANES), F32)],
        compiler_params=_params(("parallel", "parallel", "arbitrary")),
        name="diffattn_prompt",
    )(lam_par, g_subln, q, k, v)


def _da_sample_kernel(lam_ref, gs_ref, q_ref, pk_ref, pv_ref, k_ref, v_ref, o_ref, *, tq, past, lam_init):
    qz = _stack_q(q_ref[0])
    s_old = _dot_nt(qz, pk_ref[0].astype(BF16))
    s_new = _dot_nt(qz, k_ref[0])

    def visible(shape, k_off):
        r = lax.broadcasted_iota(jnp.int32, shape, 0)
        c = lax.broadcasted_iota(jnp.int32, shape, 1) + k_off
        r = jnp.where(r >= tq, r - tq, r) + past
        return (r // CHUNK) >= (c // CHUNK)

    s_old = jnp.where(visible(s_old.shape, 0), s_old, NEG_INF)
    s_new = jnp.where(visible(s_new.shape, past), s_new, NEG_INF)
    m = jnp.maximum(jnp.max(s_old, axis=-1, keepdims=True), jnp.max(s_new, axis=-1, keepdims=True))
    p_old = jnp.exp(s_old - m)
    p_new = jnp.exp(s_new - m)
    l = jnp.sum(p_old, axis=-1, keepdims=True) + jnp.sum(p_new, axis=-1, keepdims=True)
    acc = _dot(p_old.astype(BF16), pv_ref[0].astype(BF16)) + _dot(p_new.astype(BF16), v_ref[0])
    o_ref[0] = _da_finish(acc, l, _lam(lam_ref, lam_init), gs_ref[...], lam_init, tq)


def _da_sample(lam_par, g_subln, q, past_k, past_v, k, v, lam_init):
    B, T, _ = q.shape
    P = past_k.shape[1]
    new_spec = pl.BlockSpec((1, T, HEAD_W), lambda b, h: (b, 0, h))
    past_spec = pl.BlockSpec((1, P, HEAD_W), lambda b, h: (b, 0, h))
    full = lambda a: pl.BlockSpec(a.shape, lambda b, h: (0,) * a.ndim)
    return pl.pallas_call(
        functools.partial(_da_sample_kernel, tq=T, past=P, lam_init=lam_init),
        grid=(B, HEADS),
        in_specs=[full(lam_par), full(g_subln), new_spec, past_spec, past_spec, new_spec, new_spec],
        out_specs=new_spec,
        out_shape=jax.ShapeDtypeStruct((B, T, MIX_HALF), BF16),
        compiler_params=_params(("parallel", "parallel")),
        name="diffattn_sample",
    )(lam_par, g_subln, q, past_k, past_v, k, v)


def _gdn_prep_kernel(q_ref, k_ref, v_ref, gate_ref, u_ref, w_ref, qd_ref, kt_ref, ai_ref, eg_ref, *, C, G):
    HC = HEADS * C
    row = lax.broadcasted_iota(jnp.int32, (C, C), 0)
    col = lax.broadcasted_iota(jnp.int32, (C, C), 1)
    causal = row >= col
    strict = row > col
    arow = lax.broadcasted_iota(jnp.int32, (C, HC), 0)
    acol = lax.broadcasted_iota(jnp.int32, (C, HC), 1)
    eye_all = (arow == (acol % C)).astype(F32)
    brow = lax.broadcasted_iota(jnp.int32, (HC, HC), 0)
    bcolm = lax.broadcasted_iota(jnp.int32, (HC, HC), 1)
    same_head = (brow // C) == (bcolm // C)
    prow = lax.broadcasted_iota(jnp.int32, (LANES, C), 0)
    pcol = lax.broadcasted_iota(jnp.int32, (LANES, C), 1)
    tri = jnp.where((prow >= pcol) & (prow < C), 1.0, 0.0).astype(F32)

    def block_diag(m_all):
        return jnp.where(same_head, jnp.concatenate([m_all] * HEADS, axis=0), jnp.zeros((), m_all.dtype))

    gcs, gcts, ps, xs = [], [], [], []
    for g in range(G):
        rs = slice(g * C, (g + 1) * C)
        gates = gate_ref[0, rs, :]
        gc_pad = jnp.dot(tri, gates, preferred_element_type=F32, precision=lax.Precision.HIGHEST)
        gcts.append(gc_pad.T)
        gcs.append(gc_pad[0:C, :])
        eg_ref[0, g * SUBLANES:(g + 1) * SUBLANES, :] = jnp.broadcast_to(
            jnp.exp(gc_pad[C - 1:C, :]), (SUBLANES, LANES))

    def decay_of(g, h):
        return jnp.exp(jnp.where(causal, gcs[g][:, h:h + 1] - gcts[g][h:h + 1, 0:C], NEG_INF))

    for g in range(G):
        rs = slice(g * C, (g + 1) * C)
        a_heads = []
        for h in range(HEADS):
            k = k_ref[0, rs, h * HEAD_W:(h + 1) * HEAD_W]
            kb = k * gate_ref[0, rs, HEADS + h:HEADS + h + 1]
            a_heads.append(jnp.where(strict, -(_dot_nt(kb.astype(BF16), k.astype(BF16)) * decay_of(g, h)), 0.0))
        ps.append(jnp.concatenate(a_heads, axis=1))
        xs.append(eye_all + ps[g])
    def dot3_split(a, b_diag):
        (ah, al), (bh, bl) = a, b_diag
        return _dot(ah, bh) + (_dot(ah, bl) + _dot(al, bh))

    p_split = [_split_bf16(ps[g]) for g in range(G)]
    p_diag = [tuple(block_diag(part) for part in p_split[g]) for g in range(G)]
    for _ in range(int(math.log2(C)) - 1):
        for g in range(G):
            p_split[g] = _split_bf16(dot3_split(p_split[g], p_diag[g]))
            p_diag[g] = tuple(block_diag(part) for part in p_split[g])
        for g in range(G):
            xs[g] = xs[g] + dot3_split(_split_bf16(xs[g]), p_diag[g])

    for g in range(G):
        rs = slice(g * C, (g + 1) * C)
        for h in range(HEADS):
            sl = slice(h * HEAD_W, (h + 1) * HEAD_W)
            q = q_ref[0, rs, sl]
            v = v_ref[0, rs, sl]
            k = k_ref[0, rs, sl]
            gcol = gcs[g][:, h:h + 1]
            bcol = gate_ref[0, rs, HEADS + h:HEADS + h + 1]
            glast = gcs[g][C - 1:C, h:h + 1]
            egc = jnp.exp(gcol)
            t16 = xs[g][:, h * C:(h + 1) * C].astype(BF16)
            uw = _dot(t16, jnp.concatenate([v * bcol, k * bcol * egc], axis=1).astype(BF16))
            u_ref[0, rs, sl] = uw[:, 0:HEAD_W]
            w_ref[0, rs, sl] = uw[:, HEAD_W:2 * HEAD_W].astype(BF16)
            qd_ref[0, rs, sl] = (q * egc).astype(BF16)
            kt_ref[0, rs, sl] = (k * jnp.exp(glast - gcol)).astype(BF16)
            ai_ref[0, rs, h * C:(h + 1) * C] = (_dot_nt(q.astype(BF16), k.astype(BF16)) * decay_of(g, h)).astype(BF16)


def _gdn_scan_kernel(u_ref, w_ref, qd_ref, kt_ref, ai_ref, eg_ref, gz_ref, s0_ref, gout_ref, o_ref, sfin_ref, s_scr,
                     *, C, GB, n_chunks):
    c = pl.program_id(1)

    @pl.when(c == 0)
    def _():
        s_scr[...] = s0_ref[...]

    chains = [(b, h) for b in range(GB) for h in range(HEADS)]
    cols = lambda h: slice(h * HEAD_W, (h + 1) * HEAD_W)
    s16 = {bh: s_scr[bh[0], bh[1]].astype(BF16) for bh in chains}
    v_new = {(b, h): (u_ref[b, :, cols(h)] - _dot(w_ref[b, :, cols(h)], s16[b, h])).astype(BF16) for b, h in chains}
    o = {(b, h): _dot(qd_ref[b, :, cols(h)], s16[b, h]) + _dot(ai_ref[b, :, h * C:(h + 1) * C], v_new[b, h])
         for b, h in chains}
    for b, h in chains:
        s_scr[b, h] = s_scr[b, h] * eg_ref[b, 0:1, h:h + 1] + lax.dot_general(
            kt_ref[b, :, cols(h)], v_new[b, h], (((0,), (0,)), ((), ())), preferred_element_type=F32)
    for b, h in chains:
        o_ref[b, :, cols(h)] = (_rms(o[b, h], gout_ref[...]) * _silu(gz_ref[b, :, cols(h)])).astype(BF16)

    @pl.when(c == n_chunks - 1)
    def _():
        sfin_ref[...] = s_scr[...]


def _gdn(q, k, v, gz, gates, s0, g_out, C):
    B, T, _ = q.shape
    nc = T // C
    G = min(nc, 4)
    GB = 2 if B % 2 == 0 else 1
    HC = HEADS * C
    pblk = lambda w: pl.BlockSpec((1, G * C, w), lambda b, c: (b, c, 0))
    row = lambda w, dt: jax.ShapeDtypeStruct((B, T, w), dt)
    u, w, qd, kt, ai, eg = pl.pallas_call(
        functools.partial(_gdn_prep_kernel, C=C, G=G),
        grid=(B, nc // G),
        in_specs=[pblk(MIX_HALF)] * 3 + [pblk(LANES)],
        out_specs=[pblk(MIX_HALF)] * 4 + [pblk(HC), pl.BlockSpec((1, G * SUBLANES, LANES), lambda b, c: (b, c, 0))],
        out_shape=[row(MIX_HALF, F32), row(MIX_HALF, BF16), row(MIX_HALF, BF16), row(MIX_HALF, BF16), row(HC, BF16),
                   jax.ShapeDtypeStruct((B, nc * SUBLANES, LANES), F32)],
        compiler_params=_params(("parallel", "parallel")),
        name="gdn_prep",
    )(q, k, v, gates)
    sblk = lambda w: pl.BlockSpec((GB, C, w), lambda b, c: (b, c, 0))
    st = pl.BlockSpec((GB, HEADS, HEAD_W, HEAD_W), lambda b, c: (b, 0, 0, 0))
    return pl.pallas_call(
        functools.partial(_gdn_scan_kernel, C=C, GB=GB, n_chunks=nc),
        grid=(B // GB, nc),
        in_specs=[sblk(MIX_HALF)] * 4 + [sblk(HC), pl.BlockSpec((GB, SUBLANES, LANES), lambda b, c: (b, c, 0)),
                                         sblk(MIX_HALF), st, pl.BlockSpec(g_out.shape, lambda b, c: (0, 0))],
        out_specs=[sblk(MIX_HALF), st],
        out_shape=[jax.ShapeDtypeStruct((B, T, MIX_HALF), BF16),
                   jax.ShapeDtypeStruct((B, HEADS, HEAD_W, HEAD_W), F32)],
        scratch_shapes=[pltpu.VMEM((GB, HEADS, HEAD_W, HEAD_W), F32)],
        compiler_params=_params(("parallel", "arbitrary")),
        name="gdn_scan",
    )(u, w, qd, kt, ai, eg, gz, s0, g_out)


def _memkv_kernel(m_ref, g_ref, wk_ref, wv_ref, kf_ref, kb_ref, vf_ref, vb_ref):
    m = _rms(m_ref[...], g_ref[...]).astype(BF16)
    mk = _dot(m, wk_ref[...])
    mv = _dot(m, wv_ref[...])
    kf_ref[...] = mk
    kb_ref[...] = mk.astype(BF16)
    vf_ref[...] = mv
    vb_ref[...] = mv.astype(BF16)


def _memkv(mem, g_mem, w_ck, w_cv, tm):
    N, D = mem.shape
    blk = pl.BlockSpec((tm, D), lambda i: (i, 0))
    full = lambda a: pl.BlockSpec(a.shape, lambda i: (0,) * a.ndim)
    return pl.pallas_call(
        _memkv_kernel,
        grid=(N // tm,),
        in_specs=[blk, full(g_mem), full(w_ck), full(w_cv)],
        out_specs=[blk] * 4,
        out_shape=[jax.ShapeDtypeStruct((N, D), F32), jax.ShapeDtypeStruct((N, D), BF16)] * 2,
        compiler_params=_params(("parallel",)),
        name="memkv",
    )(mem, g_mem, w_ck, w_cv)


def _mid_kernel(x_ref, da_ref, gd_ref, wo_ref, gc_ref, wq_ref, mk_ref, mv_ref, wco_ref, gf_ref, wr_ref, br_ref,
                x2_ref, hf_ref, comb_ref):
    x1 = x_ref[0] + _dot(da_ref[0], wo_ref[0:MIX_HALF, :]) + _dot(gd_ref[0], wo_ref[MIX_HALF:2 * MIX_HALF, :])
    hc = _rms(x1, gc_ref[...]).astype(BF16)
    q = (_dot(hc, wq_ref[...]) * (X_DH ** -0.5)).astype(BF16)
    heads = []
    for h in range(X_HEADS):
        sl = slice(h * X_DH, (h + 1) * X_DH)
        s = _dot_nt(q[:, sl], mk_ref[0, :, sl])
        p = jnp.exp(s - jnp.max(s, axis=-1, keepdims=True))
        l = jnp.sum(p, axis=-1, keepdims=True)
        heads.append((_dot(p.astype(BF16), mv_ref[0, :, sl]) / l).astype(BF16))
    x2 = x1 + _dot(jnp.concatenate(heads, axis=1), wco_ref[...])
    x2_ref[0] = x2
    hf = _rms(x2, gf_ref[...])
    hf_ref[0] = hf.astype(BF16)

    logits = jnp.dot(hf, wr_ref[...], preferred_element_type=F32, precision=lax.Precision.HIGHEST) + br_ref[...]
    lane_i = lax.broadcasted_iota(jnp.int32, logits.shape, 1)
    lane = lane_i.astype(F32)
    big = float(LANES)
    is_grp = (lane_i >= N_EXPERTS) & (lane_i < N_EXPERTS + N_GROUPS)
    gl = jnp.where(is_grp, logits, NEG_INF)
    gmax = jnp.max(gl, axis=-1, keepdims=True)
    g_top = 1.0 / jnp.sum(jnp.exp(gl - gmax), axis=-1, keepdims=True)
    g_idx = jnp.min(jnp.where(gl == gmax, lane, big), axis=-1, keepdims=True) - float(N_EXPERTS)
    in_grp = (lane_i < N_EXPERTS) & (jnp.floor(lane * (1.0 / E_PER_GROUP)) == g_idx)
    el = jnp.where(in_grp, logits, NEG_INF)
    e1 = jnp.max(el, axis=-1, keepdims=True)
    i1 = jnp.min(jnp.where(el == e1, lane, big), axis=-1, keepdims=True)
    el2 = jnp.where(lane == i1, NEG_INF, el)
    e2 = jnp.max(el2, axis=-1, keepdims=True)
    i2 = jnp.min(jnp.where(el2 == e2, lane, big), axis=-1, keepdims=True)
    r = jnp.exp(e2 - e1)
    w1 = g_top / (1.0 + r)
    w2 = g_top * r / (1.0 + r)
    comb_ref[0] = jnp.where(lane_i == 0, i1, jnp.where(lane_i == 1, i2, jnp.where(
        lane_i == 2, w1, jnp.where(lane_i == 3, w2, 0.0))))


def _mid(x, mix_da, mix_gd, w_out, g_cross, w_cq, mk, mv, w_co, g_ffn, w_router, b_router, tm):
    B, T, D = x.shape
    M = mk.shape[1]
    blk = lambda w: pl.BlockSpec((1, tm, w), lambda b, t: (b, t, 0))
    mem = pl.BlockSpec((1, M, D), lambda b, t: (b, 0, 0))
    full = lambda a: pl.BlockSpec(a.shape, lambda b, t: (0,) * a.ndim)
    return pl.pallas_call(
        _mid_kernel,
        grid=(B, T // tm),
        in_specs=[blk(D), blk(MIX_HALF), blk(MIX_HALF), full(w_out), full(g_cross), full(w_cq), mem, mem,
                  full(w_co), full(g_ffn), full(w_router), full(b_router)],
        out_specs=[blk(D), blk(D), blk(LANES)],
        out_shape=[jax.ShapeDtypeStruct((B, T, D), F32), jax.ShapeDtypeStruct((B, T, D), BF16),
                   jax.ShapeDtypeStruct((B, T, LANES), F32)],
        compiler_params=_params(("parallel", "parallel")),
        name="mid",
    )(x, mix_da, mix_gd, w_out, g_cross, w_cq, mk, mv, w_co, g_ffn, w_router, b_router)


MOE_BLK = 16
EXP_TILE_BLKS = 16


def _slots_per_tile(tm):
    worst = 2 * tm + N_EXPERTS * (MOE_BLK - 1)
    return -(-worst // LANES) * LANES


def _route_kernel(hf_ref, rec_ref, xs_ref, ws_ref, slot_ref, meta_ref, *, tm, sl):
    rec = rec_ref[...]
    lane_i = lax.broadcasted_iota(jnp.int32, rec.shape, 1)
    lane = lane_i.astype(F32)
    i1, i2, w1, w2 = rec[:, 0:1], rec[:, 1:2], rec[:, 2:3], rec[:, 3:4]
    oh1 = jnp.where(lane == i1, 1.0, 0.0)
    oh2 = jnp.where(lane == i2, 1.0, 0.0)
    oh = oh1 + oh2
    nblk = jnp.floor((jnp.sum(oh, axis=0, keepdims=True) + (MOE_BLK - 1.0)) * (1.0 / MOE_BLK))
    ur = lax.broadcasted_iota(jnp.int32, (LANES, LANES), 0)
    uc = lax.broadcasted_iota(jnp.int32, (LANES, LANES), 1)
    start = MOE_BLK * jnp.dot(jnp.broadcast_to(nblk, (SUBLANES, LANES)), jnp.where(ur < uc, 1.0, 0.0).astype(F32),
                              preferred_element_type=F32, precision=lax.Precision.HIGHEST)[0:1]
    r = lax.broadcasted_iota(jnp.int32, (tm, tm), 0)
    c = lax.broadcasted_iota(jnp.int32, (tm, tm), 1)
    pos = start + _dot(jnp.where(r > c, 1.0, 0.0).astype(BF16), oh.astype(BF16))
    slot1 = jnp.sum(oh1 * pos, axis=-1, keepdims=True)
    slot2 = jnp.sum(oh2 * pos, axis=-1, keepdims=True)
    slot_ref[...] = jnp.where(lane_i == 0, slot1, jnp.where(lane_i == 1, slot2, 0.0))
    meta_ref[0] = jnp.broadcast_to(nblk, (SUBLANES, LANES))

    srow = lax.broadcasted_iota(jnp.int32, (sl, tm), 0).astype(F32)
    p1 = jnp.where(srow == jnp.broadcast_to(slot1, (tm, LANES)).T[0:1, :], 1.0, 0.0).astype(BF16)
    p2 = jnp.where(srow == jnp.broadcast_to(slot2, (tm, LANES)).T[0:1, :], 1.0, 0.0).astype(BF16)
    xs_ref[...] = _dot(p1 + p2, hf_ref[...]).astype(BF16)

    def parts(w):
        hi = w.astype(BF16).astype(F32)
        mid = (w - hi).astype(BF16).astype(F32)
        lo = w - hi - mid
        return jnp.where(lane_i == 0, hi, jnp.where(lane_i == 1, mid, jnp.where(lane_i == 2, lo, 0.0))).astype(BF16)

    ws_ref[...] = _dot(p1, parts(w1)) + _dot(p2, parts(w2))


def _route(hf, rec, tm):
    N, D = hf.shape
    nt = N // tm
    sl = _slots_per_tile(tm)
    blk = lambda rows, w: pl.BlockSpec((rows, w), lambda i: (i, 0))
    return pl.pallas_call(
        functools.partial(_route_kernel, tm=tm, sl=sl),
        grid=(nt,),
        in_specs=[blk(tm, D), blk(tm, LANES)],
        out_specs=[blk(sl, D), blk(sl, LANES), blk(tm, LANES), pl.BlockSpec((1, SUBLANES, LANES), lambda i: (i, 0, 0))],
        out_shape=[jax.ShapeDtypeStruct((nt * sl, D), BF16), jax.ShapeDtypeStruct((nt * sl, LANES), F32),
                   jax.ShapeDtypeStruct((N, LANES), F32), jax.ShapeDtypeStruct((nt, SUBLANES, LANES), F32)],
        compiler_params=_params(("parallel",)),
        name="moe_route",
    )(hf, rec)


def _moe_tables(meta, blocks_per_tile, max_tiles):
    nblk = meta[:, 0, :N_EXPERTS].astype(jnp.int32)
    nt = nblk.shape[0]
    tiles_e = (jnp.sum(nblk, axis=0) + EXP_TILE_BLKS - 1) // EXP_TILE_BLKS
    tile_end = jnp.cumsum(tiles_e)
    first_blk = (tile_end - tiles_e) * EXP_TILE_BLKS
    off = first_blk[None, :] + jnp.cumsum(nblk, axis=0) - nblk
    lend = jnp.cumsum(nblk, axis=1)
    lstart = lend - nblk
    lb = jnp.arange(blocks_per_tile, dtype=jnp.int32)[None, :]
    e_of = jnp.minimum(jnp.sum(lend[:, None, :] <= lb[:, :, None], axis=-1), N_EXPERTS - 1)
    valid = lb < lend[:, -1:]
    dest = jnp.take_along_axis(off, e_of, axis=1) + lb - jnp.take_along_axis(lstart, e_of, axis=1)
    flat = jnp.arange(nt, dtype=jnp.int32)[:, None] * blocks_per_tile + lb
    n_sorted = max_tiles * EXP_TILE_BLKS
    src = jnp.full((n_sorted,), -1, jnp.int32).at[jnp.where(valid, dest, n_sorted).reshape(-1)].set(
        flat.reshape(-1), mode="drop")
    g = jnp.arange(max_tiles, dtype=jnp.int32)
    tile_expert = jnp.minimum(jnp.sum(tile_end[None, :] <= g[:, None], axis=-1), N_EXPERTS - 1).astype(jnp.int32)
    return src, tile_expert, tile_end[-1:].astype(jnp.int32)


def _experts_kernel(src_ref, te_ref, nt_ref, xs_hbm, ws_hbm, wg_ref, wu_ref, wd_ref, ys_hbm,
                    xbuf, wbuf, obuf, gsem, ssem):
    g = pl.program_id(0)
    n_tiles = nt_ref[0]
    nb = EXP_TILE_BLKS

    def for_blocks(tile, fn):
        for i in range(nb):
            s = src_ref[tile * nb + i]

            @pl.when(s >= 0)
            def _():
                fn(i, pl.ds(pl.multiple_of(s * MOE_BLK, MOE_BLK), MOE_BLK))

    def gather_copies(slot, i, rows):
        dst = pl.ds(i * MOE_BLK, MOE_BLK)
        return (pltpu.make_async_copy(xs_hbm.at[rows, :], xbuf.at[slot, dst, :], gsem.at[slot]),
                pltpu.make_async_copy(ws_hbm.at[rows, :], wbuf.at[slot, dst, :], gsem.at[slot]))

    def gather_start(tile, slot):
        for_blocks(tile, lambda i, rows: [cp.start() for cp in gather_copies(slot, i, rows)])

    def gather_wait(tile, slot):
        for_blocks(tile, lambda i, rows: [cp.wait() for cp in gather_copies(slot, i, rows)])

    def scatter_copy(slot, i, rows):
        return pltpu.make_async_copy(obuf.at[slot, pl.ds(i * MOE_BLK, MOE_BLK), :], ys_hbm.at[rows, :], ssem.at[slot])

    def scatter_start(tile, slot):
        for_blocks(tile, lambda i, rows: scatter_copy(slot, i, rows).start())

    def scatter_wait(tile, slot):
        for_blocks(tile, lambda i, rows: scatter_copy(slot, i, rows).wait())

    slot = g % 2

    @pl.when(g == 0)
    def _():
        xbuf[...] = jnp.zeros(xbuf.shape, BF16)
        wbuf[...] = jnp.zeros(wbuf.shape, F32)
        gather_start(0, 0)

    @pl.when(g < n_tiles)
    def _():
        @pl.when(g + 1 < n_tiles)
        def _():
            gather_start(g + 1, 1 - slot)

        gather_wait(g, slot)
        x = xbuf[slot]
        hg = _silu(_dot(x, wg_ref[0])) * _dot(x, wu_ref[0])
        w = wbuf[slot]
        lane = lax.broadcasted_iota(jnp.int32, w.shape, 1)
        ccol = jnp.sum(jnp.where(lane < 3, w, 0.0), axis=-1, keepdims=True)
        y = _dot((hg * ccol).astype(BF16), wd_ref[0]).astype(BF16)

        @pl.when(g >= 2)
        def _():
            scatter_wait(g - 2, slot)

        obuf[slot] = y
        scatter_start(g, slot)

    @pl.when(g == n_tiles - 1)
    def _():
        scatter_wait(g, slot)

        @pl.when(g >= 1)
        def _():
            scatter_wait(g - 1, 1 - slot)


def _experts(src, tile_expert, n_tiles, xs, ws, w_gate, w_up, w_down, max_tiles):
    D = xs.shape[1]
    rows = EXP_TILE_BLKS * MOE_BLK
    wspec = lambda shape: pl.BlockSpec((1,) + shape, lambda g, src, te, nt: (te[g], 0, 0))
    grid_spec = pltpu.PrefetchScalarGridSpec(
        num_scalar_prefetch=3,
        grid=(max_tiles,),
        in_specs=[pl.BlockSpec(memory_space=pl.ANY), pl.BlockSpec(memory_space=pl.ANY),
                  wspec((D, D_EXPERT)), wspec((D, D_EXPERT)), wspec((D_EXPERT, D))],
        out_specs=pl.BlockSpec(memory_space=pl.ANY),
        scratch_shapes=[pltpu.VMEM((2, rows, D), BF16), pltpu.VMEM((2, rows, LANES), F32),
                        pltpu.VMEM((2, rows, D), BF16), pltpu.SemaphoreType.DMA((2,)),
                        pltpu.SemaphoreType.DMA((2,))],
    )
    return pl.pallas_call(
        _experts_kernel,
        grid_spec=grid_spec,
        out_shape=jax.ShapeDtypeStruct(xs.shape, BF16),
        input_output_aliases={3: 0},
        compiler_params=_params(("arbitrary",)),
        name="moe_experts",
    )(src, tile_expert, n_tiles, xs, ws, w_gate, w_up, w_down)


def _combine_kernel(x2_ref, slot_ref, ys_ref, gfin_ref, y_ref, *, tm, sl):
    slots = slot_ref[...]
    col = lax.broadcasted_iota(jnp.int32, (tm, sl), 1).astype(F32)
    pick = (jnp.where(col == slots[:, 0:1], 1.0, 0.0) + jnp.where(col == slots[:, 1:2], 1.0, 0.0)).astype(BF16)
    y_ref[...] = _rms(x2_ref[...] + _dot(pick, ys_ref[...]), gfin_ref[...])


def _combine(x2, slots, ys, g_final, tm):
    N, D = x2.shape
    sl = ys.shape[0] // (N // tm)
    blk = lambda rows, w: pl.BlockSpec((rows, w), lambda i: (i, 0))
    return pl.pallas_call(
        functools.partial(_combine_kernel, tm=tm, sl=sl),
        grid=(N // tm,),
        in_specs=[blk(tm, D), blk(tm, LANES), blk(sl, D), pl.BlockSpec(g_final.shape, lambda i: (0, 0))],
        out_specs=blk(tm, D),
        out_shape=jax.ShapeDtypeStruct((N, D), F32),
        compiler_params=_params(("parallel",)),
        name="moe_combine",
    )(x2, slots, ys, g_final)


def _moe(x2, hf, rec, w_gate, w_up, w_down, g_final, tm):
    N, _ = x2.shape
    xs, ws, slots, meta = _route(hf, rec, tm)
    blocks_per_tile = _slots_per_tile(tm) // MOE_BLK
    max_tiles = (N // tm) * blocks_per_tile // EXP_TILE_BLKS + N_EXPERTS
    src, tile_expert, n_tiles = _moe_tables(meta, blocks_per_tile, max_tiles)
    ys = _experts(src, tile_expert, n_tiles, xs, ws, w_gate, w_up, w_down, max_tiles)
    return _combine(x2, slots, ys, g_final, tm)


def _pad_rows(a, rows):
    return jnp.pad(a, ((0, rows - a.shape[0]), (0, LANES - a.shape[1])))


def _trunk(x, mk16, mv16, past_k, past_v, conv_prev, s_prev, lam_init, wts):
    B, T, D = x.shape
    tm = min(T, 512)
    if conv_prev is None:
        conv_prev = jnp.zeros((B, CONV_W - 1, GDN_QKV), F32)
    if s_prev is None:
        s_prev = jnp.zeros((B, HEADS, HEAD_W, HEAD_W), F32)
    prev8 = jnp.pad(conv_prev, ((0, 0), (SUBLANES - (CONV_W - 1), 0), (0, 0)))
    (dq, dkf, dk16, dvf, dv16, cq, ck, cv, gz, gates, cnew) = _inproj(
        x, prev8, wts["g_mix"], wts["w_main"], wts["w_gate8"], wts["conv_w8"], wts["gate_par"], tm)
    if past_k is None:
        mix_da = _da_prompt(wts["lam_par"], wts["g_subln"], dq, dk16, dv16, min(T, 512), lam_init)
    else:
        P = past_k.shape[1]
        mix_da = _da_sample(wts["lam_par"], wts["g_subln"], dq, past_k.reshape(B, P, MIX_HALF),
                            past_v.reshape(B, P, MIX_HALF), dk16, dv16, lam_init)
    C = min(T, CHUNK)
    mix_gd, s_new = _gdn(cq, ck, cv, gz, gates, s_prev, wts["g_gdn_out"], C)
    x2, hf, comb = _mid(x, mix_da, mix_gd, wts["w_out"], wts["g_cross"], wts["w_cq"], mk16, mv16, wts["w_co"],
                        wts["g_ffn"], wts["w_router"], wts["b_router"], min(T, 256))
    N = B * T
    y = _moe(x2.reshape(N, D), hf.reshape(N, D), comb.reshape(N, LANES), wts["w_gate"], wts["w_up"],
             wts["w_down"], wts["g_final"], min(N, 512))
    new_k = dkf.reshape(1, B, T, HEADS, HEAD_W)
    new_v = dvf.reshape(1, B, T, HEADS, HEAD_W)
    new_conv = cnew[:, SUBLANES - (CONV_W - 1):, :].reshape(1, B, CONV_W - 1, GDN_QKV)
    return y.reshape(B, T, D), new_k, new_v, new_conv, s_new[None]


def kernel(x_prompt, x_sample, cache_diff_k, cache_diff_v, state_gdn_conv, state_gdn, cache_mem_k, cache_mem_v,
           mem_prompt, g_mix, w_in, lam_q1, lam_k1, lam_q2, lam_k2, g_subln, conv_w, a_log, dt_bias, g_gdn_out,
           w_out, g_cross, g_mem, w_ck, w_cv, w_cq, w_co, g_ffn, w_grp, b_grp, w_rt, b_rt, w_gate, w_up, w_down,
           g_final):
    assert g_mix.shape[0] == 1, "single-layer model"
    l = 0
    lam_init = 0.8 - 0.6 * math.exp(-0.3 * l)
    gate_lo = 3 * MIX_HALF + GDN_QKV
    gate_hi = gate_lo + 2 * HEADS
    wi = w_in[l]
    row = lambda a: a.reshape(1, -1).astype(F32)
    wts = dict(
        g_mix=row(g_mix[l]),
        w_main=jnp.concatenate([wi[:, :gate_lo], wi[:, gate_hi:]], axis=1).astype(BF16),
        w_gate8=jnp.pad(wi[:, gate_lo:gate_hi], ((0, 0), (0, LANES - 2 * HEADS))).astype(BF16),
        conv_w8=jnp.pad(conv_w[l], ((0, SUBLANES - CONV_W), (0, 0))),
        gate_par=_pad_rows(jnp.stack([a_log[l], dt_bias[l]]), SUBLANES),
        lam_par=_pad_rows(jnp.stack([lam_q1[l], lam_k1[l], lam_q2[l], lam_k2[l]]), SUBLANES),
        g_subln=row(g_subln[l]), g_gdn_out=row(g_gdn_out[l]),
        w_out=w_out[l].astype(BF16), g_cross=row(g_cross[l]), w_cq=w_cq[l].astype(BF16), w_co=w_co[l].astype(BF16),
        g_ffn=row(g_ffn[l]),
        w_router=jnp.pad(jnp.concatenate([w_rt[l], w_grp[l]], axis=1),
                         ((0, 0), (0, LANES - N_EXPERTS - N_GROUPS))),
        b_router=jnp.pad(jnp.concatenate([b_rt[l], b_grp[l]]), (0, LANES - N_EXPERTS - N_GROUPS)).reshape(1, LANES),
        w_gate=w_gate[l].astype(BF16), w_up=w_up[l].astype(BF16), w_down=w_down[l].astype(BF16),
        g_final=row(g_final),
    )
    B, M, D = mem_prompt.shape
    mkf, mk16, mvf, mv16 = _memkv(mem_prompt.reshape(B * M, D), row(g_mem[l]), w_ck[l].astype(BF16),
                                  w_cv[l].astype(BF16), min(B * M, 512))
    yp, pk, pv, pc, ps = _trunk(x_prompt, mk16.reshape(B, M, D), mv16.reshape(B, M, D), None, None, None, None,
                                lam_init, wts)
    Bs = x_sample.shape[0]
    ys, sk, sv, sc, ss = _trunk(x_sample, cache_mem_k[l].reshape(Bs, M, D).astype(BF16),
                                cache_mem_v[l].reshape(Bs, M, D).astype(BF16), cache_diff_k[l], cache_diff_v[l],
                                state_gdn_conv[l], state_gdn[l], lam_init, wts)
    mem_shape = (1, B, M, X_HEADS, X_DH)
    return (yp, ys, pk, pv, pc, ps, mkf.reshape(mem_shape), mvf.reshape(mem_shape), sk, sv, sc, ss)
```

```python
import functools
import math

import jax
import jax.numpy as jnp
import numpy as np
from jax import lax
from jax.experimental import pallas as pl
from jax.experimental.pallas import tpu as pltpu

F32 = jnp.float32
BF16 = jnp.bfloat16

D_MODEL = 1024
CHUNK = 64
HEADS = 4
HEAD_W = 128
DA_DH = 64
MIX_HALF = HEADS * HEAD_W
GDN_QKV = 3 * MIX_HALF
CONV_W = 4
X_HEADS = 4
X_DH = 256
N_GROUPS = 4
E_PER_GROUP = 8
N_EXPERTS = 32
D_EXPERT = 256
NORM_EPS = 1e-6
NEG_INF = -1e30
LANES = 128
SUBLANES = 8
VMEM_LIMIT = 56 * 1024 * 1024


def _dot(a, b):
    return jnp.dot(a, b, preferred_element_type=F32)


def _dot_nt(a, b):
    return lax.dot_general(a, b, (((1,), (1,)), ((), ())), preferred_element_type=F32)


def _rms(x, g):
    return x * lax.rsqrt(jnp.mean(x * x, axis=-1, keepdims=True) + NORM_EPS) * g


def _sigmoid(x):
    return 1.0 / (1.0 + jnp.exp(-x))


def _silu(x):
    return x * _sigmoid(x)


def _split_bf16(a):
    hi = a.astype(BF16)
    lo = (a - hi.astype(F32)).astype(BF16)
    return hi, lo


def _dot3(a, b):
    ah, al = _split_bf16(a)
    bh, bl = _split_bf16(b)
    return _dot(ah, bh) + (_dot(ah, bl) + _dot(al, bh))


def _params(sem):
    return pltpu.CompilerParams(dimension_semantics=sem, vmem_limit_bytes=VMEM_LIMIT)


def _inproj_kernel(x_ref, prev_ref, gmix_ref, wm_ref, wg_ref, cw_ref, gp_ref,
                   dq_ref, dkf_ref, dkb_ref, dvf_ref, dvb_ref, cq_ref, ck_ref, cv_ref, gz_ref, gate_ref,
                   cnew_ref, xp_scr, *, tm):
    t = pl.program_id(1)
    h = _rms(x_ref[0], gmix_ref[...]).astype(BF16)

    dq = _dot(h, wm_ref[:, 0:MIX_HALF])
    dq_ref[0] = (dq * (DA_DH ** -0.5)).astype(BF16)
    dk = _dot(h, wm_ref[:, MIX_HALF:2 * MIX_HALF])
    dkb_ref[0] = dk.astype(BF16)
    dv = _dot(h, wm_ref[:, 2 * MIX_HALF:3 * MIX_HALF])
    dvb_ref[0] = dv.astype(BF16)
    for hh in range(HEADS):
        dkf_ref[0, pl.ds(hh, tm, stride=HEADS), :] = dk[:, hh * HEAD_W:(hh + 1) * HEAD_W]
        dvf_ref[0, pl.ds(hh, tm, stride=HEADS), :] = dv[:, hh * HEAD_W:(hh + 1) * HEAD_W]
    gz_ref[0] = _dot(h, wm_ref[:, 3 * MIX_HALF + GDN_QKV:4 * MIX_HALF + GDN_QKV])

    gates = _dot(h, wg_ref[...])
    z = gates + gp_ref[1:2, :]
    softplus = jnp.maximum(z, 0.0) + jnp.log1p(jnp.exp(-jnp.abs(z)))
    g_all = -jnp.exp(gp_ref[0:1, :]) * softplus
    lane = lax.broadcasted_iota(jnp.int32, gates.shape, 1)
    gate_ref[0] = jnp.where(lane < HEADS, g_all, jnp.where(lane < 2 * HEADS, _sigmoid(gates), 0.0))

    @pl.when(t == 0)
    def _():
        xp_scr[0:SUBLANES, :] = prev_ref[0]

    xp_scr[SUBLANES:SUBLANES + tm, :] = _dot(h, wm_ref[:, 3 * MIX_HALF:3 * MIX_HALF + GDN_QKV])
    outs = (cq_ref, ck_ref, cv_ref)
    for part in range(3):
        cs = slice(part * MIX_HALF, (part + 1) * MIX_HALF)
        y = xp_scr[5:5 + tm, cs] * cw_ref[0:1, cs]
        for j in range(1, CONV_W):
            y = y + xp_scr[5 + j:5 + j + tm, cs] * cw_ref[j:j + 1, cs]
        c = _silu(y)
        if part == 2:
            outs[part][0] = c
        else:
            scale = (HEAD_W ** -0.5) if part == 0 else 1.0
            for hh in range(HEADS):
                ch = c[:, hh * HEAD_W:(hh + 1) * HEAD_W]
                n = ch * lax.rsqrt(jnp.sum(ch * ch, axis=-1, keepdims=True) + NORM_EPS)
                outs[part][0, :, hh * HEAD_W:(hh + 1) * HEAD_W] = n * scale if part == 0 else n
    carry = xp_scr[tm:tm + SUBLANES, :]
    xp_scr[0:SUBLANES, :] = carry
    cnew_ref[0] = carry


def _inproj(x, prev8, g_mix, w_main, w_gate, conv_w8, gate_par, tm):
    B, T, D = x.shape
    nt = T // tm
    row = lambda w, dt: jax.ShapeDtypeStruct((B, T, w), dt)
    blk = lambda w: pl.BlockSpec((1, tm, w), lambda b, t: (b, t, 0))
    full = lambda a: pl.BlockSpec(a.shape, lambda b, t: (0,) * a.ndim)
    cache_blk = pl.BlockSpec((1, tm * HEADS, HEAD_W), lambda b, t: (b, t, 0))
    cache_rows = jax.ShapeDtypeStruct((B, T * HEADS, HEAD_W), F32)
    return pl.pallas_call(
        functools.partial(_inproj_kernel, tm=tm),
        grid=(B, nt),
        in_specs=[blk(D), pl.BlockSpec((1, SUBLANES, GDN_QKV), lambda b, t: (b, 0, 0)),
                  full(g_mix), full(w_main), full(w_gate), full(conv_w8), full(gate_par)],
        out_specs=[blk(MIX_HALF), cache_blk, blk(MIX_HALF), cache_blk] + [blk(MIX_HALF)] * 5
        + [blk(LANES), pl.BlockSpec((1, SUBLANES, GDN_QKV), lambda b, t: (b, 0, 0))],
        out_shape=[row(MIX_HALF, BF16), cache_rows, row(MIX_HALF, BF16), cache_rows,
                   row(MIX_HALF, BF16), row(MIX_HALF, F32), row(MIX_HALF, F32), row(MIX_HALF, F32),
                   row(MIX_HALF, F32), row(LANES, F32), jax.ShapeDtypeStruct((B, SUBLANES, GDN_QKV), F32)],
        scratch_shapes=[pltpu.VMEM((tm + SUBLANES, GDN_QKV), F32)],
        compiler_params=_params(("parallel", "arbitrary")),
        name="inproj",
    )(x, prev8, g_mix, w_main, w_gate, conv_w8, gate_par)


def _lam(lam_ref, lam_init):
    s1 = jnp.sum(lam_ref[0:1, :] * lam_ref[1:2, :], axis=-1, keepdims=True)
    s2 = jnp.sum(lam_ref[2:3, :] * lam_ref[3:4, :], axis=-1, keepdims=True)
    return jnp.exp(s1) - jnp.exp(s2) + lam_init


def _stack_q(q):
    lane = lax.broadcasted_iota(jnp.int32, q.shape, 1)
    zero = jnp.zeros_like(q)
    return jnp.concatenate([jnp.where(lane < DA_DH, q, zero), jnp.where(lane >= DA_DH, q, zero)], axis=0)


def _da_finish(acc, l, lam, gs, lam_init, tq):
    o = acc[:tq] / l[:tq] - lam * (acc[tq:] / l[tq:])
    return (_rms(o, gs) * (1.0 - lam_init)).astype(BF16)


def _da_prompt_kernel(lam_ref, gs_ref, q_ref, k_ref, v_ref, o_ref, vx_scr, m_scr, acc_scr, p_scr, al_scr, *,
                      tq, rows, lam_init):
    i = pl.program_id(2)

    @pl.when(i == 0)
    def _():
        vx_scr[:, 0:HEAD_W] = v_ref[0]
        vx_scr[:, HEAD_W:2 * HEAD_W] = jnp.ones((vx_scr.shape[0], HEAD_W), BF16)

    qz = _stack_q(q_ref[0])
    m_scr[...] = jnp.full(m_scr.shape, NEG_INF, F32)
    acc_scr[...] = jnp.zeros(acc_scr.shape, F32)
    n_lane_blocks = tq // LANES

    def scores(j, r0, masked):
        kj = k_ref[0, pl.ds(pl.multiple_of(j * tq, tq), tq), :]
        s = _dot_nt(qz[r0:r0 + rows], kj)
        if masked:
            r = lax.broadcasted_iota(jnp.int32, s.shape, 0) + (r0 % tq)
            c = lax.broadcasted_iota(jnp.int32, s.shape, 1)
            s = jnp.where((r // CHUNK) >= (c // CHUNK), s, NEG_INF)
        m_prev = m_scr[r0:r0 + rows, :]
        m_new = jnp.maximum(m_prev, jnp.max(s, axis=-1, keepdims=True))
        al_scr[r0:r0 + rows, :] = jnp.exp(m_prev - m_new)
        for c in range(n_lane_blocks):
            cs = slice(c * LANES, (c + 1) * LANES)
            p_scr[r0:r0 + rows, cs] = jnp.exp(s[:, cs] - m_new).astype(BF16)
        m_scr[r0:r0 + rows, :] = m_new

    def accumulate(j, r0):
        vj = vx_scr[pl.ds(pl.multiple_of(j * tq, tq), tq), :]
        alpha = al_scr[r0:r0 + rows, :]
        acc_scr[r0:r0 + rows, :] = (jnp.concatenate([alpha, alpha], axis=1) * acc_scr[r0:r0 + rows, :]
                                    + _dot(p_scr[r0:r0 + rows, :], vj))

    def stages(acc_j=None, score_j=None, masked=False):
        for r0 in range(0, 2 * tq, rows):
            if acc_j is not None:
                accumulate(acc_j, r0)
            if score_j is not None:
                scores(score_j, r0, masked)

    @pl.when(i == 0)
    def _():
        stages(score_j=0, masked=True)

    @pl.when(i > 0)
    def _():
        stages(score_j=0)

        def body(t, carry):
            stages(acc_j=t - 1, score_j=t)
            return carry

        lax.fori_loop(1, i, body, 0)
        stages(acc_j=i - 1, score_j=i, masked=True)

    stages(acc_j=i)
    acc = acc_scr[...]
    o = acc[:, 0:HEAD_W] / acc[:, HEAD_W:2 * HEAD_W]
    o = o[0:tq] - _lam(lam_ref, lam_init) * o[tq:2 * tq]
    o_ref[0] = (_rms(o, gs_ref[...]) * (1.0 - lam_init)).astype(BF16)


def _da_prompt(lam_par, g_subln, q, k, v, tq, lam_init):
    B, T, _ = q.shape
    kv_spec = pl.BlockSpec((1, T, HEAD_W), lambda b, h, i: (b, 0, h))
    q_spec = pl.BlockSpec((1, tq, HEAD_W), lambda b, h, i: (b, i, h))
    full = lambda a: pl.BlockSpec(a.shape, lambda b, h, i: (0,) * a.ndim)
    return pl.pallas_call(
        functools.partial(_da_prompt_kernel, tq=tq, rows=min(2 * tq, 128), lam_init=lam_init),
        grid=(B, HEADS, T // tq),
        in_specs=[full(lam_par), full(g_subln), q_spec, kv_spec, kv_spec],
        out_specs=q_spec,
        out_shape=jax.ShapeDtypeStruct((B, T, MIX_HALF), BF16),
        scratch_shapes=[pltpu.VMEM((T, 2 * HEAD_W), BF16), pltpu.VMEM((2 * tq, LANES), F32),
                        pltpu.VMEM((2 * tq, 2 * HEAD_W), F32), pltpu.VMEM((2 * tq, tq), BF16),
                        pltpu.VMEM((2 * tq, LANES), F32)],
        compiler_params=_params(("parallel", "parallel", "arbitrary")),
        name="diffattn_prompt",
    )(lam_par, g_subln, q, k, v)


def _da_sample_kernel(lam_ref, gs_ref, q_ref, pk_ref, pv_ref, k_ref, v_ref, o_ref, *, tq, past, lam_init):
    def visible(shape, k_off):
        r = lax.broadcasted_iota(jnp.int32, shape, 0)
        c = lax.broadcasted_iota(jnp.int32, shape, 1) + k_off
        r = jnp.where(r >= tq, r - tq, r) + past
        return (r // CHUNK) >= (c // CHUNK)

    for h in range(HEADS):
        sl = slice(h * HEAD_W, (h + 1) * HEAD_W)
        head_rows = pl.ds(h, past, stride=HEADS)
        qz = _stack_q(q_ref[0, :, sl])
        s_old = _dot_nt(qz, pk_ref[0, head_rows, :].astype(BF16))
        s_new = _dot_nt(qz, k_ref[0, :, sl])
        s_old = jnp.where(visible(s_old.shape, 0), s_old, NEG_INF)
        s_new = jnp.where(visible(s_new.shape, past), s_new, NEG_INF)
        m = jnp.maximum(jnp.max(s_old, axis=-1, keepdims=True), jnp.max(s_new, axis=-1, keepdims=True))
        p_old = jnp.exp(s_old - m)
        p_new = jnp.exp(s_new - m)
        l = jnp.sum(p_old, axis=-1, keepdims=True) + jnp.sum(p_new, axis=-1, keepdims=True)
        acc = (_dot(p_old.astype(BF16), pv_ref[0, head_rows, :].astype(BF16))
               + _dot(p_new.astype(BF16), v_ref[0, :, sl]))
        o_ref[0, :, sl] = _da_finish(acc, l, _lam(lam_ref, lam_init), gs_ref[...], lam_init, tq)


def _da_sample(lam_par, g_subln, q, past_k, past_v, k, v, lam_init):
    B, T, _ = q.shape
    P = past_k.shape[1] // HEADS
    new_spec = pl.BlockSpec((1, T, MIX_HALF), lambda b: (b, 0, 0))
    past_spec = pl.BlockSpec((1, P * HEADS, HEAD_W), lambda b: (b, 0, 0))
    full = lambda a: pl.BlockSpec(a.shape, lambda b: (0,) * a.ndim)
    return pl.pallas_call(
        functools.partial(_da_sample_kernel, tq=T, past=P, lam_init=lam_init),
        grid=(B,),
        in_specs=[full(lam_par), full(g_subln), new_spec, past_spec, past_spec, new_spec, new_spec],
        out_specs=new_spec,
        out_shape=jax.ShapeDtypeStruct((B, T, MIX_HALF), BF16),
        compiler_params=_params(("parallel",)),
        name="diffattn_sample",
    )(lam_par, g_subln, q, past_k, past_v, k, v)


def _gdn_prep_kernel(q_ref, k_ref, v_ref, gate_ref, u_ref, w_ref, qd_ref, kt_ref, ai_ref, eg_ref, *, C, G):
    HC = HEADS * C
    row = lax.broadcasted_iota(jnp.int32, (C, C), 0)
    col = lax.broadcasted_iota(jnp.int32, (C, C), 1)
    causal = row >= col
    strict = row > col
    arow = lax.broadcasted_iota(jnp.int32, (C, HC), 0)
    acol = lax.broadcasted_iota(jnp.int32, (C, HC), 1)
    eye_all = (arow == (acol % C)).astype(F32)
    brow = lax.broadcasted_iota(jnp.int32, (HC, HC), 0)
    bcolm = lax.broadcasted_iota(jnp.int32, (HC, HC), 1)
    same_head = (brow // C) == (bcolm // C)
    prow = lax.broadcasted_iota(jnp.int32, (LANES, C), 0)
    pcol = lax.broadcasted_iota(jnp.int32, (LANES, C), 1)
    tri = jnp.where((prow >= pcol) & (prow < C), 1.0, 0.0).astype(BF16)

    def cumsum_rows(x):
        hi = x.astype(BF16)
        mid = (x - hi.astype(F32)).astype(BF16)
        lo = (x - hi.astype(F32) - mid.astype(F32)).astype(BF16)
        return _dot(tri, hi) + (_dot(tri, mid) + _dot(tri, lo))

    def block_diag(m_all):
        return jnp.where(same_head, jnp.concatenate([m_all] * HEADS, axis=0), jnp.zeros((), m_all.dtype))

    gcs, gcts, ps, xs = [], [], [], []
    for g in range(G):
        rs = slice(g * C, (g + 1) * C)
        gates = gate_ref[0, rs, :]
        gc_pad = cumsum_rows(gates)
        gcts.append(gc_pad.T)
        gcs.append(gc_pad[0:C, :])
        eg_ref[0, g * SUBLANES:(g + 1) * SUBLANES, :] = jnp.broadcast_to(
            jnp.exp(gc_pad[C - 1:C, :]), (SUBLANES, LANES))

    def decay_of(g, h):
        return jnp.exp(jnp.where(causal, gcs[g][:, h:h + 1] - gcts[g][h:h + 1, 0:C], NEG_INF))

    for g in range(G):
        rs = slice(g * C, (g + 1) * C)
        a_heads = []
        for h in range(HEADS):
            k = k_ref[0, rs, h * HEAD_W:(h + 1) * HEAD_W]
            kb = k * gate_ref[0, rs, HEADS + h:HEADS + h + 1]
            a_heads.append(jnp.where(strict, -(_dot_nt(kb.astype(BF16), k.astype(BF16)) * decay_of(g, h)), 0.0))
        ps.append(jnp.concatenate(a_heads, axis=1))
        xs.append(eye_all + ps[g])
    def dot3_split(a, b_diag):
        (ah, al), (bh, bl) = a, b_diag
        return _dot(ah, bh) + (_dot(ah, bl) + _dot(al, bh))

    p_split = [_split_bf16(ps[g]) for g in range(G)]
    p_diag = [tuple(block_diag(part) for part in p_split[g]) for g in range(G)]
    for _ in range(int(math.log2(C)) - 1):
        for g in range(G):
            p_split[g] = _split_bf16(dot3_split(p_split[g], p_diag[g]))
            p_diag[g] = tuple(block_diag(part) for part in p_split[g])
        for g in range(G):
            xs[g] = xs[g] + dot3_split(_split_bf16(xs[g]), p_diag[g])

    for g in range(G):
        rs = slice(g * C, (g + 1) * C)
        for h in range(HEADS):
            sl = slice(h * HEAD_W, (h + 1) * HEAD_W)
            q = q_ref[0, rs, sl]
            v = v_ref[0, rs, sl]
            k = k_ref[0, rs, sl]
            gcol = gcs[g][:, h:h + 1]
            bcol = gate_ref[0, rs, HEADS + h:HEADS + h + 1]
            glast = gcs[g][C - 1:C, h:h + 1]
            egc = jnp.exp(gcol)
            t16 = xs[g][:, h * C:(h + 1) * C].astype(BF16)
            uw = _dot(t16, jnp.concatenate([v * bcol, k * bcol * egc], axis=1).astype(BF16))
            u_ref[0, rs, sl] = uw[:, 0:HEAD_W]
            w_ref[0, rs, sl] = uw[:, HEAD_W:2 * HEAD_W].astype(BF16)
            qd_ref[0, rs, sl] = (q * egc).astype(BF16)
            kt_ref[0, rs, sl] = (k * jnp.exp(glast - gcol)).astype(BF16)
            ai_ref[0, rs, h * C:(h + 1) * C] = (_dot_nt(q.astype(BF16), k.astype(BF16)) * decay_of(g, h)).astype(BF16)


def _gdn_scan_kernel(u_ref, w_ref, qd_ref, kt_ref, ai_ref, eg_ref, gz_ref, s0_ref, gout_ref, o_ref, sfin_ref, s_scr,
                     *, C, GB, n_chunks):
    c = pl.program_id(1)

    @pl.when(c == 0)
    def _():
        s_scr[...] = s0_ref[...]

    chains = [(b, h) for b in range(GB) for h in range(HEADS)]
    cols = lambda h: slice(h * HEAD_W, (h + 1) * HEAD_W)
    s16 = {bh: s_scr[bh[0], bh[1]].astype(BF16) for bh in chains}
    v_new = {(b, h): (u_ref[b, :, cols(h)] - _dot(w_ref[b, :, cols(h)], s16[b, h])).astype(BF16) for b, h in chains}
    o = {(b, h): _dot(qd_ref[b, :, cols(h)], s16[b, h]) + _dot(ai_ref[b, :, h * C:(h + 1) * C], v_new[b, h])
         for b, h in chains}
    for b, h in chains:
        s_scr[b, h] = s_scr[b, h] * eg_ref[b, 0:1, h:h + 1] + lax.dot_general(
            kt_ref[b, :, cols(h)], v_new[b, h], (((0,), (0,)), ((), ())), preferred_element_type=F32)
    for b, h in chains:
        o_ref[b, :, cols(h)] = (_rms(o[b, h], gout_ref[...]) * _silu(gz_ref[b, :, cols(h)])).astype(BF16)

    @pl.when(c == n_chunks - 1)
    def _():
        sfin_ref[...] = s_scr[...]


def _gdn(q, k, v, gz, gates, s0, g_out, C):
    B, T, _ = q.shape
    nc = T // C
    G = min(nc, 4)
    GB = 2 if B % 2 == 0 else 1
    HC = HEADS * C
    pblk = lambda w: pl.BlockSpec((1, G * C, w), lambda b, c: (b, c, 0))
    row = lambda w, dt: jax.ShapeDtypeStruct((B, T, w), dt)
    u, w, qd, kt, ai, eg = pl.pallas_call(
        functools.partial(_gdn_prep_kernel, C=C, G=G),
        grid=(B, nc // G),
        in_specs=[pblk(MIX_HALF)] * 3 + [pblk(LANES)],
        out_specs=[pblk(MIX_HALF)] * 4 + [pblk(HC), pl.BlockSpec((1, G * SUBLANES, LANES), lambda b, c: (b, c, 0))],
        out_shape=[row(MIX_HALF, F32), row(MIX_HALF, BF16), row(MIX_HALF, BF16), row(MIX_HALF, BF16), row(HC, BF16),
                   jax.ShapeDtypeStruct((B, nc * SUBLANES, LANES), F32)],
        compiler_params=_params(("parallel", "parallel")),
        name="gdn_prep",
    )(q, k, v, gates)
    sblk = lambda w: pl.BlockSpec((GB, C, w), lambda b, c: (b, c, 0))
    st = pl.BlockSpec((GB, HEADS, HEAD_W, HEAD_W), lambda b, c: (b, 0, 0, 0))
    return pl.pallas_call(
        functools.partial(_gdn_scan_kernel, C=C, GB=GB, n_chunks=nc),
        grid=(B // GB, nc),
        in_specs=[sblk(MIX_HALF)] * 4 + [sblk(HC), pl.BlockSpec((GB, SUBLANES, LANES), lambda b, c: (b, c, 0)),
                                         sblk(MIX_HALF), st, pl.BlockSpec(g_out.shape, lambda b, c: (0, 0))],
        out_specs=[sblk(MIX_HALF), st],
        out_shape=[jax.ShapeDtypeStruct((B, T, MIX_HALF), BF16),
                   jax.ShapeDtypeStruct((B, HEADS, HEAD_W, HEAD_W), F32)],
        scratch_shapes=[pltpu.VMEM((GB, HEADS, HEAD_W, HEAD_W), F32)],
        compiler_params=_params(("parallel", "arbitrary")),
        name="gdn_scan",
    )(u, w, qd, kt, ai, eg, gz, s0, g_out)


def _memkv_kernel(m_ref, g_ref, wk_ref, wv_ref, kf_ref, kb_ref, vf_ref, vb_ref):
    m = _rms(m_ref[...], g_ref[...]).astype(BF16)
    mk = _dot(m, wk_ref[...])
    mv = _dot(m, wv_ref[...])
    kf_ref[...] = mk
    kb_ref[...] = mk.astype(BF16)
    vf_ref[...] = mv
    vb_ref[...] = mv.astype(BF16)


def _memkv(mem, g_mem, w_ck, w_cv, tm):
    N, D = mem.shape
    blk = pl.BlockSpec((tm, D), lambda i: (i, 0))
    full = lambda a: pl.BlockSpec(a.shape, lambda i: (0,) * a.ndim)
    return pl.pallas_call(
        _memkv_kernel,
        grid=(N // tm,),
        in_specs=[blk, full(g_mem), full(w_ck), full(w_cv)],
        out_specs=[blk] * 4,
        out_shape=[jax.ShapeDtypeStruct((N, D), F32), jax.ShapeDtypeStruct((N, D), BF16)] * 2,
        compiler_params=_params(("parallel",)),
        name="memkv",
    )(mem, g_mem, w_ck, w_cv)


def _mid_kernel(x_ref, da_ref, gd_ref, wo_ref, gc_ref, wq_ref, mk_ref, mv_ref, wco_ref, gf_ref, wr_ref, br_ref,
                x2_ref, hf_ref, comb_ref):
    x1 = x_ref[0] + _dot(da_ref[0], wo_ref[0:MIX_HALF, :]) + _dot(gd_ref[0], wo_ref[MIX_HALF:2 * MIX_HALF, :])
    hc = _rms(x1, gc_ref[...]).astype(BF16)
    q = (_dot(hc, wq_ref[...]) * (X_DH ** -0.5)).astype(BF16)
    heads = []
    for h in range(X_HEADS):
        sl = slice(h * X_DH, (h + 1) * X_DH)
        s = _dot_nt(q[:, sl], mk_ref[0, :, sl])
        p = jnp.exp(s - jnp.max(s, axis=-1, keepdims=True))
        l = jnp.sum(p, axis=-1, keepdims=True)
        heads.append((_dot(p.astype(BF16), mv_ref[0, :, sl]) / l).astype(BF16))
    x2 = x1 + _dot(jnp.concatenate(heads, axis=1), wco_ref[...])
    x2_ref[0] = x2
    hf = _rms(x2, gf_ref[...])
    hf_ref[0] = hf.astype(BF16)

    logits = _dot3(hf, wr_ref[...]) + br_ref[...]
    lane_i = lax.broadcasted_iota(jnp.int32, logits.shape, 1)
    lane = lane_i.astype(F32)
    big = float(LANES)
    is_grp = (lane_i >= N_EXPERTS) & (lane_i < N_EXPERTS + N_GROUPS)
    gl = jnp.where(is_grp, logits, NEG_INF)
    gmax = jnp.max(gl, axis=-1, keepdims=True)
    g_top = 1.0 / jnp.sum(jnp.exp(gl - gmax), axis=-1, keepdims=True)
    g_idx = jnp.min(jnp.where(gl == gmax, lane, big), axis=-1, keepdims=True) - float(N_EXPERTS)
    in_grp = (lane_i < N_EXPERTS) & (jnp.floor(lane * (1.0 / E_PER_GROUP)) == g_idx)
    el = jnp.where(in_grp, logits, NEG_INF)
    e1 = jnp.max(el, axis=-1, keepdims=True)
    i1 = jnp.min(jnp.where(el == e1, lane, big), axis=-1, keepdims=True)
    el2 = jnp.where(lane == i1, NEG_INF, el)
    e2 = jnp.max(el2, axis=-1, keepdims=True)
    i2 = jnp.min(jnp.where(el2 == e2, lane, big), axis=-1, keepdims=True)
    r = jnp.exp(e2 - e1)
    w1 = g_top / (1.0 + r)
    w2 = g_top * r / (1.0 + r)
    comb_ref[0] = jnp.where(lane_i == 0, i1, jnp.where(lane_i == 1, i2, jnp.where(
        lane_i == 2, w1, jnp.where(lane_i == 3, w2, 0.0))))


def _mid(x, mix_da, mix_gd, w_out, g_cross, w_cq, mk, mv, w_co, g_ffn, w_router, b_router, tm):
    B, T, D = x.shape
    M = mk.shape[1]
    blk = lambda w: pl.BlockSpec((1, tm, w), lambda b, t: (b, t, 0))
    mem = pl.BlockSpec((1, M, D), lambda b, t: (b, 0, 0))
    full = lambda a: pl.BlockSpec(a.shape, lambda b, t: (0,) * a.ndim)
    return pl.pallas_call(
        _mid_kernel,
        grid=(B, T // tm),
        in_specs=[blk(D), blk(MIX_HALF), blk(MIX_HALF), full(w_out), full(g_cross), full(w_cq), mem, mem,
                  full(w_co), full(g_ffn), full(w_router), full(b_router)],
        out_specs=[blk(D), blk(D), blk(LANES)],
        out_shape=[jax.ShapeDtypeStruct((B, T, D), F32), jax.ShapeDtypeStruct((B, T, D), BF16),
                   jax.ShapeDtypeStruct((B, T, LANES), F32)],
        compiler_params=_params(("parallel", "parallel")),
        name="mid",
    )(x, mix_da, mix_gd, w_out, g_cross, w_cq, mk, mv, w_co, g_ffn, w_router, b_router)


MOE_BLK = 16
EXP_TILE_BLKS = 16


def _slots_per_tile(tm):
    worst = 2 * tm + N_EXPERTS * (MOE_BLK - 1)
    return -(-worst // LANES) * LANES


def _route_kernel(hf_ref, rec_ref, xs_ref, ws_ref, slot_ref, meta_ref, *, tm, sl):
    rec = rec_ref[...]
    lane_i = lax.broadcasted_iota(jnp.int32, rec.shape, 1)
    lane = lane_i.astype(F32)
    i1, i2, w1, w2 = rec[:, 0:1], rec[:, 1:2], rec[:, 2:3], rec[:, 3:4]
    oh1 = jnp.where(lane == i1, 1.0, 0.0)
    oh2 = jnp.where(lane == i2, 1.0, 0.0)
    oh = oh1 + oh2
    nblk = jnp.floor((jnp.sum(oh, axis=0, keepdims=True) + (MOE_BLK - 1.0)) * (1.0 / MOE_BLK))
    ur = lax.broadcasted_iota(jnp.int32, (LANES, LANES), 0)
    uc = lax.broadcasted_iota(jnp.int32, (LANES, LANES), 1)
    start = MOE_BLK * _dot(jnp.broadcast_to(nblk, (2 * SUBLANES, LANES)).astype(BF16),
                           jnp.where(ur < uc, 1.0, 0.0).astype(BF16))[0:1]
    r = lax.broadcasted_iota(jnp.int32, (tm, tm), 0)
    c = lax.broadcasted_iota(jnp.int32, (tm, tm), 1)
    pos = start + _dot(jnp.where(r > c, 1.0, 0.0).astype(BF16), oh.astype(BF16))
    slot1 = jnp.sum(oh1 * pos, axis=-1, keepdims=True)
    slot2 = jnp.sum(oh2 * pos, axis=-1, keepdims=True)
    slot_ref[...] = jnp.where(lane_i == 0, slot1, jnp.where(lane_i == 1, slot2, 0.0))
    meta_ref[0] = jnp.broadcast_to(nblk, (SUBLANES, LANES))

    srow = lax.broadcasted_iota(jnp.int32, (sl, tm), 0).astype(F32)
    p1 = jnp.where(srow == jnp.broadcast_to(slot1, (tm, LANES)).T[0:1, :], 1.0, 0.0).astype(BF16)
    p2 = jnp.where(srow == jnp.broadcast_to(slot2, (tm, LANES)).T[0:1, :], 1.0, 0.0).astype(BF16)
    xs_ref[...] = _dot(p1 + p2, hf_ref[...]).astype(BF16)

    def parts(w):
        hi = w.astype(BF16).astype(F32)
        mid = (w - hi).astype(BF16).astype(F32)
        lo = w - hi - mid
        return jnp.where(lane_i == 0, hi, jnp.where(lane_i == 1, mid, jnp.where(lane_i == 2, lo, 0.0))).astype(BF16)

    ws_ref[...] = _dot(p1, parts(w1)) + _dot(p2, parts(w2))


def _route(hf, rec, tm):
    N, D = hf.shape
    nt = N // tm
    sl = _slots_per_tile(tm)
    blk = lambda rows, w: pl.BlockSpec((rows, w), lambda i: (i, 0))
    return pl.pallas_call(
        functools.partial(_route_kernel, tm=tm, sl=sl),
        grid=(nt,),
        in_specs=[blk(tm, D), blk(tm, LANES)],
        out_specs=[blk(sl, D), blk(sl, LANES), blk(tm, LANES), pl.BlockSpec((1, SUBLANES, LANES), lambda i: (i, 0, 0))],
        out_shape=[jax.ShapeDtypeStruct((nt * sl, D), BF16), jax.ShapeDtypeStruct((nt * sl, LANES), F32),
                   jax.ShapeDtypeStruct((N, LANES), F32), jax.ShapeDtypeStruct((nt, SUBLANES, LANES), F32)],
        compiler_params=_params(("parallel",)),
        name="moe_route",
    )(hf, rec)


def _moe_tables(meta, blocks_per_tile, max_tiles):
    nblk = meta[:, 0, :N_EXPERTS].astype(jnp.int32)
    nt = nblk.shape[0]
    tiles_e = (jnp.sum(nblk, axis=0) + EXP_TILE_BLKS - 1) // EXP_TILE_BLKS
    tile_end = jnp.cumsum(tiles_e)
    first_blk = (tile_end - tiles_e) * EXP_TILE_BLKS
    off = first_blk[None, :] + jnp.cumsum(nblk, axis=0) - nblk
    lend = jnp.cumsum(nblk, axis=1)
    lstart = lend - nblk
    lb = jnp.arange(blocks_per_tile, dtype=jnp.int32)[None, :]
    in_run = (lstart[:, None, :] <= lb[:, :, None]) & (lb[:, :, None] < lend[:, None, :])
    valid = lb < lend[:, -1:]
    dest = jnp.sum(jnp.where(in_run, (off - lstart)[:, None, :], 0), axis=-1) + lb
    flat = jnp.arange(nt, dtype=jnp.int32)[:, None] * blocks_per_tile + lb
    n_sorted = max_tiles * EXP_TILE_BLKS
    src = jnp.full((n_sorted,), -1, jnp.int32).at[jnp.where(valid, dest, n_sorted).reshape(-1)].set(
        flat.reshape(-1), mode="drop")
    g = jnp.arange(max_tiles, dtype=jnp.int32)
    tile_expert = jnp.minimum(jnp.sum(tile_end[None, :] <= g[:, None], axis=-1), N_EXPERTS - 1).astype(jnp.int32)
    return src, tile_expert, tile_end[-1:].astype(jnp.int32)


def _experts_kernel(src_ref, te_ref, nt_ref, xs_hbm, ws_hbm, wg_ref, wu_ref, wd_ref, ys_hbm,
                    xbuf, wbuf, obuf, gsem, ssem):
    g = pl.program_id(0)
    n_tiles = nt_ref[0]
    nb = EXP_TILE_BLKS

    def for_blocks(tile, fn):
        for i in range(nb):
            s = src_ref[tile * nb + i]

            @pl.when(s >= 0)
            def _():
                fn(i, pl.ds(pl.multiple_of(s * MOE_BLK, MOE_BLK), MOE_BLK))

    def gather_copies(slot, i, rows):
        dst = pl.ds(i * MOE_BLK, MOE_BLK)
        return (pltpu.make_async_copy(xs_hbm.at[rows, :], xbuf.at[slot, dst, :], gsem.at[slot]),
                pltpu.make_async_copy(ws_hbm.at[rows, :], wbuf.at[slot, dst, :], gsem.at[slot]))

    def gather_start(tile, slot):
        for_blocks(tile, lambda i, rows: [cp.start() for cp in gather_copies(slot, i, rows)])

    def gather_wait(tile, slot):
        for_blocks(tile, lambda i, rows: [cp.wait() for cp in gather_copies(slot, i, rows)])

    def scatter_copy(slot, i, rows):
        return pltpu.make_async_copy(obuf.at[slot, pl.ds(i * MOE_BLK, MOE_BLK), :], ys_hbm.at[rows, :], ssem.at[slot])

    def scatter_start(tile, slot):
        for_blocks(tile, lambda i, rows: scatter_copy(slot, i, rows).start())

    def scatter_wait(tile, slot):
        for_blocks(tile, lambda i, rows: scatter_copy(slot, i, rows).wait())

    slot = g % 2

    @pl.when(g == 0)
    def _():
        xbuf[...] = jnp.zeros(xbuf.shape, BF16)
        wbuf[...] = jnp.zeros(wbuf.shape, F32)
        gather_start(0, 0)

    @pl.when(g < n_tiles)
    def _():
        @pl.when(g + 1 < n_tiles)
        def _():
            gather_start(g + 1, 1 - slot)

        gather_wait(g, slot)
        x = xbuf[slot]
        hg = _silu(_dot(x, wg_ref[0])) * _dot(x, wu_ref[0])
        w = wbuf[slot]
        lane = lax.broadcasted_iota(jnp.int32, w.shape, 1)
        ccol = jnp.sum(jnp.where(lane < 3, w, 0.0), axis=-1, keepdims=True)
        y = _dot((hg * ccol).astype(BF16), wd_ref[0]).astype(BF16)

        @pl.when(g >= 2)
        def _():
            scatter_wait(g - 2, slot)

        obuf[slot] = y
        scatter_start(g, slot)

    @pl.when(g == n_tiles - 1)
    def _():
        scatter_wait(g, slot)

        @pl.when(g >= 1)
        def _():
            scatter_wait(g - 1, 1 - slot)


def _experts(src, tile_expert, n_tiles, xs, ws, w_gate, w_up, w_down, max_tiles):
    D = xs.shape[1]
    rows = EXP_TILE_BLKS * MOE_BLK
    wspec = lambda shape: pl.BlockSpec((1,) + shape, lambda g, src, te, nt: (te[g], 0, 0))
    grid_spec = pltpu.PrefetchScalarGridSpec(
        num_scalar_prefetch=3,
        grid=(max_tiles,),
        in_specs=[pl.BlockSpec(memory_space=pl.ANY), pl.BlockSpec(memory_space=pl.ANY),
                  wspec((D, D_EXPERT)), wspec((D, D_EXPERT)), wspec((D_EXPERT, D))],
        out_specs=pl.BlockSpec(memory_space=pl.ANY),
        scratch_shapes=[pltpu.VMEM((2, rows, D), BF16), pltpu.VMEM((2, rows, LANES), F32),
                        pltpu.VMEM((2, rows, D), BF16), pltpu.SemaphoreType.DMA((2,)),
                        pltpu.SemaphoreType.DMA((2,))],
    )
    return pl.pallas_call(
        _experts_kernel,
        grid_spec=grid_spec,
        out_shape=jax.ShapeDtypeStruct(xs.shape, BF16),
        input_output_aliases={3: 0},
        compiler_params=_params(("arbitrary",)),
        name="moe_experts",
    )(src, tile_expert, n_tiles, xs, ws, w_gate, w_up, w_down)


def _combine_kernel(x2_ref, slot_ref, ys_ref, gfin_ref, y_ref, *, tm, sl):
    slots = slot_ref[...]
    col = lax.broadcasted_iota(jnp.int32, (tm, sl), 1).astype(F32)
    pick = (jnp.where(col == slots[:, 0:1], 1.0, 0.0) + jnp.where(col == slots[:, 1:2], 1.0, 0.0)).astype(BF16)
    y_ref[...] = _rms(x2_ref[...] + _dot(pick, ys_ref[...]), gfin_ref[...])


def _combine(x2, slots, ys, g_final, tm):
    N, D = x2.shape
    sl = ys.shape[0] // (N // tm)
    blk = lambda rows, w: pl.BlockSpec((rows, w), lambda i: (i, 0))
    return pl.pallas_call(
        functools.partial(_combine_kernel, tm=tm, sl=sl),
        grid=(N // tm,),
        in_specs=[blk(tm, D), blk(tm, LANES), blk(sl, D), pl.BlockSpec(g_final.shape, lambda i: (0, 0))],
        out_specs=blk(tm, D),
        out_shape=jax.ShapeDtypeStruct((N, D), F32),
        compiler_params=_params(("parallel",)),
        name="moe_combine",
    )(x2, slots, ys, g_final)


def _moe(x2, hf, rec, w_gate, w_up, w_down, g_final, tm):
    N, _ = x2.shape
    xs, ws, slots, meta = _route(hf, rec, tm)
    blocks_per_tile = _slots_per_tile(tm) // MOE_BLK
    max_tiles = (N // tm) * blocks_per_tile // EXP_TILE_BLKS + N_EXPERTS
    src, tile_expert, n_tiles = _moe_tables(meta, blocks_per_tile, max_tiles)
    ys = _experts(src, tile_expert, n_tiles, xs, ws, w_gate, w_up, w_down, max_tiles)
    return _combine(x2, slots, ys, g_final, tm)


def _pad_rows(a, rows):
    return jnp.pad(a, ((0, rows - a.shape[0]), (0, LANES - a.shape[1])))


def _trunk(x, mk16, mv16, past_k, past_v, conv_prev, s_prev, lam_init, wts):
    B, T, D = x.shape
    tm = min(T, 512)
    if conv_prev is None:
        conv_prev = jnp.zeros((B, CONV_W - 1, GDN_QKV), F32)
    if s_prev is None:
        s_prev = jnp.zeros((B, HEADS, HEAD_W, HEAD_W), F32)
    prev8 = jnp.pad(conv_prev, ((0, 0), (SUBLANES - (CONV_W - 1), 0), (0, 0)))
    (dq, dkf, dk16, dvf, dv16, cq, ck, cv, gz, gates, cnew) = _inproj(
        x, prev8, wts["g_mix"], wts["w_main"], wts["w_gate8"], wts["conv_w8"], wts["gate_par"], tm)
    if past_k is None:
        mix_da = _da_prompt(wts["lam_par"], wts["g_subln"], dq, dk16, dv16, min(T, 512), lam_init)
    else:
        P = past_k.shape[1]
        mix_da = _da_sample(wts["lam_par"], wts["g_subln"], dq, past_k.reshape(B, P * HEADS, HEAD_W),
                            past_v.reshape(B, P * HEADS, HEAD_W), dk16, dv16, lam_init)
    C = min(T, CHUNK)
    mix_gd, s_new = _gdn(cq, ck, cv, gz, gates, s_prev, wts["g_gdn_out"], C)
    x2, hf, comb = _mid(x, mix_da, mix_gd, wts["w_out"], wts["g_cross"], wts["w_cq"], mk16, mv16, wts["w_co"],
                        wts["g_ffn"], wts["w_router"], wts["b_router"], min(T, 512))
    N = B * T
    y = _moe(x2.reshape(N, D), hf.reshape(N, D), comb.reshape(N, LANES), wts["w_gate"], wts["w_up"],
             wts["w_down"], wts["g_final"], min(N, 512))
    new_k = dkf.reshape(1, B, T, HEADS, HEAD_W)
    new_v = dvf.reshape(1, B, T, HEADS, HEAD_W)
    new_conv = cnew[:, SUBLANES - (CONV_W - 1):, :].reshape(1, B, CONV_W - 1, GDN_QKV)
    return y.reshape(B, T, D), new_k, new_v, new_conv, s_new[None]


def kernel(x_prompt, x_sample, cache_diff_k, cache_diff_v, state_gdn_conv, state_gdn, cache_mem_k, cache_mem_v,
           mem_prompt, g_mix, w_in, lam_q1, lam_k1, lam_q2, lam_k2, g_subln, conv_w, a_log, dt_bias, g_gdn_out,
           w_out, g_cross, g_mem, w_ck, w_cv, w_cq, w_co, g_ffn, w_grp, b_grp, w_rt, b_rt, w_gate, w_up, w_down,
           g_final):
    assert g_mix.shape[0] == 1, "single-layer model"
    l = 0
    lam_init = 0.8 - 0.6 * math.exp(-0.3 * l)
    gate_lo = 3 * MIX_HALF + GDN_QKV
    gate_hi = gate_lo + 2 * HEADS
    wi = w_in[l]
    row = lambda a: a.reshape(1, -1).astype(F32)
    wts = dict(
        g_mix=row(g_mix[l]),
        w_main=jnp.concatenate([wi[:, :gate_lo], wi[:, gate_hi:]], axis=1).astype(BF16),
        w_gate8=jnp.pad(wi[:, gate_lo:gate_hi], ((0, 0), (0, LANES - 2 * HEADS))).astype(BF16),
        conv_w8=jnp.pad(conv_w[l], ((0, SUBLANES - CONV_W), (0, 0))),
        gate_par=_pad_rows(jnp.stack([a_log[l], dt_bias[l]]), SUBLANES),
        lam_par=_pad_rows(jnp.stack([lam_q1[l], lam_k1[l], lam_q2[l], lam_k2[l]]), SUBLANES),
        g_subln=row(g_subln[l]), g_gdn_out=row(g_gdn_out[l]),
        w_out=w_out[l].astype(BF16), g_cross=row(g_cross[l]), w_cq=w_cq[l].astype(BF16), w_co=w_co[l].astype(BF16),
        g_ffn=row(g_ffn[l]),
        w_router=jnp.pad(jnp.concatenate([w_rt[l], w_grp[l]], axis=1),
                         ((0, 0), (0, LANES - N_EXPERTS - N_GROUPS))),
        b_router=jnp.pad(jnp.concatenate([b_rt[l], b_grp[l]]), (0, LANES - N_EXPERTS - N_GROUPS)).reshape(1, LANES),
        w_gate=w_gate[l].astype(BF16), w_up=w_up[l].astype(BF16), w_down=w_down[l].astype(BF16),
        g_final=row(g_final),
    )
    B, M, D = mem_prompt.shape
    mkf, mk16, mvf, mv16 = _memkv(mem_prompt.reshape(B * M, D), row(g_mem[l]), w_ck[l].astype(BF16),
                                  w_cv[l].astype(BF16), min(B * M, 512))
    yp, pk, pv, pc, ps = _trunk(x_prompt, mk16.reshape(B, M, D), mv16.reshape(B, M, D), None, None, None, None,
                                lam_init, wts)
    Bs = x_sample.shape[0]
    ys, sk, sv, sc, ss = _trunk(x_sample, cache_mem_k[l].reshape(Bs, M, D).astype(BF16),
                                cache_mem_v[l].reshape(Bs, M, D).astype(BF16), cache_diff_k[l], cache_diff_v[l],
                                state_gdn_conv[l], state_gdn[l], lam_init, wts)
    mem_shape = (1, B, M, X_HEADS, X_DH)
    return (yp, ys, pk, pv, pc, ps, mkf.reshape(mem_shape), mvf.reshape(mem_shape), sk, sv, sc, ss)
```

```python
import functools
import math

import jax
import jax.numpy as jnp
import numpy as np
from jax import lax
from jax.experimental import pallas as pl
from jax.experimental.pallas import tpu as pltpu

F32 = jnp.float32
BF16 = jnp.bfloat16

D_MODEL = 1024
CHUNK = 64
HEADS = 4
HEAD_W = 128
DA_DH = 64
MIX_HALF = HEADS * HEAD_W
GDN_QKV = 3 * MIX_HALF
CONV_W = 4
X_HEADS = 4
X_DH = 256
N_GROUPS = 4
E_PER_GROUP = 8
N_EXPERTS = 32
D_EXPERT = 256
NORM_EPS = 1e-6
NEG_INF = -1e30
LANES = 128
SUBLANES = 8
VMEM_LIMIT = 56 * 1024 * 1024


def _dot(a, b):
    return jnp.dot(a, b, preferred_element_type=F32)


def _dot_nt(a, b):
    return lax.dot_general(a, b, (((1,), (1,)), ((), ())), preferred_element_type=F32)


def _rms(x, g):
    return x * lax.rsqrt(jnp.mean(x * x, axis=-1, keepdims=True) + NORM_EPS) * g


def _sigmoid(x):
    return 1.0 / (1.0 + jnp.exp(-x))


def _silu(x):
    return x * _sigmoid(x)


def _split_bf16(a):
    hi = a.astype(BF16)
    lo = (a - hi.astype(F32)).astype(BF16)
    return hi, lo


def _dot3(a, b):
    ah, al = _split_bf16(a)
    bh, bl = _split_bf16(b)
    return _dot(ah, bh) + (_dot(ah, bl) + _dot(al, bh))


def _params(sem):
    return pltpu.CompilerParams(dimension_semantics=sem, vmem_limit_bytes=VMEM_LIMIT)


def _inproj_kernel(x_ref, prev_ref, gmix_ref, wm_ref, wg_ref, cw_ref, gp_ref,
                   dq_ref, dkf_ref, dkb_ref, dvf_ref, dvb_ref, cq_ref, ck_ref, cv_ref, gz_ref, gate_ref,
                   cnew_ref, xp_scr, *, tm):
    t = pl.program_id(1)
    h = _rms(x_ref[0], gmix_ref[...]).astype(BF16)

    @pl.when(t == 0)
    def _():
        xp_scr[0:SUBLANES, :] = prev_ref[0]

    conv_col0 = 3 * MIX_HALF
    outs = (cq_ref, ck_ref, cv_ref)

    def conv_part(part):
        cs = slice(part * MIX_HALF, (part + 1) * MIX_HALF)
        xp_scr[SUBLANES:SUBLANES + tm, cs] = _dot(
            h, wm_ref[:, conv_col0 + part * MIX_HALF:conv_col0 + (part + 1) * MIX_HALF])
        y = xp_scr[5:5 + tm, cs] * cw_ref[0:1, cs]
        for j in range(1, CONV_W):
            y = y + xp_scr[5 + j:5 + j + tm, cs] * cw_ref[j:j + 1, cs]
        c = _silu(y)
        if part == 2:
            outs[part][0] = c
        else:
            scale = (HEAD_W ** -0.5) if part == 0 else 1.0
            for hh in range(HEADS):
                ch = c[:, hh * HEAD_W:(hh + 1) * HEAD_W]
                n = ch * lax.rsqrt(jnp.sum(ch * ch, axis=-1, keepdims=True) + NORM_EPS)
                outs[part][0, :, hh * HEAD_W:(hh + 1) * HEAD_W] = n * scale if part == 0 else n

    conv_part(0)
    dq = _dot(h, wm_ref[:, 0:MIX_HALF])
    dq_ref[0] = (dq * (DA_DH ** -0.5)).astype(BF16)
    conv_part(1)
    dk = _dot(h, wm_ref[:, MIX_HALF:2 * MIX_HALF])
    dkb_ref[0] = dk.astype(BF16)
    conv_part(2)
    dv = _dot(h, wm_ref[:, 2 * MIX_HALF:3 * MIX_HALF])
    dvb_ref[0] = dv.astype(BF16)
    for hh in range(HEADS):
        dkf_ref[0, pl.ds(hh, tm, stride=HEADS), :] = dk[:, hh * HEAD_W:(hh + 1) * HEAD_W]
        dvf_ref[0, pl.ds(hh, tm, stride=HEADS), :] = dv[:, hh * HEAD_W:(hh + 1) * HEAD_W]

    gates = _dot(h, wg_ref[...])
    z = gates + gp_ref[1:2, :]
    softplus = jnp.maximum(z, 0.0) + jnp.log1p(jnp.exp(-jnp.abs(z)))
    g_all = -jnp.exp(gp_ref[0:1, :]) * softplus
    lane = lax.broadcasted_iota(jnp.int32, gates.shape, 1)
    gate_ref[0] = jnp.where(lane < HEADS, g_all, jnp.where(lane < 2 * HEADS, _sigmoid(gates), 0.0))
    gz_ref[0] = _dot(h, wm_ref[:, 3 * MIX_HALF + GDN_QKV:4 * MIX_HALF + GDN_QKV])
    carry = xp_scr[tm:tm + SUBLANES, :]
    xp_scr[0:SUBLANES, :] = carry
    cnew_ref[0] = carry


def _inproj(x, prev8, g_mix, w_main, w_gate, conv_w8, gate_par, tm):
    B, T, D = x.shape
    nt = T // tm
    row = lambda w, dt: jax.ShapeDtypeStruct((B, T, w), dt)
    blk = lambda w: pl.BlockSpec((1, tm, w), lambda b, t: (b, t, 0))
    full = lambda a: pl.BlockSpec(a.shape, lambda b, t: (0,) * a.ndim)
    cache_blk = pl.BlockSpec((1, tm * HEADS, HEAD_W), lambda b, t: (b, t, 0))
    cache_rows = jax.ShapeDtypeStruct((B, T * HEADS, HEAD_W), F32)
    return pl.pallas_call(
        functools.partial(_inproj_kernel, tm=tm),
        grid=(B, nt),
        in_specs=[blk(D), pl.BlockSpec((1, SUBLANES, GDN_QKV), lambda b, t: (b, 0, 0)),
                  full(g_mix), full(w_main), full(w_gate), full(conv_w8), full(gate_par)],
        out_specs=[blk(MIX_HALF), cache_blk, blk(MIX_HALF), cache_blk] + [blk(MIX_HALF)] * 5
        + [blk(LANES), pl.BlockSpec((1, SUBLANES, GDN_QKV), lambda b, t: (b, 0, 0))],
        out_shape=[row(MIX_HALF, BF16), cache_rows, row(MIX_HALF, BF16), cache_rows,
                   row(MIX_HALF, BF16), row(MIX_HALF, F32), row(MIX_HALF, F32), row(MIX_HALF, F32),
                   row(MIX_HALF, F32), row(LANES, F32), jax.ShapeDtypeStruct((B, SUBLANES, GDN_QKV), F32)],
        scratch_shapes=[pltpu.VMEM((tm + SUBLANES, GDN_QKV), F32)],
        compiler_params=_params(("parallel", "arbitrary")),
        name="inproj",
    )(x, prev8, g_mix, w_main, w_gate, conv_w8, gate_par)


def _lam(lam_ref, lam_init):
    s1 = jnp.sum(lam_ref[0:1, :] * lam_ref[1:2, :], axis=-1, keepdims=True)
    s2 = jnp.sum(lam_ref[2:3, :] * lam_ref[3:4, :], axis=-1, keepdims=True)
    return jnp.exp(s1) - jnp.exp(s2) + lam_init


def _stack_q(q):
    lane = lax.broadcasted_iota(jnp.int32, q.shape, 1)
    zero = jnp.zeros_like(q)
    return jnp.concatenate([jnp.where(lane < DA_DH, q, zero), jnp.where(lane >= DA_DH, q, zero)], axis=0)


def _da_finish(acc, l, lam, gs, lam_init, tq):
    o = acc[:tq] / l[:tq] - lam * (acc[tq:] / l[tq:])
    return (_rms(o, gs) * (1.0 - lam_init)).astype(BF16)


def _da_prompt_kernel(lam_ref, gs_ref, q_ref, k_ref, v_ref, o_ref, vx_scr, m_scr, acc_scr, p_scr, al_scr, *,
                      tq, rows, heads, lam_init):
    i = pl.program_id(2)
    cols = lambda hh: slice(hh * HEAD_W, (hh + 1) * HEAD_W)

    @pl.when(i == 0)
    def _():
        for hh in range(heads):
            vx_scr[hh, :, 0:HEAD_W] = v_ref[0, :, cols(hh)]
            vx_scr[hh, :, HEAD_W:2 * HEAD_W] = jnp.ones((vx_scr.shape[1], HEAD_W), BF16)

    qz = [_stack_q(q_ref[0, :, cols(hh)]) for hh in range(heads)]
    m_scr[...] = jnp.full(m_scr.shape, NEG_INF, F32)
    acc_scr[...] = jnp.zeros(acc_scr.shape, F32)

    def keys_seen(r0):
        return min(tq, -(-((r0 % tq) + rows) // LANES) * LANES)

    def scores(hh, j, r0, masked):
        kw = keys_seen(r0) if masked else tq
        kj = k_ref[0, pl.ds(pl.multiple_of(j * tq, tq), kw), cols(hh)]
        s = _dot_nt(qz[hh][r0:r0 + rows], kj)
        if masked:
            r = lax.broadcasted_iota(jnp.int32, s.shape, 0) + (r0 % tq)
            c = lax.broadcasted_iota(jnp.int32, s.shape, 1)
            s = jnp.where((r // CHUNK) >= (c // CHUNK), s, NEG_INF)
        m_prev = m_scr[hh, r0:r0 + rows, :]
        m_new = jnp.maximum(m_prev, jnp.max(s, axis=-1, keepdims=True))
        al_scr[hh, r0:r0 + rows, :] = jnp.exp(m_prev - m_new)
        for c in range(kw // LANES):
            cs = slice(c * LANES, (c + 1) * LANES)
            p_scr[hh, r0:r0 + rows, cs] = jnp.exp(s[:, cs] - m_new).astype(BF16)
        m_scr[hh, r0:r0 + rows, :] = m_new

    def accumulate(hh, j, r0, diagonal):
        kw = keys_seen(r0) if diagonal else tq
        vj = vx_scr[hh, pl.ds(pl.multiple_of(j * tq, tq), kw), :]
        alpha = al_scr[hh, r0:r0 + rows, :]
        acc_scr[hh, r0:r0 + rows, :] = (jnp.concatenate([alpha, alpha], axis=1) * acc_scr[hh, r0:r0 + rows, :]
                                        + _dot(p_scr[hh, r0:r0 + rows, 0:kw], vj))

    def stages(acc_j=None, score_j=None, masked=False, diagonal=False):
        for r0 in range(0, 2 * tq, rows):
            for hh in range(heads):
                if acc_j is not None:
                    accumulate(hh, acc_j, r0, diagonal)
                if score_j is not None:
                    scores(hh, score_j, r0, masked)

    @pl.when(i == 0)
    def _():
        stages(score_j=0, masked=True)

    @pl.when(i > 0)
    def _():
        stages(score_j=0)

        def body(t, carry):
            stages(acc_j=t - 1, score_j=t)
            return carry

        lax.fori_loop(1, i, body, 0)
        stages(acc_j=i - 1, score_j=i, masked=True)

    stages(acc_j=i, diagonal=True)
    lam = _lam(lam_ref, lam_init)
    for hh in range(heads):
        acc = acc_scr[hh]
        o = acc[:, 0:HEAD_W] / acc[:, HEAD_W:2 * HEAD_W]
        o = o[0:tq] - lam * o[tq:2 * tq]
        o_ref[0, :, cols(hh)] = (_rms(o, gs_ref[...]) * (1.0 - lam_init)).astype(BF16)


def _da_prompt(lam_par, g_subln, q, k, v, tq, lam_init):
    B, T, _ = q.shape
    hg = 4
    kv_spec = pl.BlockSpec((1, T, hg * HEAD_W), lambda b, h, i: (b, 0, h))
    q_spec = pl.BlockSpec((1, tq, hg * HEAD_W), lambda b, h, i: (b, i, h))
    full = lambda a: pl.BlockSpec(a.shape, lambda b, h, i: (0,) * a.ndim)
    return pl.pallas_call(
        functools.partial(_da_prompt_kernel, tq=tq, rows=min(2 * tq, 128), heads=hg, lam_init=lam_init),
        grid=(B, HEADS // hg, T // tq),
        in_specs=[full(lam_par), full(g_subln), q_spec, kv_spec, kv_spec],
        out_specs=q_spec,
        out_shape=jax.ShapeDtypeStruct((B, T, MIX_HALF), BF16),
        scratch_shapes=[pltpu.VMEM((hg, T, 2 * HEAD_W), BF16), pltpu.VMEM((hg, 2 * tq, LANES), F32),
                        pltpu.VMEM((hg, 2 * tq, 2 * HEAD_W), F32), pltpu.VMEM((hg, 2 * tq, tq), BF16),
                        pltpu.VMEM((hg, 2 * tq, LANES), F32)],
        compiler_params=_params(("parallel", "parallel", "arbitrary")),
        name="diffattn_prompt",
    )(lam_par, g_subln, q, k, v)


def _da_sample_kernel(lam_ref, gs_ref, q_ref, pk_ref, pv_ref, k_ref, v_ref, o_ref, *, tq, past, lam_init):
    def visible(shape, k_off):
        r = lax.broadcasted_iota(jnp.int32, shape, 0)
        c = lax.broadcasted_iota(jnp.int32, shape, 1) + k_off
        r = jnp.where(r >= tq, r - tq, r) + past
        return (r // CHUNK) >= (c // CHUNK)

    for h in range(HEADS):
        sl = slice(h * HEAD_W, (h + 1) * HEAD_W)
        head_rows = pl.ds(h, past, stride=HEADS)
        qz = _stack_q(q_ref[0, :, sl])
        s_old = _dot_nt(qz, pk_ref[0, head_rows, :].astype(BF16))
        s_new = _dot_nt(qz, k_ref[0, :, sl])
        s_old = jnp.where(visible(s_old.shape, 0), s_old, NEG_INF)
        s_new = jnp.where(visible(s_new.shape, past), s_new, NEG_INF)
        m = jnp.maximum(jnp.max(s_old, axis=-1, keepdims=True), jnp.max(s_new, axis=-1, keepdims=True))
        p_old = jnp.exp(s_old - m)
        p_new = jnp.exp(s_new - m)
        l = jnp.sum(p_old, axis=-1, keepdims=True) + jnp.sum(p_new, axis=-1, keepdims=True)
        acc = (_dot(p_old.astype(BF16), pv_ref[0, head_rows, :].astype(BF16))
               + _dot(p_new.astype(BF16), v_ref[0, :, sl]))
        o_ref[0, :, sl] = _da_finish(acc, l, _lam(lam_ref, lam_init), gs_ref[...], lam_init, tq)


def _da_sample(lam_par, g_subln, q, past_k, past_v, k, v, lam_init):
    B, T, _ = q.shape
    P = past_k.shape[1] // HEADS
    new_spec = pl.BlockSpec((1, T, MIX_HALF), lambda b: (b, 0, 0))
    past_spec = pl.BlockSpec((1, P * HEADS, HEAD_W), lambda b: (b, 0, 0))
    full = lambda a: pl.BlockSpec(a.shape, lambda b: (0,) * a.ndim)
    return pl.pallas_call(
        functools.partial(_da_sample_kernel, tq=T, past=P, lam_init=lam_init),
        grid=(B,),
        in_specs=[full(lam_par), full(g_subln), new_spec, past_spec, past_spec, new_spec, new_spec],
        out_specs=new_spec,
        out_shape=jax.ShapeDtypeStruct((B, T, MIX_HALF), BF16),
        compiler_params=_params(("parallel",)),
        name="diffattn_sample",
    )(lam_par, g_subln, q, past_k, past_v, k, v)


def _gdn_prep_kernel(q_ref, k_ref, v_ref, gate_ref, u_ref, w_ref, qd_ref, kt_ref, ai_ref, eg_ref, *, C, G):
    HC = HEADS * C
    row = lax.broadcasted_iota(jnp.int32, (C, C), 0)
    col = lax.broadcasted_iota(jnp.int32, (C, C), 1)
    causal = row >= col
    strict = row > col
    arow = lax.broadcasted_iota(jnp.int32, (C, HC), 0)
    acol = lax.broadcasted_iota(jnp.int32, (C, HC), 1)
    eye_all = (arow == (acol % C)).astype(F32)
    brow = lax.broadcasted_iota(jnp.int32, (HC, HC), 0)
    bcolm = lax.broadcasted_iota(jnp.int32, (HC, HC), 1)
    same_head = (brow // C) == (bcolm // C)
    prow = lax.broadcasted_iota(jnp.int32, (LANES, C), 0)
    pcol = lax.broadcasted_iota(jnp.int32, (LANES, C), 1)
    tri = jnp.where((prow >= pcol) & (prow < C), 1.0, 0.0).astype(BF16)

    def cumsum_rows(x):
        hi = x.astype(BF16)
        mid = (x - hi.astype(F32)).astype(BF16)
        lo = (x - hi.astype(F32) - mid.astype(F32)).astype(BF16)
        return _dot(tri, hi) + (_dot(tri, mid) + _dot(tri, lo))

    def block_diag(m_all):
        return jnp.where(same_head, jnp.concatenate([m_all] * HEADS, axis=0), jnp.zeros((), m_all.dtype))

    gcs, gcts, ps, xs = [], [], [], []
    for g in range(G):
        rs = slice(g * C, (g + 1) * C)
        gates = gate_ref[0, rs, :]
        gc_pad = cumsum_rows(gates)
        gcts.append(gc_pad.T)
        gcs.append(gc_pad[0:C, :])
        eg_ref[0, g * SUBLANES:(g + 1) * SUBLANES, :] = jnp.broadcast_to(
            jnp.exp(gc_pad[C - 1:C, :]), (SUBLANES, LANES))

    def decay_of(g, h):
        return jnp.exp(jnp.where(causal, gcs[g][:, h:h + 1] - gcts[g][h:h + 1, 0:C], NEG_INF))

    for g in range(G):
        rs = slice(g * C, (g + 1) * C)
        a_heads = []
        for h in range(HEADS):
            k = k_ref[0, rs, h * HEAD_W:(h + 1) * HEAD_W]
            kb = k * gate_ref[0, rs, HEADS + h:HEADS + h + 1]
            a_heads.append(jnp.where(strict, -(_dot_nt(kb.astype(BF16), k.astype(BF16)) * decay_of(g, h)), 0.0))
        ps.append(jnp.concatenate(a_heads, axis=1))
        xs.append(eye_all + ps[g])
    def dot3_split(a, b_diag):
        (ah, al), (bh, bl) = a, b_diag
        return _dot(ah, bh) + (_dot(ah, bl) + _dot(al, bh))

    p_split = [_split_bf16(ps[g]) for g in range(G)]
    p_diag = [tuple(block_diag(part) for part in p_split[g]) for g in range(G)]
    for _ in range(int(math.log2(C)) - 1):
        for g in range(G):
            p_split[g] = _split_bf16(dot3_split(p_split[g], p_diag[g]))
            p_diag[g] = tuple(block_diag(part) for part in p_split[g])
        for g in range(G):
            xs[g] = xs[g] + dot3_split(_split_bf16(xs[g]), p_diag[g])

    for g in range(G):
        rs = slice(g * C, (g + 1) * C)
        for h in range(HEADS):
            sl = slice(h * HEAD_W, (h + 1) * HEAD_W)
            q = q_ref[0, rs, sl]
            v = v_ref[0, rs, sl]
            k = k_ref[0, rs, sl]
            gcol = gcs[g][:, h:h + 1]
            bcol = gate_ref[0, rs, HEADS + h:HEADS + h + 1]
            glast = gcs[g][C - 1:C, h:h + 1]
            egc = jnp.exp(gcol)
            t16 = xs[g][:, h * C:(h + 1) * C].astype(BF16)
            uw = _dot(t16, jnp.concatenate([v * bcol, k * bcol * egc], axis=1).astype(BF16))
            u_ref[0, rs, sl] = uw[:, 0:HEAD_W]
            w_ref[0, rs, sl] = uw[:, HEAD_W:2 * HEAD_W].astype(BF16)
            qd_ref[0, rs, sl] = (q * egc).astype(BF16)
            kt_ref[0, rs, sl] = (k * jnp.exp(glast - gcol)).astype(BF16)
            ai_ref[0, rs, h * C:(h + 1) * C] = (_dot_nt(q.astype(BF16), k.astype(BF16)) * decay_of(g, h)).astype(BF16)


def _gdn_scan_kernel(u_ref, w_ref, qd_ref, kt_ref, ai_ref, eg_ref, gz_ref, s0_ref, gout_ref, o_ref, sfin_ref, s_scr,
                     *, C, GB, n_chunks):
    c = pl.program_id(1)

    @pl.when(c == 0)
    def _():
        s_scr[...] = s0_ref[...]

    chains = [(b, h) for b in range(GB) for h in range(HEADS)]
    cols = lambda h: slice(h * HEAD_W, (h + 1) * HEAD_W)
    s16 = {bh: s_scr[bh[0], bh[1]].astype(BF16) for bh in chains}
    v_new = {(b, h): (u_ref[b, :, cols(h)] - _dot(w_ref[b, :, cols(h)], s16[b, h])).astype(BF16) for b, h in chains}
    o = {(b, h): _dot(qd_ref[b, :, cols(h)], s16[b, h]) + _dot(ai_ref[b, :, h * C:(h + 1) * C], v_new[b, h])
         for b, h in chains}
    for b, h in chains:
        s_scr[b, h] = s_scr[b, h] * eg_ref[b, 0:1, h:h + 1] + lax.dot_general(
            kt_ref[b, :, cols(h)], v_new[b, h], (((0,), (0,)), ((), ())), preferred_element_type=F32)
    for b, h in chains:
        o_ref[b, :, cols(h)] = (_rms(o[b, h], gout_ref[...]) * _silu(gz_ref[b, :, cols(h)])).astype(BF16)

    @pl.when(c == n_chunks - 1)
    def _():
        sfin_ref[...] = s_scr[...]


def _gdn(q, k, v, gz, gates, s0, g_out, C):
    B, T, _ = q.shape
    nc = T // C
    G = min(nc, 8)
    GB = next(n for n in (4, 2, 1) if B % n == 0)
    HC = HEADS * C
    pblk = lambda w: pl.BlockSpec((1, G * C, w), lambda b, c: (b, c, 0))
    row = lambda w, dt: jax.ShapeDtypeStruct((B, T, w), dt)
    u, w, qd, kt, ai, eg = pl.pallas_call(
        functools.partial(_gdn_prep_kernel, C=C, G=G),
        grid=(B, nc // G),
        in_specs=[pblk(MIX_HALF)] * 3 + [pblk(LANES)],
        out_specs=[pblk(MIX_HALF)] * 4 + [pblk(HC), pl.BlockSpec((1, G * SUBLANES, LANES), lambda b, c: (b, c, 0))],
        out_shape=[row(MIX_HALF, F32), row(MIX_HALF, BF16), row(MIX_HALF, BF16), row(MIX_HALF, BF16), row(HC, BF16),
                   jax.ShapeDtypeStruct((B, nc * SUBLANES, LANES), F32)],
        compiler_params=_params(("parallel", "parallel")),
        name="gdn_prep",
    )(q, k, v, gates)
    sblk = lambda w: pl.BlockSpec((GB, C, w), lambda b, c: (b, c, 0))
    st = pl.BlockSpec((GB, HEADS, HEAD_W, HEAD_W), lambda b, c: (b, 0, 0, 0))
    return pl.pallas_call(
        functools.partial(_gdn_scan_kernel, C=C, GB=GB, n_chunks=nc),
        grid=(B // GB, nc),
        in_specs=[sblk(MIX_HALF)] * 4 + [sblk(HC), pl.BlockSpec((GB, SUBLANES, LANES), lambda b, c: (b, c, 0)),
                                         sblk(MIX_HALF), st, pl.BlockSpec(g_out.shape, lambda b, c: (0, 0))],
        out_specs=[sblk(MIX_HALF), st],
        out_shape=[jax.ShapeDtypeStruct((B, T, MIX_HALF), BF16),
                   jax.ShapeDtypeStruct((B, HEADS, HEAD_W, HEAD_W), F32)],
        scratch_shapes=[pltpu.VMEM((GB, HEADS, HEAD_W, HEAD_W), F32)],
        compiler_params=_params(("parallel", "arbitrary")),
        name="gdn_scan",
    )(u, w, qd, kt, ai, eg, gz, s0, g_out)


def _memkv_kernel(m_ref, g_ref, wk_ref, wv_ref, kf_ref, kb_ref, vf_ref, vb_ref):
    m = _rms(m_ref[...], g_ref[...]).astype(BF16)
    mk = _dot(m, wk_ref[...])
    mv = _dot(m, wv_ref[...])
    kf_ref[...] = mk
    kb_ref[...] = mk.astype(BF16)
    vf_ref[...] = mv
    vb_ref[...] = mv.astype(BF16)


def _memkv(mem, g_mem, w_ck, w_cv, tm):
    N, D = mem.shape
    blk = pl.BlockSpec((tm, D), lambda i: (i, 0))
    full = lambda a: pl.BlockSpec(a.shape, lambda i: (0,) * a.ndim)
    return pl.pallas_call(
        _memkv_kernel,
        grid=(N // tm,),
        in_specs=[blk, full(g_mem), full(w_ck), full(w_cv)],
        out_specs=[blk] * 4,
        out_shape=[jax.ShapeDtypeStruct((N, D), F32), jax.ShapeDtypeStruct((N, D), BF16)] * 2,
        compiler_params=_params(("parallel",)),
        name="memkv",
    )(mem, g_mem, w_ck, w_cv)


def _mid_kernel(x_ref, da_ref, gd_ref, wo_ref, gc_ref, wq_ref, mk_ref, mv_ref, wco_ref, gf_ref, wr_ref, br_ref,
                x2_ref, hf_ref, comb_ref):
    x1 = x_ref[0] + _dot(da_ref[0], wo_ref[0:MIX_HALF, :]) + _dot(gd_ref[0], wo_ref[MIX_HALF:2 * MIX_HALF, :])
    hc = _rms(x1, gc_ref[...]).astype(BF16)
    q = (_dot(hc, wq_ref[...]) * (X_DH ** -0.5)).astype(BF16)
    heads = []
    for h in range(X_HEADS):
        sl = slice(h * X_DH, (h + 1) * X_DH)
        s = _dot_nt(q[:, sl], mk_ref[0, :, sl])
        p = jnp.exp(s - jnp.max(s, axis=-1, keepdims=True))
        l = jnp.sum(p, axis=-1, keepdims=True)
        heads.append((_dot(p.astype(BF16), mv_ref[0, :, sl]) / l).astype(BF16))
    x2 = x1 + _dot(jnp.concatenate(heads, axis=1), wco_ref[...])
    x2_ref[0] = x2
    hf = _rms(x2, gf_ref[...])
    hf_ref[0] = hf.astype(BF16)

    logits = _dot3(hf, wr_ref[...]) + br_ref[...]
    lane_i = lax.broadcasted_iota(jnp.int32, logits.shape, 1)
    lane = lane_i.astype(F32)
    big = float(LANES)
    is_grp = (lane_i >= N_EXPERTS) & (lane_i < N_EXPERTS + N_GROUPS)
    gl = jnp.where(is_grp, logits, NEG_INF)
    gmax = jnp.max(gl, axis=-1, keepdims=True)
    g_top = 1.0 / jnp.sum(jnp.exp(gl - gmax), axis=-1, keepdims=True)
    g_idx = jnp.min(jnp.where(gl == gmax, lane, big), axis=-1, keepdims=True) - float(N_EXPERTS)
    in_grp = (lane_i < N_EXPERTS) & (jnp.floor(lane * (1.0 / E_PER_GROUP)) == g_idx)
    el = jnp.where(in_grp, logits, NEG_INF)
    e1 = jnp.max(el, axis=-1, keepdims=True)
    i1 = jnp.min(jnp.where(el == e1, lane, big), axis=-1, keepdims=True)
    el2 = jnp.where(lane == i1, NEG_INF, el)
    e2 = jnp.max(el2, axis=-1, keepdims=True)
    i2 = jnp.min(jnp.where(el2 == e2, lane, big), axis=-1, keepdims=True)
    r = jnp.exp(e2 - e1)
    w1 = g_top / (1.0 + r)
    w2 = g_top * r / (1.0 + r)
    comb_ref[0] = jnp.where(lane_i == 0, i1, jnp.where(lane_i == 1, i2, jnp.where(
        lane_i == 2, w1, jnp.where(lane_i == 3, w2, 0.0))))


def _mid(x, mix_da, mix_gd, w_out, g_cross, w_cq, mk, mv, w_co, g_ffn, w_router, b_router, tm):
    B, T, D = x.shape
    M = mk.shape[1]
    blk = lambda w: pl.BlockSpec((1, tm, w), lambda b, t: (b, t, 0))
    mem = pl.BlockSpec((1, M, D), lambda b, t: (b, 0, 0))
    full = lambda a: pl.BlockSpec(a.shape, lambda b, t: (0,) * a.ndim)
    return pl.pallas_call(
        _mid_kernel,
        grid=(B, T // tm),
        in_specs=[blk(D), blk(MIX_HALF), blk(MIX_HALF), full(w_out), full(g_cross), full(w_cq), mem, mem,
                  full(w_co), full(g_ffn), full(w_router), full(b_router)],
        out_specs=[blk(D), blk(D), blk(LANES)],
        out_shape=[jax.ShapeDtypeStruct((B, T, D), F32), jax.ShapeDtypeStruct((B, T, D), BF16),
                   jax.ShapeDtypeStruct((B, T, LANES), F32)],
        compiler_params=_params(("parallel", "parallel")),
        name="mid",
    )(x, mix_da, mix_gd, w_out, g_cross, w_cq, mk, mv, w_co, g_ffn, w_router, b_router)


MOE_BLK = 16
EXP_TILE_BLKS = 16


def _slots_per_tile(tm):
    worst = 2 * tm + N_EXPERTS * (MOE_BLK - 1)
    return -(-worst // LANES) * LANES


def _route_kernel(hf_ref, rec_ref, xs_ref, ws_ref, slot_ref, meta_ref, *, tm, sl):
    rec = rec_ref[...]
    lane_i = lax.broadcasted_iota(jnp.int32, rec.shape, 1)
    lane = lane_i.astype(F32)
    i1, i2, w1, w2 = rec[:, 0:1], rec[:, 1:2], rec[:, 2:3], rec[:, 3:4]
    oh1 = jnp.where(lane == i1, 1.0, 0.0)
    oh2 = jnp.where(lane == i2, 1.0, 0.0)
    oh = oh1 + oh2
    nblk = jnp.floor((jnp.sum(oh, axis=0, keepdims=True) + (MOE_BLK - 1.0)) * (1.0 / MOE_BLK))
    ur = lax.broadcasted_iota(jnp.int32, (LANES, LANES), 0)
    uc = lax.broadcasted_iota(jnp.int32, (LANES, LANES), 1)
    start = MOE_BLK * _dot(jnp.broadcast_to(nblk, (2 * SUBLANES, LANES)).astype(BF16),
                           jnp.where(ur < uc, 1.0, 0.0).astype(BF16))[0:1]
    r = lax.broadcasted_iota(jnp.int32, (tm, tm), 0)
    c = lax.broadcasted_iota(jnp.int32, (tm, tm), 1)
    pos = start + _dot(jnp.where(r > c, 1.0, 0.0).astype(BF16), oh.astype(BF16))
    slot1 = jnp.sum(oh1 * pos, axis=-1, keepdims=True)
    slot2 = jnp.sum(oh2 * pos, axis=-1, keepdims=True)
    slot_ref[...] = jnp.where(lane_i == 0, slot1, jnp.where(lane_i == 1, slot2, 0.0))
    meta_ref[0] = jnp.broadcast_to(nblk, (SUBLANES, LANES))

    srow = lax.broadcasted_iota(jnp.int32, (sl, tm), 0).astype(F32)
    p1 = jnp.where(srow == jnp.broadcast_to(slot1, (tm, LANES)).T[0:1, :], 1.0, 0.0).astype(BF16)
    p2 = jnp.where(srow == jnp.broadcast_to(slot2, (tm, LANES)).T[0:1, :], 1.0, 0.0).astype(BF16)
    xs_ref[...] = _dot(p1 + p2, hf_ref[...]).astype(BF16)

    def parts(w):
        hi = w.astype(BF16).astype(F32)
        mid = (w - hi).astype(BF16).astype(F32)
        lo = w - hi - mid
        return jnp.where(lane_i == 0, hi, jnp.where(lane_i == 1, mid, jnp.where(lane_i == 2, lo, 0.0))).astype(BF16)

    ws_ref[...] = _dot(p1, parts(w1)) + _dot(p2, parts(w2))


def _route(hf, rec, tm):
    N, D = hf.shape
    nt = N // tm
    sl = _slots_per_tile(tm)
    blk = lambda rows, w: pl.BlockSpec((rows, w), lambda i: (i, 0))
    return pl.pallas_call(
        functools.partial(_route_kernel, tm=tm, sl=sl),
        grid=(nt,),
        in_specs=[blk(tm, D), blk(tm, LANES)],
        out_specs=[blk(sl, D), blk(sl, LANES), blk(tm, LANES), pl.BlockSpec((1, SUBLANES, LANES), lambda i: (i, 0, 0))],
        out_shape=[jax.ShapeDtypeStruct((nt * sl, D), BF16), jax.ShapeDtypeStruct((nt * sl, LANES), F32),
                   jax.ShapeDtypeStruct((N, LANES), F32), jax.ShapeDtypeStruct((nt, SUBLANES, LANES), F32)],
        compiler_params=_params(("parallel",)),
        name="moe_route",
    )(hf, rec)


def _moe_tables(meta, blocks_per_tile, max_tiles):
    nblk = meta[:, 0, :N_EXPERTS].astype(jnp.int32)
    nt = nblk.shape[0]
    tiles_e = (jnp.sum(nblk, axis=0) + EXP_TILE_BLKS - 1) // EXP_TILE_BLKS
    tile_end = jnp.cumsum(tiles_e)
    first_blk = (tile_end - tiles_e) * EXP_TILE_BLKS
    off = first_blk[None, :] + jnp.cumsum(nblk, axis=0) - nblk
    lend = jnp.cumsum(nblk, axis=1)
    lstart = lend - nblk
    lb = jnp.arange(blocks_per_tile, dtype=jnp.int32)[None, :]
    in_run = (lstart[:, None, :] <= lb[:, :, None]) & (lb[:, :, None] < lend[:, None, :])
    valid = lb < lend[:, -1:]
    dest = jnp.sum(jnp.where(in_run, (off - lstart)[:, None, :], 0), axis=-1) + lb
    flat = jnp.arange(nt, dtype=jnp.int32)[:, None] * blocks_per_tile + lb
    n_sorted = max_tiles * EXP_TILE_BLKS
    src = jnp.full((n_sorted,), -1, jnp.int32).at[jnp.where(valid, dest, n_sorted).reshape(-1)].set(
        flat.reshape(-1), mode="drop")
    g = jnp.arange(max_tiles, dtype=jnp.int32)
    tile_expert = jnp.minimum(jnp.sum(tile_end[None, :] <= g[:, None], axis=-1), N_EXPERTS - 1).astype(jnp.int32)
    return src, tile_expert, tile_end[-1:].astype(jnp.int32)


def _experts_kernel(src_ref, te_ref, nt_ref, xs_hbm, ws_hbm, wg_ref, wu_ref, wd_ref, ys_hbm,
                    xbuf, wbuf, obuf, gsem, ssem):
    g = pl.program_id(0)
    n_tiles = nt_ref[0]
    nb = EXP_TILE_BLKS

    def for_blocks(tile, fn):
        for i in range(nb):
            s = src_ref[tile * nb + i]

            @pl.when(s >= 0)
            def _():
                fn(i, s)

    def gather_copies(slot, i, blk):
        return (pltpu.make_async_copy(xs_hbm.at[blk], xbuf.at[slot, i], gsem.at[slot]),
                pltpu.make_async_copy(ws_hbm.at[blk], wbuf.at[slot, i], gsem.at[slot]))

    def gather_start(tile, slot):
        for_blocks(tile, lambda i, blk: [cp.start() for cp in gather_copies(slot, i, blk)])

    def gather_wait(tile, slot):
        for_blocks(tile, lambda i, blk: [cp.wait() for cp in gather_copies(slot, i, blk)])

    def scatter_copy(slot, i, blk):
        return pltpu.make_async_copy(obuf.at[slot, i], ys_hbm.at[blk], ssem.at[slot])

    def scatter_start(tile, slot):
        for_blocks(tile, lambda i, blk: scatter_copy(slot, i, blk).start())

    def scatter_wait(tile, slot):
        for_blocks(tile, lambda i, blk: scatter_copy(slot, i, blk).wait())

    slot = g % 2

    @pl.when(g == 0)
    def _():
        xbuf[...] = jnp.zeros(xbuf.shape, BF16)
        wbuf[...] = jnp.zeros(wbuf.shape, F32)
        gather_start(0, 0)

    @pl.when(g < n_tiles)
    def _():
        @pl.when(g + 1 < n_tiles)
        def _():
            gather_start(g + 1, 1 - slot)

        gather_wait(g, slot)
        rows = nb * MOE_BLK
        x = xbuf[slot].reshape(rows, xbuf.shape[-1])
        hg = _silu(_dot(x, wg_ref[0])) * _dot(x, wu_ref[0])
        w = wbuf[slot].reshape(rows, LANES)
        lane = lax.broadcasted_iota(jnp.int32, w.shape, 1)
        ccol = jnp.sum(jnp.where(lane < 3, w, 0.0), axis=-1, keepdims=True)
        y = _dot((hg * ccol).astype(BF16), wd_ref[0]).astype(BF16)

        @pl.when(g >= 2)
        def _():
            scatter_wait(g - 2, slot)

        obuf[slot] = y.reshape(nb, MOE_BLK, y.shape[-1])
        scatter_start(g, slot)

    @pl.when(g == n_tiles - 1)
    def _():
        scatter_wait(g, slot)

        @pl.when(g >= 1)
        def _():
            scatter_wait(g - 1, 1 - slot)


def _experts(src, tile_expert, n_tiles, xs, ws, w_gate, w_up, w_down, max_tiles):
    n_rows, D = xs.shape
    slab = lambda a: a.reshape(n_rows // MOE_BLK, MOE_BLK, a.shape[1])
    buf = lambda w, dt: pltpu.VMEM((2, EXP_TILE_BLKS, MOE_BLK, w), dt)
    wspec = lambda shape: pl.BlockSpec((1,) + shape, lambda g, src, te, nt: (te[g], 0, 0))
    grid_spec = pltpu.PrefetchScalarGridSpec(
        num_scalar_prefetch=3,
        grid=(max_tiles,),
        in_specs=[pl.BlockSpec(memory_space=pl.ANY), pl.BlockSpec(memory_space=pl.ANY),
                  wspec((D, D_EXPERT)), wspec((D, D_EXPERT)), wspec((D_EXPERT, D))],
        out_specs=pl.BlockSpec(memory_space=pl.ANY),
        scratch_shapes=[buf(D, BF16), buf(LANES, F32), buf(D, BF16), pltpu.SemaphoreType.DMA((2,)),
                        pltpu.SemaphoreType.DMA((2,))],
    )
    return pl.pallas_call(
        _experts_kernel,
        grid_spec=grid_spec,
        out_shape=jax.ShapeDtypeStruct(slab(xs).shape, BF16),
        input_output_aliases={3: 0},
        compiler_params=_params(("arbitrary",)),
        name="moe_experts",
    )(src, tile_expert, n_tiles, slab(xs), slab(ws), w_gate, w_up, w_down).reshape(n_rows, D)


def _combine_kernel(x2_ref, slot_ref, ys_ref, gfin_ref, y_ref, *, tm, sl):
    slots = slot_ref[...]
    col = lax.broadcasted_iota(jnp.int32, (tm, sl), 1).astype(F32)
    pick = (jnp.where(col == slots[:, 0:1], 1.0, 0.0) + jnp.where(col == slots[:, 1:2], 1.0, 0.0)).astype(BF16)
    y_ref[...] = _rms(x2_ref[...] + _dot(pick, ys_ref[...]), gfin_ref[...])


def _combine(x2, slots, ys, g_final, tm):
    N, D = x2.shape
    sl = ys.shape[0] // (N // tm)
    blk = lambda rows, w: pl.BlockSpec((rows, w), lambda i: (i, 0))
    return pl.pallas_call(
        functools.partial(_combine_kernel, tm=tm, sl=sl),
        grid=(N // tm,),
        in_specs=[blk(tm, D), blk(tm, LANES), blk(sl, D), pl.BlockSpec(g_final.shape, lambda i: (0, 0))],
        out_specs=blk(tm, D),
        out_shape=jax.ShapeDtypeStruct((N, D), F32),
        compiler_params=_params(("parallel",)),
        name="moe_combine",
    )(x2, slots, ys, g_final)


def _moe(x2, hf, rec, w_gate, w_up, w_down, g_final, tm):
    N, _ = x2.shape
    xs, ws, slots, meta = _route(hf, rec, tm)
    blocks_per_tile = _slots_per_tile(tm) // MOE_BLK
    max_tiles = (N // tm) * blocks_per_tile // EXP_TILE_BLKS + N_EXPERTS
    src, tile_expert, n_tiles = _moe_tables(meta, blocks_per_tile, max_tiles)
    ys = _experts(src, tile_expert, n_tiles, xs, ws, w_gate, w_up, w_down, max_tiles)
    return _combine(x2, slots, ys, g_final, tm)


def _pad_rows(a, rows):
    return jnp.pad(a, ((0, rows - a.shape[0]), (0, LANES - a.shape[1])))


def _trunk(x, mk16, mv16, past_k, past_v, conv_prev, s_prev, lam_init, wts):
    B, T, D = x.shape
    tm = min(T, 512)
    if conv_prev is None:
        conv_prev = jnp.zeros((B, CONV_W - 1, GDN_QKV), F32)
    if s_prev is None:
        s_prev = jnp.zeros((B, HEADS, HEAD_W, HEAD_W), F32)
    prev8 = jnp.pad(conv_prev, ((0, 0), (SUBLANES - (CONV_W - 1), 0), (0, 0)))
    (dq, dkf, dk16, dvf, dv16, cq, ck, cv, gz, gates, cnew) = _inproj(
        x, prev8, wts["g_mix"], wts["w_main"], wts["w_gate8"], wts["conv_w8"], wts["gate_par"], tm)
    if past_k is None:
        mix_da = _da_prompt(wts["lam_par"], wts["g_subln"], dq, dk16, dv16, min(T, 512), lam_init)
    else:
        P = past_k.shape[1]
        mix_da = _da_sample(wts["lam_par"], wts["g_subln"], dq, past_k.reshape(B, P * HEADS, HEAD_W),
                            past_v.reshape(B, P * HEADS, HEAD_W), dk16, dv16, lam_init)
    C = min(T, CHUNK)
    mix_gd, s_new = _gdn(cq, ck, cv, gz, gates, s_prev, wts["g_gdn_out"], C)
    x2, hf, comb = _mid(x, mix_da, mix_gd, wts["w_out"], wts["g_cross"], wts["w_cq"], mk16, mv16, wts["w_co"],
                        wts["g_ffn"], wts["w_router"], wts["b_router"], min(T, 512))
    N = B * T
    y = _moe(x2.reshape(N, D), hf.reshape(N, D), comb.reshape(N, LANES), wts["w_gate"], wts["w_up"],
             wts["w_down"], wts["g_final"], min(N, 512))
    new_k = dkf.reshape(1, B, T, HEADS, HEAD_W)
    new_v = dvf.reshape(1, B, T, HEADS, HEAD_W)
    new_conv = cnew[:, SUBLANES - (CONV_W - 1):, :].reshape(1, B, CONV_W - 1, GDN_QKV)
    return y.reshape(B, T, D), new_k, new_v, new_conv, s_new[None]


def kernel(x_prompt, x_sample, cache_diff_k, cache_diff_v, state_gdn_conv, state_gdn, cache_mem_k, cache_mem_v,
           mem_prompt, g_mix, w_in, lam_q1, lam_k1, lam_q2, lam_k2, g_subln, conv_w, a_log, dt_bias, g_gdn_out,
           w_out, g_cross, g_mem, w_ck, w_cv, w_cq, w_co, g_ffn, w_grp, b_grp, w_rt, b_rt, w_gate, w_up, w_down,
           g_final):
    assert g_mix.shape[0] == 1, "single-layer model"
    l = 0
    lam_init = 0.8 - 0.6 * math.exp(-0.3 * l)
    gate_lo = 3 * MIX_HALF + GDN_QKV
    gate_hi = gate_lo + 2 * HEADS
    wi = w_in[l]
    row = lambda a: a.reshape(1, -1).astype(F32)
    wts = dict(
        g_mix=row(g_mix[l]),
        w_main=jnp.concatenate([wi[:, :gate_lo], wi[:, gate_hi:]], axis=1).astype(BF16),
        w_gate8=jnp.pad(wi[:, gate_lo:gate_hi], ((0, 0), (0, LANES - 2 * HEADS))).astype(BF16),
        conv_w8=jnp.pad(conv_w[l], ((0, SUBLANES - CONV_W), (0, 0))),
        gate_par=_pad_rows(jnp.stack([a_log[l], dt_bias[l]]), SUBLANES),
        lam_par=_pad_rows(jnp.stack([lam_q1[l], lam_k1[l], lam_q2[l], lam_k2[l]]), SUBLANES),
        g_subln=row(g_subln[l]), g_gdn_out=row(g_gdn_out[l]),
        w_out=w_out[l].astype(BF16), g_cross=row(g_cross[l]), w_cq=w_cq[l].astype(BF16), w_co=w_co[l].astype(BF16),
        g_ffn=row(g_ffn[l]),
        w_router=jnp.pad(jnp.concatenate([w_rt[l], w_grp[l]], axis=1),
                         ((0, 0), (0, LANES - N_EXPERTS - N_GROUPS))),
        b_router=jnp.pad(jnp.concatenate([b_rt[l], b_grp[l]]), (0, LANES - N_EXPERTS - N_GROUPS)).reshape(1, LANES),
        w_gate=w_gate[l].astype(BF16), w_up=w_up[l].astype(BF16), w_down=w_down[l].astype(BF16),
        g_final=row(g_final),
    )
    B, M, D = mem_prompt.shape
    mkf, mk16, mvf, mv16 = _memkv(mem_prompt.reshape(B * M, D), row(g_mem[l]), w_ck[l].astype(BF16),
                                  w_cv[l].astype(BF16), min(B * M, 512))
    yp, pk, pv, pc, ps = _trunk(x_prompt, mk16.reshape(B, M, D), mv16.reshape(B, M, D), None, None, None, None,
                                lam_init, wts)
    Bs = x_sample.shape[0]
    ys, sk, sv, sc, ss = _trunk(x_sample, cache_mem_k[l].reshape(Bs, M, D).astype(BF16),
                                cache_mem_v[l].reshape(Bs, M, D).astype(BF16), cache_diff_k[l], cache_diff_v[l],
                                state_gdn_conv[l], state_gdn[l], lam_init, wts)
    mem_shape = (1, B, M, X_HEADS, X_DH)
    return (yp, ys, pk, pv, pc, ps, mkf.reshape(mem_shape), mvf.reshape(mem_shape), sk, sv, sc, ss)
```

```python
import functools
import math

import jax
import jax.numpy as jnp
import numpy as np
from jax import lax
from jax.experimental import pallas as pl
from jax.experimental.pallas import tpu as pltpu

F32 = jnp.float32
BF16 = jnp.bfloat16

D_MODEL = 1024
CHUNK = 64
HEADS = 4
HEAD_W = 128
DA_DH = 64
MIX_HALF = HEADS * HEAD_W
GDN_QKV = 3 * MIX_HALF
CONV_W = 4
X_HEADS = 4
X_DH = 256
N_GROUPS = 4
E_PER_GROUP = 8
N_EXPERTS = 32
D_EXPERT = 256
NORM_EPS = 1e-6
NEG_INF = -1e30
LANES = 128
SUBLANES = 8
VMEM_LIMIT = 56 * 1024 * 1024


def _dot(a, b):
    return jnp.dot(a, b, preferred_element_type=F32)


def _dot_nt(a, b):
    return lax.dot_general(a, b, (((1,), (1,)), ((), ())), preferred_element_type=F32)


def _rms(x, g):
    return x * lax.rsqrt(jnp.mean(x * x, axis=-1, keepdims=True) + NORM_EPS) * g


def _sigmoid(x):
    return 1.0 / (1.0 + jnp.exp(-x))


def _silu(x):
    return x * _sigmoid(x)


def _split_bf16(a):
    hi = a.astype(BF16)
    lo = (a - hi.astype(F32)).astype(BF16)
    return hi, lo


def _dot3(a, b):
    ah, al = _split_bf16(a)
    bh, bl = _split_bf16(b)
    return _dot(ah, bh) + (_dot(ah, bl) + _dot(al, bh))


def _params(sem):
    return pltpu.CompilerParams(dimension_semantics=sem, vmem_limit_bytes=VMEM_LIMIT)


def _inproj_kernel(x_ref, prev_ref, gmix_ref, wm_ref, wg_ref, cw_ref, gp_ref,
                   dq_ref, dkf_ref, dkb_ref, dvf_ref, dvb_ref, cq_ref, ck_ref, cv_ref, gz_ref, gate_ref,
                   cnew_ref, xp_scr, *, tm):
    t = pl.program_id(1)
    h = _rms(x_ref[0], gmix_ref[...]).astype(BF16)

    @pl.when(t == 0)
    def _():
        xp_scr[0:SUBLANES, :] = prev_ref[0]

    conv_col0 = 3 * MIX_HALF
    outs = (cq_ref, ck_ref, cv_ref)

    def conv_part(part):
        cs = slice(part * MIX_HALF, (part + 1) * MIX_HALF)
        xp_scr[SUBLANES:SUBLANES + tm, cs] = _dot(
            h, wm_ref[:, conv_col0 + part * MIX_HALF:conv_col0 + (part + 1) * MIX_HALF])
        y = xp_scr[5:5 + tm, cs] * cw_ref[0:1, cs]
        for j in range(1, CONV_W):
            y = y + xp_scr[5 + j:5 + j + tm, cs] * cw_ref[j:j + 1, cs]
        c = _silu(y)
        if part == 2:
            outs[part][0] = c
        else:
            scale = (HEAD_W ** -0.5) if part == 0 else 1.0
            for hh in range(HEADS):
                ch = c[:, hh * HEAD_W:(hh + 1) * HEAD_W]
                n = ch * lax.rsqrt(jnp.sum(ch * ch, axis=-1, keepdims=True) + NORM_EPS)
                outs[part][0, :, hh * HEAD_W:(hh + 1) * HEAD_W] = n * scale if part == 0 else n

    conv_part(0)
    dq = _dot(h, wm_ref[:, 0:MIX_HALF])
    dq_ref[0] = (dq * (DA_DH ** -0.5)).astype(BF16)
    conv_part(1)
    dk = _dot(h, wm_ref[:, MIX_HALF:2 * MIX_HALF])
    dkb_ref[0] = dk.astype(BF16)
    conv_part(2)
    dv = _dot(h, wm_ref[:, 2 * MIX_HALF:3 * MIX_HALF])
    dvb_ref[0] = dv.astype(BF16)
    for hh in range(HEADS):
        dkf_ref[0, pl.ds(hh, tm, stride=HEADS), :] = dk[:, hh * HEAD_W:(hh + 1) * HEAD_W]
        dvf_ref[0, pl.ds(hh, tm, stride=HEADS), :] = dv[:, hh * HEAD_W:(hh + 1) * HEAD_W]

    gates = _dot(h, wg_ref[...])
    z = gates + gp_ref[1:2, :]
    softplus = jnp.maximum(z, 0.0) + jnp.log1p(jnp.exp(-jnp.abs(z)))
    g_all = -jnp.exp(gp_ref[0:1, :]) * softplus
    lane = lax.broadcasted_iota(jnp.int32, gates.shape, 1)
    gate_ref[0] = jnp.where(lane < HEADS, g_all, jnp.where(lane < 2 * HEADS, _sigmoid(gates), 0.0))
    gz_ref[0] = _dot(h, wm_ref[:, 3 * MIX_HALF + GDN_QKV:4 * MIX_HALF + GDN_QKV])
    carry = xp_scr[tm:tm + SUBLANES, :]
    xp_scr[0:SUBLANES, :] = carry
    cnew_ref[0] = carry


def _inproj(x, prev8, g_mix, w_main, w_gate, conv_w8, gate_par, tm):
    B, T, D = x.shape
    nt = T // tm
    row = lambda w, dt: jax.ShapeDtypeStruct((B, T, w), dt)
    blk = lambda w: pl.BlockSpec((1, tm, w), lambda b, t: (b, t, 0))
    full = lambda a: pl.BlockSpec(a.shape, lambda b, t: (0,) * a.ndim)
    cache_blk = pl.BlockSpec((1, tm * HEADS, HEAD_W), lambda b, t: (b, t, 0))
    cache_rows = jax.ShapeDtypeStruct((B, T * HEADS, HEAD_W), F32)
    return pl.pallas_call(
        functools.partial(_inproj_kernel, tm=tm),
        grid=(B, nt),
        in_specs=[blk(D), pl.BlockSpec((1, SUBLANES, GDN_QKV), lambda b, t: (b, 0, 0)),
                  full(g_mix), full(w_main), full(w_gate), full(conv_w8), full(gate_par)],
        out_specs=[blk(MIX_HALF), cache_blk, blk(MIX_HALF), cache_blk] + [blk(MIX_HALF)] * 5
        + [blk(LANES), pl.BlockSpec((1, SUBLANES, GDN_QKV), lambda b, t: (b, 0, 0))],
        out_shape=[row(MIX_HALF, BF16), cache_rows, row(MIX_HALF, BF16), cache_rows,
                   row(MIX_HALF, BF16), row(MIX_HALF, F32), row(MIX_HALF, F32), row(MIX_HALF, F32),
                   row(MIX_HALF, F32), row(LANES, F32), jax.ShapeDtypeStruct((B, SUBLANES, GDN_QKV), F32)],
        scratch_shapes=[pltpu.VMEM((tm + SUBLANES, GDN_QKV), F32)],
        compiler_params=_params(("parallel", "arbitrary")),
        name="inproj",
    )(x, prev8, g_mix, w_main, w_gate, conv_w8, gate_par)


def _lam(lam_ref, lam_init):
    s1 = jnp.sum(lam_ref[0:1, :] * lam_ref[1:2, :], axis=-1, keepdims=True)
    s2 = jnp.sum(lam_ref[2:3, :] * lam_ref[3:4, :], axis=-1, keepdims=True)
    return jnp.exp(s1) - jnp.exp(s2) + lam_init


def _stack_q(q):
    lane = lax.broadcasted_iota(jnp.int32, q.shape, 1)
    zero = jnp.zeros_like(q)
    return jnp.concatenate([jnp.where(lane < DA_DH, q, zero), jnp.where(lane >= DA_DH, q, zero)], axis=0)


def _da_finish(acc, l, lam, gs, lam_init, tq):
    o = acc[:tq] / l[:tq] - lam * (acc[tq:] / l[tq:])
    return (_rms(o, gs) * (1.0 - lam_init)).astype(BF16)


def _da_prompt_kernel(lam_ref, gs_ref, q_ref, k_ref, v_ref, o_ref, vx_scr, m_scr, acc_scr, p_scr, al_scr, *,
                      tq, rows, heads, lam_init):
    i = pl.program_id(2)
    cols = lambda hh: slice(hh * HEAD_W, (hh + 1) * HEAD_W)

    @pl.when(i == 0)
    def _():
        for hh in range(heads):
            vx_scr[hh, :, 0:HEAD_W] = v_ref[0, :, cols(hh)]
            vx_scr[hh, :, HEAD_W:2 * HEAD_W] = jnp.ones((vx_scr.shape[1], HEAD_W), BF16)

    qz = [_stack_q(q_ref[0, :, cols(hh)]) for hh in range(heads)]
    m_scr[...] = jnp.full(m_scr.shape, NEG_INF, F32)
    acc_scr[...] = jnp.zeros(acc_scr.shape, F32)

    def keys_seen(r0):
        return min(tq, -(-((r0 % tq) + rows) // LANES) * LANES)

    def scores(hh, j, r0, masked):
        kw = keys_seen(r0) if masked else tq
        kj = k_ref[0, pl.ds(pl.multiple_of(j * tq, tq), kw), cols(hh)]
        s = _dot_nt(qz[hh][r0:r0 + rows], kj)
        if masked:
            r = lax.broadcasted_iota(jnp.int32, s.shape, 0) + (r0 % tq)
            c = lax.broadcasted_iota(jnp.int32, s.shape, 1)
            s = jnp.where((r // CHUNK) >= (c // CHUNK), s, NEG_INF)
        m_prev = m_scr[hh, r0:r0 + rows, :]
        m_new = jnp.maximum(m_prev, jnp.max(s, axis=-1, keepdims=True))
        al_scr[hh, r0:r0 + rows, :] = jnp.exp(m_prev - m_new)
        for c in range(kw // LANES):
            cs = slice(c * LANES, (c + 1) * LANES)
            p_scr[hh, r0:r0 + rows, cs] = jnp.exp(s[:, cs] - m_new).astype(BF16)
        m_scr[hh, r0:r0 + rows, :] = m_new

    def accumulate(hh, j, r0, diagonal):
        kw = keys_seen(r0) if diagonal else tq
        vj = vx_scr[hh, pl.ds(pl.multiple_of(j * tq, tq), kw), :]
        alpha = al_scr[hh, r0:r0 + rows, :]
        acc_scr[hh, r0:r0 + rows, :] = (jnp.concatenate([alpha, alpha], axis=1) * acc_scr[hh, r0:r0 + rows, :]
                                        + _dot(p_scr[hh, r0:r0 + rows, 0:kw], vj))

    def stages(acc_j=None, score_j=None, masked=False, diagonal=False):
        for r0 in range(0, 2 * tq, rows):
            for hh in range(heads):
                if acc_j is not None:
                    accumulate(hh, acc_j, r0, diagonal)
                if score_j is not None:
                    scores(hh, score_j, r0, masked)

    @pl.when(i == 0)
    def _():
        stages(score_j=0, masked=True)

    @pl.when(i > 0)
    def _():
        stages(score_j=0)

        def body(t, carry):
            stages(acc_j=t - 1, score_j=t)
            return carry

        lax.fori_loop(1, i, body, 0)
        stages(acc_j=i - 1, score_j=i, masked=True)

    stages(acc_j=i, diagonal=True)
    lam = _lam(lam_ref, lam_init)
    for hh in range(heads):
        acc = acc_scr[hh]
        o = acc[:, 0:HEAD_W] / acc[:, HEAD_W:2 * HEAD_W]
        o = o[0:tq] - lam * o[tq:2 * tq]
        o_ref[0, :, cols(hh)] = (_rms(o, gs_ref[...]) * (1.0 - lam_init)).astype(BF16)


def _da_prompt(lam_par, g_subln, q, k, v, tq, lam_init):
    B, T, _ = q.shape
    hg = 4
    kv_spec = pl.BlockSpec((1, T, hg * HEAD_W), lambda b, h, i: (b, 0, h))
    q_spec = pl.BlockSpec((1, tq, hg * HEAD_W), lambda b, h, i: (b, i, h))
    full = lambda a: pl.BlockSpec(a.shape, lambda b, h, i: (0,) * a.ndim)
    return pl.pallas_call(
        functools.partial(_da_prompt_kernel, tq=tq, rows=min(2 * tq, 128), heads=hg, lam_init=lam_init),
        grid=(B, HEADS // hg, T // tq),
        in_specs=[full(lam_par), full(g_subln), q_spec, kv_spec, kv_spec],
        out_specs=q_spec,
        out_shape=jax.ShapeDtypeStruct((B, T, MIX_HALF), BF16),
        scratch_shapes=[pltpu.VMEM((hg, T, 2 * HEAD_W), BF16), pltpu.VMEM((hg, 2 * tq, LANES), F32),
                        pltpu.VMEM((hg, 2 * tq, 2 * HEAD_W), F32), pltpu.VMEM((hg, 2 * tq, tq), BF16),
                        pltpu.VMEM((hg, 2 * tq, LANES), F32)],
        compiler_params=_params(("parallel", "parallel", "arbitrary")),
        name="diffattn_prompt",
    )(lam_par, g_subln, q, k, v)


def _da_sample_kernel(lam_ref, gs_ref, q_ref, pk_ref, pv_ref, k_ref, v_ref, o_ref, *, tq, past, lam_init):
    def visible(shape, k_off):
        r = lax.broadcasted_iota(jnp.int32, shape, 0)
        c = lax.broadcasted_iota(jnp.int32, shape, 1) + k_off
        r = jnp.where(r >= tq, r - tq, r) + past
        return (r // CHUNK) >= (c // CHUNK)

    for h in range(HEADS):
        sl = slice(h * HEAD_W, (h + 1) * HEAD_W)
        head_rows = pl.ds(h, past, stride=HEADS)
        qz = _stack_q(q_ref[0, :, sl])
        s_old = _dot_nt(qz, pk_ref[0, head_rows, :].astype(BF16))
        s_new = _dot_nt(qz, k_ref[0, :, sl])
        s_old = jnp.where(visible(s_old.shape, 0), s_old, NEG_INF)
        s_new = jnp.where(visible(s_new.shape, past), s_new, NEG_INF)
        m = jnp.maximum(jnp.max(s_old, axis=-1, keepdims=True), jnp.max(s_new, axis=-1, keepdims=True))
        p_old = jnp.exp(s_old - m)
        p_new = jnp.exp(s_new - m)
        l = jnp.sum(p_old, axis=-1, keepdims=True) + jnp.sum(p_new, axis=-1, keepdims=True)
        acc = (_dot(p_old.astype(BF16), pv_ref[0, head_rows, :].astype(BF16))
               + _dot(p_new.astype(BF16), v_ref[0, :, sl]))
        o_ref[0, :, sl] = _da_finish(acc, l, _lam(lam_ref, lam_init), gs_ref[...], lam_init, tq)


def _da_sample(lam_par, g_subln, q, past_k, past_v, k, v, lam_init):
    B, T, _ = q.shape
    P = past_k.shape[1] // HEADS
    new_spec = pl.BlockSpec((1, T, MIX_HALF), lambda b: (b, 0, 0))
    past_spec = pl.BlockSpec((1, P * HEADS, HEAD_W), lambda b: (b, 0, 0))
    full = lambda a: pl.BlockSpec(a.shape, lambda b: (0,) * a.ndim)
    return pl.pallas_call(
        functools.partial(_da_sample_kernel, tq=T, past=P, lam_init=lam_init),
        grid=(B,),
        in_specs=[full(lam_par), full(g_subln), new_spec, past_spec, past_spec, new_spec, new_spec],
        out_specs=new_spec,
        out_shape=jax.ShapeDtypeStruct((B, T, MIX_HALF), BF16),
        compiler_params=_params(("parallel",)),
        name="diffattn_sample",
    )(lam_par, g_subln, q, past_k, past_v, k, v)


def _gdn_prep_kernel(q_ref, k_ref, v_ref, gate_ref, u_ref, w_ref, qd_ref, kt_ref, ai_ref, eg_ref, *, C, G):
    HC = HEADS * C
    row = lax.broadcasted_iota(jnp.int32, (C, C), 0)
    col = lax.broadcasted_iota(jnp.int32, (C, C), 1)
    causal = row >= col
    strict = row > col
    arow = lax.broadcasted_iota(jnp.int32, (C, HC), 0)
    acol = lax.broadcasted_iota(jnp.int32, (C, HC), 1)
    eye_all = (arow == (acol % C)).astype(F32)
    brow = lax.broadcasted_iota(jnp.int32, (HC, HC), 0)
    bcolm = lax.broadcasted_iota(jnp.int32, (HC, HC), 1)
    same_head = (brow // C) == (bcolm // C)
    prow = lax.broadcasted_iota(jnp.int32, (LANES, C), 0)
    pcol = lax.broadcasted_iota(jnp.int32, (LANES, C), 1)
    tri = jnp.where((prow >= pcol) & (prow < C), 1.0, 0.0).astype(BF16)

    def cumsum_rows(x):
        hi = x.astype(BF16)
        mid = (x - hi.astype(F32)).astype(BF16)
        lo = (x - hi.astype(F32) - mid.astype(F32)).astype(BF16)
        return _dot(tri, hi) + (_dot(tri, mid) + _dot(tri, lo))

    def block_diag(m_all):
        return jnp.where(same_head, jnp.concatenate([m_all] * HEADS, axis=0), jnp.zeros((), m_all.dtype))

    gcs, gcts, ps, xs = [], [], [], []
    for g in range(G):
        rs = slice(g * C, (g + 1) * C)
        gates = gate_ref[0, rs, :]
        gc_pad = cumsum_rows(gates)
        gcts.append(gc_pad.T)
        gcs.append(gc_pad[0:C, :])
        eg_ref[0, g * SUBLANES:(g + 1) * SUBLANES, :] = jnp.broadcast_to(
            jnp.exp(gc_pad[C - 1:C, :]), (SUBLANES, LANES))

    def decay_of(g, h):
        return jnp.exp(jnp.where(causal, gcs[g][:, h:h + 1] - gcts[g][h:h + 1, 0:C], NEG_INF))

    for g in range(G):
        rs = slice(g * C, (g + 1) * C)
        a_heads = []
        for h in range(HEADS):
            k = k_ref[0, rs, h * HEAD_W:(h + 1) * HEAD_W]
            kb = k * gate_ref[0, rs, HEADS + h:HEADS + h + 1]
            q = q_ref[0, rs, h * HEAD_W:(h + 1) * HEAD_W]
            both = _dot_nt(jnp.concatenate([kb.astype(BF16), q.astype(BF16)], axis=0), k.astype(BF16))
            decay = decay_of(g, h)
            a_heads.append(jnp.where(strict, -(both[0:C] * decay), 0.0))
            ai_ref[0, rs, h * C:(h + 1) * C] = (both[C:2 * C] * decay).astype(BF16)
        ps.append(jnp.concatenate(a_heads, axis=1))
        xs.append(eye_all + ps[g])
    def times(lefts, right_diag):
        bh, bl = right_diag
        top = _dot(jnp.concatenate([part for split in lefts for part in split], axis=0), bh)
        bot = _dot(jnp.concatenate([split[0] for split in lefts], axis=0), bl)
        return [top[2 * n * C:(2 * n + 1) * C] + top[(2 * n + 1) * C:(2 * n + 2) * C] + bot[n * C:(n + 1) * C]
                for n in range(len(lefts))]

    n_factors = int(math.log2(C))
    p_split = [_split_bf16(ps[g]) for g in range(G)]
    for g in range(G):
        p_split[g] = _split_bf16(times([p_split[g]], tuple(block_diag(part) for part in p_split[g]))[0])
    for k in range(1, n_factors):
        for g in range(G):
            diag = tuple(block_diag(part) for part in p_split[g])
            if k < n_factors - 1:
                x_prod, p_next = times([_split_bf16(xs[g]), p_split[g]], diag)
                p_split[g] = _split_bf16(p_next)
            else:
                x_prod, = times([_split_bf16(xs[g])], diag)
            xs[g] = xs[g] + x_prod

    for g in range(G):
        rs = slice(g * C, (g + 1) * C)
        for h in range(HEADS):
            sl = slice(h * HEAD_W, (h + 1) * HEAD_W)
            q = q_ref[0, rs, sl]
            v = v_ref[0, rs, sl]
            k = k_ref[0, rs, sl]
            gcol = gcs[g][:, h:h + 1]
            bcol = gate_ref[0, rs, HEADS + h:HEADS + h + 1]
            glast = gcs[g][C - 1:C, h:h + 1]
            egc = jnp.exp(gcol)
            t16 = xs[g][:, h * C:(h + 1) * C].astype(BF16)
            uw = _dot(t16, jnp.concatenate([v * bcol, k * bcol * egc], axis=1).astype(BF16))
            u_ref[0, rs, sl] = uw[:, 0:HEAD_W]
            w_ref[0, rs, sl] = uw[:, HEAD_W:2 * HEAD_W].astype(BF16)
            qd_ref[0, rs, sl] = (q * egc).astype(BF16)
            kt_ref[0, rs, sl] = (k * jnp.exp(glast - gcol)).astype(BF16)


def _gdn_scan_kernel(u_ref, w_ref, qd_ref, kt_ref, ai_ref, eg_ref, gz_ref, s0_ref, gout_ref, o_ref, sfin_ref, s_scr,
                     *, C, GB, n_chunks):
    c = pl.program_id(1)

    @pl.when(c == 0)
    def _():
        s_scr[...] = s0_ref[...]

    chains = [(b, h) for b in range(GB) for h in range(HEADS)]
    cols = lambda h: slice(h * HEAD_W, (h + 1) * HEAD_W)
    s16 = {bh: s_scr[bh[0], bh[1]].astype(BF16) for bh in chains}
    v_new = {(b, h): (u_ref[b, :, cols(h)] - _dot(w_ref[b, :, cols(h)], s16[b, h])).astype(BF16) for b, h in chains}
    o = {(b, h): _dot(qd_ref[b, :, cols(h)], s16[b, h]) + _dot(ai_ref[b, :, h * C:(h + 1) * C], v_new[b, h])
         for b, h in chains}
    for b, h in chains:
        s_scr[b, h] = s_scr[b, h] * eg_ref[b, 0:1, h:h + 1] + lax.dot_general(
            kt_ref[b, :, cols(h)], v_new[b, h], (((0,), (0,)), ((), ())), preferred_element_type=F32)
    for b, h in chains:
        o_ref[b, :, cols(h)] = (_rms(o[b, h], gout_ref[...]) * _silu(gz_ref[b, :, cols(h)])).astype(BF16)

    @pl.when(c == n_chunks - 1)
    def _():
        sfin_ref[...] = s_scr[...]


def _gdn(q, k, v, gz, gates, s0, g_out, C):
    B, T, _ = q.shape
    nc = T // C
    G = min(nc, 8)
    GB = next(n for n in (4, 2, 1) if B % n == 0)
    HC = HEADS * C
    pblk = lambda w: pl.BlockSpec((1, G * C, w), lambda b, c: (b, c, 0))
    row = lambda w, dt: jax.ShapeDtypeStruct((B, T, w), dt)
    u, w, qd, kt, ai, eg = pl.pallas_call(
        functools.partial(_gdn_prep_kernel, C=C, G=G),
        grid=(B, nc // G),
        in_specs=[pblk(MIX_HALF)] * 3 + [pblk(LANES)],
        out_specs=[pblk(MIX_HALF)] * 4 + [pblk(HC), pl.BlockSpec((1, G * SUBLANES, LANES), lambda b, c: (b, c, 0))],
        out_shape=[row(MIX_HALF, F32), row(MIX_HALF, BF16), row(MIX_HALF, BF16), row(MIX_HALF, BF16), row(HC, BF16),
                   jax.ShapeDtypeStruct((B, nc * SUBLANES, LANES), F32)],
        compiler_params=_params(("parallel", "parallel")),
        name="gdn_prep",
    )(q, k, v, gates)
    sblk = lambda w: pl.BlockSpec((GB, C, w), lambda b, c: (b, c, 0))
    st = pl.BlockSpec((GB, HEADS, HEAD_W, HEAD_W), lambda b, c: (b, 0, 0, 0))
    return pl.pallas_call(
        functools.partial(_gdn_scan_kernel, C=C, GB=GB, n_chunks=nc),
        grid=(B // GB, nc),
        in_specs=[sblk(MIX_HALF)] * 4 + [sblk(HC), pl.BlockSpec((GB, SUBLANES, LANES), lambda b, c: (b, c, 0)),
                                         sblk(MIX_HALF), st, pl.BlockSpec(g_out.shape, lambda b, c: (0, 0))],
        out_specs=[sblk(MIX_HALF), st],
        out_shape=[jax.ShapeDtypeStruct((B, T, MIX_HALF), BF16),
                   jax.ShapeDtypeStruct((B, HEADS, HEAD_W, HEAD_W), F32)],
        scratch_shapes=[pltpu.VMEM((GB, HEADS, HEAD_W, HEAD_W), F32)],
        compiler_params=_params(("parallel", "arbitrary")),
        name="gdn_scan",
    )(u, w, qd, kt, ai, eg, gz, s0, g_out)


def _memkv_kernel(m_ref, g_ref, wk_ref, wv_ref, kf_ref, kb_ref, vf_ref, vb_ref):
    m = _rms(m_ref[...], g_ref[...]).astype(BF16)
    mk = _dot(m, wk_ref[...])
    mv = _dot(m, wv_ref[...])
    kf_ref[...] = mk
    kb_ref[...] = mk.astype(BF16)
    vf_ref[...] = mv
    vb_ref[...] = mv.astype(BF16)


def _memkv(mem, g_mem, w_ck, w_cv, tm):
    N, D = mem.shape
    blk = pl.BlockSpec((tm, D), lambda i: (i, 0))
    full = lambda a: pl.BlockSpec(a.shape, lambda i: (0,) * a.ndim)
    return pl.pallas_call(
        _memkv_kernel,
        grid=(N // tm,),
        in_specs=[blk, full(g_mem), full(w_ck), full(w_cv)],
        out_specs=[blk] * 4,
        out_shape=[jax.ShapeDtypeStruct((N, D), F32), jax.ShapeDtypeStruct((N, D), BF16)] * 2,
        compiler_params=_params(("parallel",)),
        name="memkv",
    )(mem, g_mem, w_ck, w_cv)


def _mid_kernel(x_ref, da_ref, gd_ref, wo_ref, gc_ref, wq_ref, mk_ref, mv_ref, wco_ref, gf_ref, wr_ref, br_ref,
                x2_ref, hf_ref, comb_ref):
    x1 = x_ref[0] + _dot(da_ref[0], wo_ref[0:MIX_HALF, :]) + _dot(gd_ref[0], wo_ref[MIX_HALF:2 * MIX_HALF, :])
    hc = _rms(x1, gc_ref[...]).astype(BF16)
    q = (_dot(hc, wq_ref[...]) * (X_DH ** -0.5)).astype(BF16)
    heads = []
    for h in range(X_HEADS):
        sl = slice(h * X_DH, (h + 1) * X_DH)
        s = _dot_nt(q[:, sl], mk_ref[0, :, sl])
        p = jnp.exp(s - jnp.max(s, axis=-1, keepdims=True))
        l = jnp.sum(p, axis=-1, keepdims=True)
        heads.append((_dot(p.astype(BF16), mv_ref[0, :, sl]) / l).astype(BF16))
    x2 = x1 + _dot(jnp.concatenate(heads, axis=1), wco_ref[...])
    x2_ref[0] = x2
    hf = _rms(x2, gf_ref[...])
    hf_ref[0] = hf.astype(BF16)

    logits = _dot3(hf, wr_ref[...]) + br_ref[...]
    lane_i = lax.broadcasted_iota(jnp.int32, logits.shape, 1)
    lane = lane_i.astype(F32)
    big = float(LANES)
    is_grp = (lane_i >= N_EXPERTS) & (lane_i < N_EXPERTS + N_GROUPS)
    gl = jnp.where(is_grp, logits, NEG_INF)
    gmax = jnp.max(gl, axis=-1, keepdims=True)
    g_top = 1.0 / jnp.sum(jnp.exp(gl - gmax), axis=-1, keepdims=True)
    g_idx = jnp.min(jnp.where(gl == gmax, lane, big), axis=-1, keepdims=True) - float(N_EXPERTS)
    in_grp = (lane_i < N_EXPERTS) & (jnp.floor(lane * (1.0 / E_PER_GROUP)) == g_idx)
    el = jnp.where(in_grp, logits, NEG_INF)
    e1 = jnp.max(el, axis=-1, keepdims=True)
    i1 = jnp.min(jnp.where(el == e1, lane, big), axis=-1, keepdims=True)
    el2 = jnp.where(lane == i1, NEG_INF, el)
    e2 = jnp.max(el2, axis=-1, keepdims=True)
    i2 = jnp.min(jnp.where(el2 == e2, lane, big), axis=-1, keepdims=True)
    r = jnp.exp(e2 - e1)
    w1 = g_top / (1.0 + r)
    w2 = g_top * r / (1.0 + r)
    comb_ref[0] = jnp.where(lane_i == 0, i1, jnp.where(lane_i == 1, i2, jnp.where(
        lane_i == 2, w1, jnp.where(lane_i == 3, w2, 0.0))))


def _mid(x, mix_da, mix_gd, w_out, g_cross, w_cq, mk, mv, w_co, g_ffn, w_router, b_router, tm):
    B, T, D = x.shape
    M = mk.shape[1]
    blk = lambda w: pl.BlockSpec((1, tm, w), lambda b, t: (b, t, 0))
    mem = pl.BlockSpec((1, M, D), lambda b, t: (b, 0, 0))
    full = lambda a: pl.BlockSpec(a.shape, lambda b, t: (0,) * a.ndim)
    return pl.pallas_call(
        _mid_kernel,
        grid=(B, T // tm),
        in_specs=[blk(D), blk(MIX_HALF), blk(MIX_HALF), full(w_out), full(g_cross), full(w_cq), mem, mem,
                  full(w_co), full(g_ffn), full(w_router), full(b_router)],
        out_specs=[blk(D), blk(D), blk(LANES)],
        out_shape=[jax.ShapeDtypeStruct((B, T, D), F32), jax.ShapeDtypeStruct((B, T, D), BF16),
                   jax.ShapeDtypeStruct((B, T, LANES), F32)],
        compiler_params=_params(("parallel", "parallel")),
        name="mid",
    )(x, mix_da, mix_gd, w_out, g_cross, w_cq, mk, mv, w_co, g_ffn, w_router, b_router)


MOE_BLK = 16
EXP_TILE_BLKS = 16


def _slots_per_tile(tm):
    worst = 2 * tm + N_EXPERTS * (MOE_BLK - 1)
    return -(-worst // LANES) * LANES


def _route_kernel(hf_ref, rec_ref, xs_ref, slot_ref, meta_ref, *, tm, sl, n_tiles):
    @pl.when(pl.program_id(0) == n_tiles)
    def _():
        xs_ref[...] = jnp.zeros(xs_ref.shape, BF16)
        slot_ref[...] = jnp.zeros(slot_ref.shape, F32)
        meta_ref[...] = jnp.zeros(meta_ref.shape, F32)

    @pl.when(pl.program_id(0) < n_tiles)
    def _():
        _route_tile(hf_ref, rec_ref, xs_ref, slot_ref, meta_ref, tm, sl)


def _route_tile(hf_ref, rec_ref, xs_ref, slot_ref, meta_ref, tm, sl):
    rec = rec_ref[...]
    lane_i = lax.broadcasted_iota(jnp.int32, rec.shape, 1)
    lane = lane_i.astype(F32)
    i1, i2, w1, w2 = rec[:, 0:1], rec[:, 1:2], rec[:, 2:3], rec[:, 3:4]
    oh1 = jnp.where(lane == i1, 1.0, 0.0)
    oh2 = jnp.where(lane == i2, 1.0, 0.0)
    oh = oh1 + oh2
    nblk = jnp.floor((jnp.sum(oh, axis=0, keepdims=True) + (MOE_BLK - 1.0)) * (1.0 / MOE_BLK))
    ur = lax.broadcasted_iota(jnp.int32, (LANES, LANES), 0)
    uc = lax.broadcasted_iota(jnp.int32, (LANES, LANES), 1)
    start = MOE_BLK * _dot(jnp.broadcast_to(nblk, (2 * SUBLANES, LANES)).astype(BF16),
                           jnp.where(ur < uc, 1.0, 0.0).astype(BF16))[0:1]
    r = lax.broadcasted_iota(jnp.int32, (tm, tm), 0)
    c = lax.broadcasted_iota(jnp.int32, (tm, tm), 1)
    pos = start + _dot(jnp.where(r > c, 1.0, 0.0).astype(BF16), oh.astype(BF16))
    slot1 = jnp.sum(oh1 * pos, axis=-1, keepdims=True)
    slot2 = jnp.sum(oh2 * pos, axis=-1, keepdims=True)
    slot_ref[...] = jnp.where(lane_i == 0, slot1, jnp.where(lane_i == 1, slot2, 0.0))
    meta_ref[0] = jnp.broadcast_to(nblk, (SUBLANES, LANES))

    srow = lax.broadcasted_iota(jnp.int32, (sl, tm), 0).astype(F32)
    perm = (jnp.where(srow == jnp.broadcast_to(slot1, (tm, LANES)).T[0:1, :], 1.0, 0.0)
            + jnp.where(srow == jnp.broadcast_to(slot2, (tm, LANES)).T[0:1, :], 2.0, 0.0)).astype(BF16)

    def parts(w, base):
        hi = w.astype(BF16).astype(F32)
        mid = (w - hi).astype(BF16).astype(F32)
        lo = w - hi - mid
        return jnp.where(lane_i == base, hi, jnp.where(lane_i == base + 1, mid,
                                                       jnp.where(lane_i == base + 2, lo, 0.0)))

    extra = parts(w1, 0) + parts(w2, 3) + jnp.where(lane_i == 6, 1.0, 0.0)
    moved = _dot(perm, jnp.concatenate([hf_ref[...], extra.astype(BF16)], axis=1))
    d = hf_ref.shape[1]
    second = moved[:, d + 6:d + 7] == 2.0
    moved = moved * jnp.where(second, 0.5, 1.0)
    lane_s = lax.broadcasted_iota(jnp.int32, (sl, LANES), 1)
    xs_ref[:, 0:d] = moved[:, 0:d].astype(BF16)
    xs_ref[:, d:d + LANES] = jnp.where(lane_s == 7, jnp.where(second, 1.0, 0.0), moved[:, d:d + LANES]).astype(BF16)


def _route(hf, rec, tm):
    N, D = hf.shape
    nt = N // tm
    sl = _slots_per_tile(tm)
    in_blk = lambda w: pl.BlockSpec((tm, w), lambda i: (jnp.minimum(i, nt - 1), 0))
    blk = lambda rows, w: pl.BlockSpec((rows, w), lambda i: (i, 0))
    return pl.pallas_call(
        functools.partial(_route_kernel, tm=tm, sl=sl, n_tiles=nt),
        grid=(nt + 1,),
        in_specs=[in_blk(D), in_blk(LANES)],
        out_specs=[blk(sl, D + LANES), blk(tm, LANES), pl.BlockSpec((1, SUBLANES, LANES), lambda i: (i, 0, 0))],
        out_shape=[jax.ShapeDtypeStruct(((nt + 1) * sl, D + LANES), BF16),
                   jax.ShapeDtypeStruct(((nt + 1) * tm, LANES), F32),
                   jax.ShapeDtypeStruct((nt + 1, SUBLANES, LANES), F32)],
        compiler_params=_params(("parallel",)),
        name="moe_route",
    )(hf, rec)


def _moe_tables(meta, blocks_per_tile, max_tiles):
    nblk = meta[:, 0, :N_EXPERTS].astype(jnp.int32)
    nt = nblk.shape[0]
    tiles_e = (jnp.sum(nblk, axis=0) + EXP_TILE_BLKS - 1) // EXP_TILE_BLKS
    tile_end = jnp.cumsum(tiles_e)
    first_blk = (tile_end - tiles_e) * EXP_TILE_BLKS
    off = first_blk[None, :] + jnp.cumsum(nblk, axis=0) - nblk
    lend = jnp.cumsum(nblk, axis=1)
    lstart = lend - nblk
    lb = jnp.arange(blocks_per_tile, dtype=jnp.int32)[None, :]
    in_run = (lstart[:, None, :] <= lb[:, :, None]) & (lb[:, :, None] < lend[:, None, :])
    valid = lb < lend[:, -1:]
    dest = jnp.sum(jnp.where(in_run, (off - lstart)[:, None, :], 0), axis=-1) + lb
    flat = jnp.arange(nt, dtype=jnp.int32)[:, None] * blocks_per_tile + lb
    n_sorted = max_tiles * EXP_TILE_BLKS
    src = jnp.full((n_sorted,), -1, jnp.int32).at[jnp.where(valid, dest, n_sorted).reshape(-1)].set(
        flat.reshape(-1), mode="drop")
    g = jnp.arange(max_tiles, dtype=jnp.int32)
    tile_expert = jnp.minimum(jnp.sum(tile_end[None, :] <= g[:, None], axis=-1), N_EXPERTS - 1).astype(jnp.int32)
    return src, tile_expert, tile_end[-1:].astype(jnp.int32)


def _experts_kernel(src_ref, te_ref, nt_ref, xs_hbm, wg_ref, wu_ref, wd_ref, ys_hbm, xbuf, obuf, gsem, ssem, *,
                    zero_blk, spare_blk):
    g = pl.program_id(0)
    n_tiles = nt_ref[0]
    nb = EXP_TILE_BLKS
    d = obuf.shape[-1]

    def gather_copy(tile, slot, i):
        s = src_ref[tile * nb + i]
        return pltpu.make_async_copy(xs_hbm.at[jnp.where(s >= 0, s, zero_blk)], xbuf.at[slot, i], gsem.at[slot])

    def scatter_copy(tile, slot, i):
        s = src_ref[tile * nb + i]
        dst = jnp.where(s >= 0, s, spare_blk + slot * nb + i)
        return pltpu.make_async_copy(obuf.at[slot, i], ys_hbm.at[dst, :, pl.ds(0, d)], ssem.at[slot])

    def gather_start(tile, slot):
        for i in range(nb):
            gather_copy(tile, slot, i).start()

    def gather_wait(tile, slot):
        for i in range(nb):
            gather_copy(tile, slot, i).wait()

    def scatter_start(tile, slot):
        for i in range(nb):
            scatter_copy(tile, slot, i).start()

    def scatter_wait(tile, slot):
        for i in range(nb):
            scatter_copy(tile, slot, i).wait()

    slot = g % 2

    @pl.when(g == 0)
    def _():
        gather_start(0, 0)

    @pl.when(g < n_tiles)
    def _():
        @pl.when(g + 1 < n_tiles)
        def _():
            gather_start(g + 1, 1 - slot)

        gather_wait(g, slot)
        rows = nb * MOE_BLK
        xe = xbuf[slot].reshape(rows, xbuf.shape[-1])
        x = xe[:, 0:d]
        extra = xe[:, d:d + LANES].astype(F32)
        lane = lax.broadcasted_iota(jnp.int32, extra.shape, 1)
        first_lane = 3.0 * jnp.sum(jnp.where(lane == 7, extra, 0.0), axis=-1, keepdims=True)
        lane_f = lane.astype(F32)
        mine = (lane_f >= first_lane) & (lane_f < first_lane + 3.0)
        ccol = jnp.sum(jnp.where(mine, extra, 0.0), axis=-1, keepdims=True)
        hg = _silu(_dot(x, wg_ref[0])) * _dot(x, wu_ref[0])
        y = _dot((hg * ccol).astype(BF16), wd_ref[0]).astype(BF16)

        @pl.when(g >= 2)
        def _():
            scatter_wait(g - 2, slot)

        obuf[slot] = y.reshape(nb, MOE_BLK, d)
        scatter_start(g, slot)

    @pl.when(g == n_tiles - 1)
    def _():
        scatter_wait(g, slot)

        @pl.when(g >= 1)
        def _():
            scatter_wait(g - 1, 1 - slot)


def _experts(src, tile_expert, n_tiles, xs, w_gate, w_up, w_down, max_tiles, zero_blk):
    n_rows, width = xs.shape
    D = width - LANES
    slab = xs.reshape(n_rows // MOE_BLK, MOE_BLK, width)
    wspec = lambda shape: pl.BlockSpec((1,) + shape, lambda g, src, te, nt: (te[g], 0, 0))
    grid_spec = pltpu.PrefetchScalarGridSpec(
        num_scalar_prefetch=3,
        grid=(max_tiles,),
        in_specs=[pl.BlockSpec(memory_space=pl.ANY), wspec((D, D_EXPERT)), wspec((D, D_EXPERT)), wspec((D_EXPERT, D))],
        out_specs=pl.BlockSpec(memory_space=pl.ANY),
        scratch_shapes=[pltpu.VMEM((2, EXP_TILE_BLKS, MOE_BLK, width), BF16),
                        pltpu.VMEM((2, EXP_TILE_BLKS, MOE_BLK, D), BF16),
                        pltpu.SemaphoreType.DMA((2,)), pltpu.SemaphoreType.DMA((2,))],
    )
    return pl.pallas_call(
        functools.partial(_experts_kernel, zero_blk=zero_blk, spare_blk=zero_blk + 1),
        grid_spec=grid_spec,
        out_shape=jax.ShapeDtypeStruct(slab.shape, BF16),
        input_output_aliases={3: 0},
        compiler_params=_params(("arbitrary",)),
        name="moe_experts",
    )(src, tile_expert, n_tiles, slab, w_gate, w_up, w_down).reshape(n_rows, width)


def _combine_kernel(x2_ref, slot_ref, ys_ref, gfin_ref, y_ref, *, tm, sl):
    slots = slot_ref[...]
    col = lax.broadcasted_iota(jnp.int32, (tm, sl), 1).astype(F32)
    pick = (jnp.where(col == slots[:, 0:1], 1.0, 0.0) + jnp.where(col == slots[:, 1:2], 1.0, 0.0)).astype(BF16)
    y_ref[...] = _rms(x2_ref[...] + _dot(pick, ys_ref[:, 0:x2_ref.shape[1]]), gfin_ref[...])


def _combine(x2, slots, ys, g_final, tm, sl):
    N, D = x2.shape
    blk = lambda rows, w: pl.BlockSpec((rows, w), lambda i: (i, 0))
    return pl.pallas_call(
        functools.partial(_combine_kernel, tm=tm, sl=sl),
        grid=(N // tm,),
        in_specs=[blk(tm, D), blk(tm, LANES), blk(sl, ys.shape[1]), pl.BlockSpec(g_final.shape, lambda i: (0, 0))],
        out_specs=blk(tm, D),
        out_shape=jax.ShapeDtypeStruct((N, D), F32),
        compiler_params=_params(("parallel",)),
        name="moe_combine",
    )(x2, slots, ys, g_final)


def _moe(x2, hf, rec, w_gate, w_up, w_down, g_final, tm):
    N, _ = x2.shape
    nt = N // tm
    sl = _slots_per_tile(tm)
    xs, slots, meta = _route(hf, rec, tm)
    blocks_per_tile = sl // MOE_BLK
    max_tiles = nt * blocks_per_tile // EXP_TILE_BLKS + N_EXPERTS
    src, tile_expert, n_tiles = _moe_tables(meta[:nt], blocks_per_tile, max_tiles)
    ys = _experts(src, tile_expert, n_tiles, xs, w_gate, w_up, w_down, max_tiles, nt * blocks_per_tile)
    return _combine(x2, slots, ys, g_final, tm, sl)


def _pad_rows(a, rows):
    return jnp.pad(a, ((0, rows - a.shape[0]), (0, LANES - a.shape[1])))


def _trunk(x, mk16, mv16, past_k, past_v, conv_prev, s_prev, lam_init, wts):
    B, T, D = x.shape
    tm = min(T, 512)
    if conv_prev is None:
        conv_prev = jnp.zeros((B, CONV_W - 1, GDN_QKV), F32)
    if s_prev is None:
        s_prev = jnp.zeros((B, HEADS, HEAD_W, HEAD_W), F32)
    prev8 = jnp.pad(conv_prev, ((0, 0), (SUBLANES - (CONV_W - 1), 0), (0, 0)))
    (dq, dkf, dk16, dvf, dv16, cq, ck, cv, gz, gates, cnew) = _inproj(
        x, prev8, wts["g_mix"], wts["w_main"], wts["w_gate8"], wts["conv_w8"], wts["gate_par"], tm)
    if past_k is None:
        mix_da = _da_prompt(wts["lam_par"], wts["g_subln"], dq, dk16, dv16, min(T, 512), lam_init)
    else:
        P = past_k.shape[1]
        mix_da = _da_sample(wts["lam_par"], wts["g_subln"], dq, past_k.reshape(B, P * HEADS, HEAD_W),
                            past_v.reshape(B, P * HEADS, HEAD_W), dk16, dv16, lam_init)
    C = min(T, CHUNK)
    mix_gd, s_new = _gdn(cq, ck, cv, gz, gates, s_prev, wts["g_gdn_out"], C)
    x2, hf, comb = _mid(x, mix_da, mix_gd, wts["w_out"], wts["g_cross"], wts["w_cq"], mk16, mv16, wts["w_co"],
                        wts["g_ffn"], wts["w_router"], wts["b_router"], min(T, 512))
    N = B * T
    y = _moe(x2.reshape(N, D), hf.reshape(N, D), comb.reshape(N, LANES), wts["w_gate"], wts["w_up"],
             wts["w_down"], wts["g_final"], min(N, 512))
    new_k = dkf.reshape(1, B, T, HEADS, HEAD_W)
    new_v = dvf.reshape(1, B, T, HEADS, HEAD_W)
    new_conv = cnew[:, SUBLANES - (CONV_W - 1):, :].reshape(1, B, CONV_W - 1, GDN_QKV)
    return y.reshape(B, T, D), new_k, new_v, new_conv, s_new[None]


def kernel(x_prompt, x_sample, cache_diff_k, cache_diff_v, state_gdn_conv, state_gdn, cache_mem_k, cache_mem_v,
           mem_prompt, g_mix, w_in, lam_q1, lam_k1, lam_q2, lam_k2, g_subln, conv_w, a_log, dt_bias, g_gdn_out,
           w_out, g_cross, g_mem, w_ck, w_cv, w_cq, w_co, g_ffn, w_grp, b_grp, w_rt, b_rt, w_gate, w_up, w_down,
           g_final):
    assert g_mix.shape[0] == 1, "single-layer model"
    l = 0
    lam_init = 0.8 - 0.6 * math.exp(-0.3 * l)
    gate_lo = 3 * MIX_HALF + GDN_QKV
    gate_hi = gate_lo + 2 * HEADS
    wi = w_in[l]
    row = lambda a: a.reshape(1, -1).astype(F32)
    wts = dict(
        g_mix=row(g_mix[l]),
        w_main=jnp.concatenate([wi[:, :gate_lo], wi[:, gate_hi:]], axis=1).astype(BF16),
        w_gate8=jnp.pad(wi[:, gate_lo:gate_hi], ((0, 0), (0, LANES - 2 * HEADS))).astype(BF16),
        conv_w8=jnp.pad(conv_w[l], ((0, SUBLANES - CONV_W), (0, 0))),
        gate_par=_pad_rows(jnp.stack([a_log[l], dt_bias[l]]), SUBLANES),
        lam_par=_pad_rows(jnp.stack([lam_q1[l], lam_k1[l], lam_q2[l], lam_k2[l]]), SUBLANES),
        g_subln=row(g_subln[l]), g_gdn_out=row(g_gdn_out[l]),
        w_out=w_out[l].astype(BF16), g_cross=row(g_cross[l]), w_cq=w_cq[l].astype(BF16), w_co=w_co[l].astype(BF16),
        g_ffn=row(g_ffn[l]),
        w_router=jnp.pad(jnp.concatenate([w_rt[l], w_grp[l]], axis=1),
                         ((0, 0), (0, LANES - N_EXPERTS - N_GROUPS))),
        b_router=jnp.pad(jnp.concatenate([b_rt[l], b_grp[l]]), (0, LANES - N_EXPERTS - N_GROUPS)).reshape(1, LANES),
        w_gate=w_gate[l].astype(BF16), w_up=w_up[l].astype(BF16), w_down=w_down[l].astype(BF16),
        g_final=row(g_final),
    )
    B, M, D = mem_prompt.shape
    mkf, mk16, mvf, mv16 = _memkv(mem_prompt.reshape(B * M, D), row(g_mem[l]), w_ck[l].astype(BF16),
                                  w_cv[l].astype(BF16), min(B * M, 512))
    yp, pk, pv, pc, ps = _trunk(x_prompt, mk16.reshape(B, M, D), mv16.reshape(B, M, D), None, None, None, None,
                                lam_init, wts)
    Bs = x_sample.shape[0]
    ys, sk, sv, sc, ss = _trunk(x_sample, cache_mem_k[l].reshape(Bs, M, D).astype(BF16),
                                cache_mem_v[l].reshape(Bs, M, D).astype(BF16), cache_diff_k[l], cache_diff_v[l],
                                state_gdn_conv[l], state_gdn[l], lam_init, wts)
    mem_shape = (1, B, M, X_HEADS, X_DH)
    return (yp, ys, pk, pv, pc, ps, mkf.reshape(mem_shape), mvf.reshape(mem_shape), sk, sv, sc, ss)
```

```python
import functools
import math

import jax
import jax.numpy as jnp
import numpy as np
from jax import lax
from jax.experimental import pallas as pl
from jax.experimental.pallas import tpu as pltpu

F32 = jnp.float32
BF16 = jnp.bfloat16

D_MODEL = 1024
CHUNK = 64
HEADS = 4
HEAD_W = 128
DA_DH = 64
MIX_HALF = HEADS * HEAD_W
GDN_QKV = 3 * MIX_HALF
CONV_W = 4
X_HEADS = 4
X_DH = 256
N_GROUPS = 4
E_PER_GROUP = 8
N_EXPERTS = 32
D_EXPERT = 256
NORM_EPS = 1e-6
NEG_INF = -1e30
LANES = 128
SUBLANES = 8
VMEM_LIMIT = 56 * 1024 * 1024


def _dot(a, b):
    return jnp.dot(a, b, preferred_element_type=F32)


def _dot_nt(a, b):
    return lax.dot_general(a, b, (((1,), (1,)), ((), ())), preferred_element_type=F32)


def _rms(x, g):
    return x * lax.rsqrt(jnp.mean(x * x, axis=-1, keepdims=True) + NORM_EPS) * g


def _sigmoid(x):
    return 1.0 / (1.0 + jnp.exp(-x))


def _silu(x):
    return x * _sigmoid(x)


def _split_bf16(a):
    hi = a.astype(BF16)
    lo = (a - hi.astype(F32)).astype(BF16)
    return hi, lo


def _dot3(a, b):
    ah, al = _split_bf16(a)
    bh, bl = _split_bf16(b)
    return _dot(ah, bh) + (_dot(ah, bl) + _dot(al, bh))


def _params(sem):
    return pltpu.CompilerParams(dimension_semantics=sem, vmem_limit_bytes=VMEM_LIMIT)


def _inproj_kernel(x_ref, prev_ref, gmix_ref, wm_ref, wg_ref, cw_ref, gp_ref,
                   dq_ref, dkf_ref, dkb_ref, dvf_ref, dvb_ref, cq_ref, ck_ref, cv_ref, gz_ref, gate_ref,
                   cnew_ref, xp_scr, h_scr, *, tm):
    t = pl.program_id(1)
    h_scr[...] = _rms(x_ref[0], gmix_ref[...]).astype(BF16)

    @pl.when(t == 0)
    def _():
        xp_scr[0:SUBLANES, :] = prev_ref[0]

    conv_col0 = 3 * MIX_HALF
    outs = (cq_ref, ck_ref, cv_ref)

    def conv_part(part):
        cs = slice(part * MIX_HALF, (part + 1) * MIX_HALF)
        xp_scr[SUBLANES:SUBLANES + tm, cs] = _dot(
            h_scr[...], wm_ref[:, conv_col0 + part * MIX_HALF:conv_col0 + (part + 1) * MIX_HALF])
        for hh in range(HEADS):
            hs = slice(part * MIX_HALF + hh * HEAD_W, part * MIX_HALF + (hh + 1) * HEAD_W)
            y = xp_scr[5:5 + tm, hs] * cw_ref[0:1, hs]
            for j in range(1, CONV_W):
                y = y + xp_scr[5 + j:5 + j + tm, hs] * cw_ref[j:j + 1, hs]
            c = _silu(y)
            if part < 2:
                c = c * lax.rsqrt(jnp.sum(c * c, axis=-1, keepdims=True) + NORM_EPS)
            if part == 0:
                c = c * (HEAD_W ** -0.5)
            outs[part][0, :, hh * HEAD_W:(hh + 1) * HEAD_W] = c

    conv_part(0)
    dq = _dot(h_scr[...],wm_ref[:, 0:MIX_HALF])
    dq_ref[0] = (dq * (DA_DH ** -0.5)).astype(BF16)
    conv_part(1)
    dk = _dot(h_scr[...],wm_ref[:, MIX_HALF:2 * MIX_HALF])
    dkb_ref[0] = dk.astype(BF16)
    conv_part(2)
    dv = _dot(h_scr[...],wm_ref[:, 2 * MIX_HALF:3 * MIX_HALF])
    dvb_ref[0] = dv.astype(BF16)
    for hh in range(HEADS):
        dkf_ref[0, pl.ds(hh, tm, stride=HEADS), :] = dk[:, hh * HEAD_W:(hh + 1) * HEAD_W]
        dvf_ref[0, pl.ds(hh, tm, stride=HEADS), :] = dv[:, hh * HEAD_W:(hh + 1) * HEAD_W]

    gates = _dot(h_scr[...],wg_ref[...])
    z = gates + gp_ref[1:2, :]
    softplus = jnp.maximum(z, 0.0) + jnp.log1p(jnp.exp(-jnp.abs(z)))
    g_all = -jnp.exp(gp_ref[0:1, :]) * softplus
    lane = lax.broadcasted_iota(jnp.int32, gates.shape, 1)
    gate_ref[0] = jnp.where(lane < HEADS, g_all, jnp.where(lane < 2 * HEADS, _sigmoid(gates), 0.0))
    gz_ref[0] = _dot(h_scr[...],wm_ref[:, 3 * MIX_HALF + GDN_QKV:4 * MIX_HALF + GDN_QKV])
    carry = xp_scr[tm:tm + SUBLANES, :]
    xp_scr[0:SUBLANES, :] = carry
    cnew_ref[0] = carry


def _inproj(x, prev8, g_mix, w_main, w_gate, conv_w8, gate_par, tm):
    B, T, D = x.shape
    nt = T // tm
    row = lambda w, dt: jax.ShapeDtypeStruct((B, T, w), dt)
    blk = lambda w: pl.BlockSpec((1, tm, w), lambda b, t: (b, t, 0))
    full = lambda a: pl.BlockSpec(a.shape, lambda b, t: (0,) * a.ndim)
    cache_blk = pl.BlockSpec((1, tm * HEADS, HEAD_W), lambda b, t: (b, t, 0))
    cache_rows = jax.ShapeDtypeStruct((B, T * HEADS, HEAD_W), F32)
    return pl.pallas_call(
        functools.partial(_inproj_kernel, tm=tm),
        grid=(B, nt),
        in_specs=[blk(D), pl.BlockSpec((1, SUBLANES, GDN_QKV), lambda b, t: (b, 0, 0)),
                  full(g_mix), full(w_main), full(w_gate), full(conv_w8), full(gate_par)],
        out_specs=[blk(MIX_HALF), cache_blk, blk(MIX_HALF), cache_blk] + [blk(MIX_HALF)] * 5
        + [blk(LANES), pl.BlockSpec((1, SUBLANES, GDN_QKV), lambda b, t: (b, 0, 0))],
        out_shape=[row(MIX_HALF, BF16), cache_rows, row(MIX_HALF, BF16), cache_rows,
                   row(MIX_HALF, BF16), row(MIX_HALF, F32), row(MIX_HALF, F32), row(MIX_HALF, F32),
                   row(MIX_HALF, F32), row(LANES, F32), jax.ShapeDtypeStruct((B, SUBLANES, GDN_QKV), F32)],
        scratch_shapes=[pltpu.VMEM((tm + SUBLANES, GDN_QKV), F32), pltpu.VMEM((tm, D), BF16)],
        compiler_params=_params(("parallel", "arbitrary")),
        name="inproj",
    )(x, prev8, g_mix, w_main, w_gate, conv_w8, gate_par)


def _lam(lam_ref, lam_init):
    s1 = jnp.sum(lam_ref[0:1, :] * lam_ref[1:2, :], axis=-1, keepdims=True)
    s2 = jnp.sum(lam_ref[2:3, :] * lam_ref[3:4, :], axis=-1, keepdims=True)
    return jnp.exp(s1) - jnp.exp(s2) + lam_init


def _stack_q(q):
    lane = lax.broadcasted_iota(jnp.int32, q.shape, 1)
    zero = jnp.zeros_like(q)
    return jnp.concatenate([jnp.where(lane < DA_DH, q, zero), jnp.where(lane >= DA_DH, q, zero)], axis=0)


def _da_finish(acc, l, lam, gs, lam_init, tq):
    o = acc[:tq] / l[:tq] - lam * (acc[tq:] / l[tq:])
    return (_rms(o, gs) * (1.0 - lam_init)).astype(BF16)


def _da_prompt_kernel(lam_ref, gs_ref, q_ref, k_ref, v_ref, o_ref, vx_scr, m_scr, acc_scr, p_scr, al_scr, *,
                      tq, rows, heads, lam_init):
    i = pl.program_id(2)
    cols = lambda hh: slice(hh * HEAD_W, (hh + 1) * HEAD_W)

    @pl.when(i == 0)
    def _():
        for hh in range(heads):
            vx_scr[hh, :, 0:HEAD_W] = v_ref[0, :, cols(hh)]
            vx_scr[hh, :, HEAD_W:2 * HEAD_W] = jnp.ones((vx_scr.shape[1], HEAD_W), BF16)

    qz = [_stack_q(q_ref[0, :, cols(hh)]) for hh in range(heads)]
    m_scr[...] = jnp.full(m_scr.shape, NEG_INF, F32)
    acc_scr[...] = jnp.zeros(acc_scr.shape, F32)

    def keys_seen(r0):
        return min(tq, -(-((r0 % tq) + rows) // LANES) * LANES)

    def scores(hh, j, r0, masked):
        kw = keys_seen(r0) if masked else tq
        kj = k_ref[0, pl.ds(pl.multiple_of(j * tq, tq), kw), cols(hh)]
        s = _dot_nt(qz[hh][r0:r0 + rows], kj)
        if masked:
            r = lax.broadcasted_iota(jnp.int32, s.shape, 0) + (r0 % tq)
            c = lax.broadcasted_iota(jnp.int32, s.shape, 1)
            s = jnp.where((r // CHUNK) >= (c // CHUNK), s, NEG_INF)
        m_prev = m_scr[hh, r0:r0 + rows, :]
        m_new = jnp.maximum(m_prev, jnp.max(s, axis=-1, keepdims=True))
        al_scr[hh, r0:r0 + rows, :] = jnp.exp(m_prev - m_new)
        for c in range(kw // LANES):
            cs = slice(c * LANES, (c + 1) * LANES)
            p_scr[hh, r0:r0 + rows, cs] = jnp.exp(s[:, cs] - m_new).astype(BF16)
        m_scr[hh, r0:r0 + rows, :] = m_new

    def accumulate(hh, j, r0, diagonal):
        kw = keys_seen(r0) if diagonal else tq
        vj = vx_scr[hh, pl.ds(pl.multiple_of(j * tq, tq), kw), :]
        alpha = al_scr[hh, r0:r0 + rows, :]
        acc_scr[hh, r0:r0 + rows, :] = (jnp.concatenate([alpha, alpha], axis=1) * acc_scr[hh, r0:r0 + rows, :]
                                        + _dot(p_scr[hh, r0:r0 + rows, 0:kw], vj))

    def stages(acc_j=None, score_j=None, masked=False, diagonal=False):
        for r0 in range(0, 2 * tq, rows):
            for hh in range(heads):
                if acc_j is not None:
                    accumulate(hh, acc_j, r0, diagonal)
                if score_j is not None:
                    scores(hh, score_j, r0, masked)

    @pl.when(i == 0)
    def _():
        stages(score_j=0, masked=True)

    @pl.when(i > 0)
    def _():
        stages(score_j=0)

        def body(t, carry):
            stages(acc_j=t - 1, score_j=t)
            return carry

        lax.fori_loop(1, i, body, 0)
        stages(acc_j=i - 1, score_j=i, masked=True)

    stages(acc_j=i, diagonal=True)
    lam = _lam(lam_ref, lam_init)
    for hh in range(heads):
        acc = acc_scr[hh]
        o = acc[:, 0:HEAD_W] / acc[:, HEAD_W:2 * HEAD_W]
        o = o[0:tq] - lam * o[tq:2 * tq]
        o_ref[0, :, cols(hh)] = (_rms(o, gs_ref[...]) * (1.0 - lam_init)).astype(BF16)


def _da_prompt(lam_par, g_subln, q, k, v, tq, lam_init):
    B, T, _ = q.shape
    hg = 4
    kv_spec = pl.BlockSpec((1, T, hg * HEAD_W), lambda b, h, i: (b, 0, h))
    q_spec = pl.BlockSpec((1, tq, hg * HEAD_W), lambda b, h, i: (b, i, h))
    full = lambda a: pl.BlockSpec(a.shape, lambda b, h, i: (0,) * a.ndim)
    return pl.pallas_call(
        functools.partial(_da_prompt_kernel, tq=tq, rows=min(2 * tq, 128), heads=hg, lam_init=lam_init),
        grid=(B, HEADS // hg, T // tq),
        in_specs=[full(lam_par), full(g_subln), q_spec, kv_spec, kv_spec],
        out_specs=q_spec,
        out_shape=jax.ShapeDtypeStruct((B, T, MIX_HALF), BF16),
        scratch_shapes=[pltpu.VMEM((hg, T, 2 * HEAD_W), BF16), pltpu.VMEM((hg, 2 * tq, LANES), F32),
                        pltpu.VMEM((hg, 2 * tq, 2 * HEAD_W), F32), pltpu.VMEM((hg, 2 * tq, tq), BF16),
                        pltpu.VMEM((hg, 2 * tq, LANES), F32)],
        compiler_params=_params(("parallel", "parallel", "arbitrary")),
        name="diffattn_prompt",
    )(lam_par, g_subln, q, k, v)


def _da_sample_kernel(lam_ref, gs_ref, q_ref, pk_ref, pv_ref, k_ref, v_ref, o_ref, *, tq, past, lam_init):
    def visible(shape, k_off):
        r = lax.broadcasted_iota(jnp.int32, shape, 0)
        c = lax.broadcasted_iota(jnp.int32, shape, 1) + k_off
        r = jnp.where(r >= tq, r - tq, r) + past
        return (r // CHUNK) >= (c // CHUNK)

    for h in range(HEADS):
        sl = slice(h * HEAD_W, (h + 1) * HEAD_W)
        head_rows = pl.ds(h, past, stride=HEADS)
        qz = _stack_q(q_ref[0, :, sl])
        s_old = _dot_nt(qz, pk_ref[0, head_rows, :].astype(BF16))
        s_new = _dot_nt(qz, k_ref[0, :, sl])
        s_old = jnp.where(visible(s_old.shape, 0), s_old, NEG_INF)
        s_new = jnp.where(visible(s_new.shape, past), s_new, NEG_INF)
        m = jnp.maximum(jnp.max(s_old, axis=-1, keepdims=True), jnp.max(s_new, axis=-1, keepdims=True))
        p_old = jnp.exp(s_old - m)
        p_new = jnp.exp(s_new - m)
        l = jnp.sum(p_old, axis=-1, keepdims=True) + jnp.sum(p_new, axis=-1, keepdims=True)
        acc = (_dot(p_old.astype(BF16), pv_ref[0, head_rows, :].astype(BF16))
               + _dot(p_new.astype(BF16), v_ref[0, :, sl]))
        o_ref[0, :, sl] = _da_finish(acc, l, _lam(lam_ref, lam_init), gs_ref[...], lam_init, tq)


def _da_sample(lam_par, g_subln, q, past_k, past_v, k, v, lam_init):
    B, T, _ = q.shape
    P = past_k.shape[1] // HEADS
    new_spec = pl.BlockSpec((1, T, MIX_HALF), lambda b: (b, 0, 0))
    past_spec = pl.BlockSpec((1, P * HEADS, HEAD_W), lambda b: (b, 0, 0))
    full = lambda a: pl.BlockSpec(a.shape, lambda b: (0,) * a.ndim)
    return pl.pallas_call(
        functools.partial(_da_sample_kernel, tq=T, past=P, lam_init=lam_init),
        grid=(B,),
        in_specs=[full(lam_par), full(g_subln), new_spec, past_spec, past_spec, new_spec, new_spec],
        out_specs=new_spec,
        out_shape=jax.ShapeDtypeStruct((B, T, MIX_HALF), BF16),
        compiler_params=_params(("parallel",)),
        name="diffattn_sample",
    )(lam_par, g_subln, q, past_k, past_v, k, v)


def _gdn_prep_kernel(q_ref, k_ref, v_ref, gate_ref, u_ref, w_ref, qd_ref, kt_ref, ai_ref, eg_ref, *, C, G):
    HC = HEADS * C
    row = lax.broadcasted_iota(jnp.int32, (C, C), 0)
    col = lax.broadcasted_iota(jnp.int32, (C, C), 1)
    causal = row >= col
    strict = row > col
    arow = lax.broadcasted_iota(jnp.int32, (C, HC), 0)
    acol = lax.broadcasted_iota(jnp.int32, (C, HC), 1)
    eye_all = (arow == (acol % C)).astype(F32)
    brow = lax.broadcasted_iota(jnp.int32, (HC, HC), 0)
    bcolm = lax.broadcasted_iota(jnp.int32, (HC, HC), 1)
    same_head = (brow // C) == (bcolm // C)
    prow = lax.broadcasted_iota(jnp.int32, (LANES, C), 0)
    pcol = lax.broadcasted_iota(jnp.int32, (LANES, C), 1)
    tri = jnp.where((prow >= pcol) & (prow < C), 1.0, 0.0).astype(BF16)

    def cumsum_rows(x):
        hi = x.astype(BF16)
        mid = (x - hi.astype(F32)).astype(BF16)
        lo = (x - hi.astype(F32) - mid.astype(F32)).astype(BF16)
        return _dot(tri, hi) + (_dot(tri, mid) + _dot(tri, lo))

    def block_diag(m_all):
        return jnp.where(same_head, jnp.concatenate([m_all] * HEADS, axis=0), jnp.zeros((), m_all.dtype))

    gcs, gcts, ps, xs = [], [], [], []
    for g in range(G):
        rs = slice(g * C, (g + 1) * C)
        gates = gate_ref[0, rs, :]
        gc_pad = cumsum_rows(gates)
        gcts.append(gc_pad.T)
        gcs.append(gc_pad[0:C, :])
        eg_ref[0, g * SUBLANES:(g + 1) * SUBLANES, :] = jnp.broadcast_to(
            jnp.exp(gc_pad[C - 1:C, :]), (SUBLANES, LANES))

    def decay_of(g, h):
        return jnp.exp(jnp.where(causal, gcs[g][:, h:h + 1] - gcts[g][h:h + 1, 0:C], NEG_INF))

    for g in range(G):
        rs = slice(g * C, (g + 1) * C)
        a_heads = []
        for h in range(HEADS):
            k = k_ref[0, rs, h * HEAD_W:(h + 1) * HEAD_W]
            kb = k * gate_ref[0, rs, HEADS + h:HEADS + h + 1]
            q = q_ref[0, rs, h * HEAD_W:(h + 1) * HEAD_W]
            both = _dot_nt(jnp.concatenate([kb.astype(BF16), q.astype(BF16)], axis=0), k.astype(BF16))
            decay = decay_of(g, h)
            a_heads.append(jnp.where(strict, -(both[0:C] * decay), 0.0))
            ai_ref[0, rs, h * C:(h + 1) * C] = (both[C:2 * C] * decay).astype(BF16)
        ps.append(jnp.concatenate(a_heads, axis=1))
        xs.append(eye_all + ps[g])
    def times(lefts, right_diag):
        bh, bl = right_diag
        top = _dot(jnp.concatenate([part for split in lefts for part in split], axis=0), bh)
        bot = _dot(jnp.concatenate([split[0] for split in lefts], axis=0), bl)
        return [top[2 * n * C:(2 * n + 1) * C] + top[(2 * n + 1) * C:(2 * n + 2) * C] + bot[n * C:(n + 1) * C]
                for n in range(len(lefts))]

    n_factors = int(math.log2(C))
    p_split = [_split_bf16(ps[g]) for g in range(G)]
    for g in range(G):
        p_split[g] = _split_bf16(times([p_split[g]], tuple(block_diag(part) for part in p_split[g]))[0])
    for k in range(1, n_factors):
        for g in range(G):
            diag = tuple(block_diag(part) for part in p_split[g])
            if k < n_factors - 1:
                x_prod, p_next = times([_split_bf16(xs[g]), p_split[g]], diag)
                p_split[g] = _split_bf16(p_next)
            else:
                x_prod, = times([_split_bf16(xs[g])], diag)
            xs[g] = xs[g] + x_prod

    for g in range(G):
        rs = slice(g * C, (g + 1) * C)
        for h in range(HEADS):
            sl = slice(h * HEAD_W, (h + 1) * HEAD_W)
            q = q_ref[0, rs, sl]
            v = v_ref[0, rs, sl]
            k = k_ref[0, rs, sl]
            gcol = gcs[g][:, h:h + 1]
            bcol = gate_ref[0, rs, HEADS + h:HEADS + h + 1]
            glast = gcs[g][C - 1:C, h:h + 1]
            egc = jnp.exp(gcol)
            t16 = xs[g][:, h * C:(h + 1) * C].astype(BF16)
            uw = _dot(t16, jnp.concatenate([v * bcol, k * bcol * egc], axis=1).astype(BF16))
            u_ref[0, rs, sl] = uw[:, 0:HEAD_W]
            w_ref[0, rs, sl] = uw[:, HEAD_W:2 * HEAD_W].astype(BF16)
            qd_ref[0, rs, sl] = (q * egc).astype(BF16)
            kt_ref[0, rs, sl] = (k * jnp.exp(glast - gcol)).astype(BF16)


def _gdn_scan_kernel(u_ref, w_ref, qd_ref, kt_ref, ai_ref, eg_ref, gz_ref, s0_ref, gout_ref, o_ref, sfin_ref, s_scr,
                     *, C, GB, n_chunks):
    c = pl.program_id(1)

    @pl.when(c == 0)
    def _():
        s_scr[...] = s0_ref[...]

    chains = [(b, h) for b in range(GB) for h in range(HEADS)]
    cols = lambda h: slice(h * HEAD_W, (h + 1) * HEAD_W)
    s16 = {bh: s_scr[bh[0], bh[1]].astype(BF16) for bh in chains}
    v_new = {(b, h): (u_ref[b, :, cols(h)] - _dot(w_ref[b, :, cols(h)], s16[b, h])).astype(BF16) for b, h in chains}
    o = {(b, h): _dot(qd_ref[b, :, cols(h)], s16[b, h]) + _dot(ai_ref[b, :, h * C:(h + 1) * C], v_new[b, h])
         for b, h in chains}
    for b, h in chains:
        s_scr[b, h] = s_scr[b, h] * eg_ref[b, 0:1, h:h + 1] + lax.dot_general(
            kt_ref[b, :, cols(h)], v_new[b, h], (((0,), (0,)), ((), ())), preferred_element_type=F32)
    for b, h in chains:
        o_ref[b, :, cols(h)] = (_rms(o[b, h], gout_ref[...]) * _silu(gz_ref[b, :, cols(h)])).astype(BF16)

    @pl.when(c == n_chunks - 1)
    def _():
        sfin_ref[...] = s_scr[...]


def _gdn(q, k, v, gz, gates, s0, g_out, C):
    B, T, _ = q.shape
    nc = T // C
    G = min(nc, 8)
    GB = next(n for n in (8, 4, 2, 1) if B % n == 0)
    HC = HEADS * C
    pblk = lambda w: pl.BlockSpec((1, G * C, w), lambda b, c: (b, c, 0))
    row = lambda w, dt: jax.ShapeDtypeStruct((B, T, w), dt)
    u, w, qd, kt, ai, eg = pl.pallas_call(
        functools.partial(_gdn_prep_kernel, C=C, G=G),
        grid=(B, nc // G),
        in_specs=[pblk(MIX_HALF)] * 3 + [pblk(LANES)],
        out_specs=[pblk(MIX_HALF)] * 4 + [pblk(HC), pl.BlockSpec((1, G * SUBLANES, LANES), lambda b, c: (b, c, 0))],
        out_shape=[row(MIX_HALF, F32), row(MIX_HALF, BF16), row(MIX_HALF, BF16), row(MIX_HALF, BF16), row(HC, BF16),
                   jax.ShapeDtypeStruct((B, nc * SUBLANES, LANES), F32)],
        compiler_params=_params(("parallel", "parallel")),
        name="gdn_prep",
    )(q, k, v, gates)
    sblk = lambda w: pl.BlockSpec((GB, C, w), lambda b, c: (b, c, 0))
    st = pl.BlockSpec((GB, HEADS, HEAD_W, HEAD_W), lambda b, c: (b, 0, 0, 0))
    return pl.pallas_call(
        functools.partial(_gdn_scan_kernel, C=C, GB=GB, n_chunks=nc),
        grid=(B // GB, nc),
        in_specs=[sblk(MIX_HALF)] * 4 + [sblk(HC), pl.BlockSpec((GB, SUBLANES, LANES), lambda b, c: (b, c, 0)),
                                         sblk(MIX_HALF), st, pl.BlockSpec(g_out.shape, lambda b, c: (0, 0))],
        out_specs=[sblk(MIX_HALF), st],
        out_shape=[jax.ShapeDtypeStruct((B, T, MIX_HALF), BF16),
                   jax.ShapeDtypeStruct((B, HEADS, HEAD_W, HEAD_W), F32)],
        scratch_shapes=[pltpu.VMEM((GB, HEADS, HEAD_W, HEAD_W), F32)],
        compiler_params=_params(("parallel", "arbitrary")),
        name="gdn_scan",
    )(u, w, qd, kt, ai, eg, gz, s0, g_out)


def _memkv_kernel(m_ref, g_ref, wk_ref, wv_ref, kf_ref, kb_ref, vf_ref, vb_ref):
    m = _rms(m_ref[...], g_ref[...]).astype(BF16)
    mk = _dot(m, wk_ref[...])
    mv = _dot(m, wv_ref[...])
    kf_ref[...] = mk
    kb_ref[...] = mk.astype(BF16)
    vf_ref[...] = mv
    vb_ref[...] = mv.astype(BF16)


def _memkv(mem, g_mem, w_ck, w_cv, tm):
    N, D = mem.shape
    blk = pl.BlockSpec((tm, D), lambda i: (i, 0))
    full = lambda a: pl.BlockSpec(a.shape, lambda i: (0,) * a.ndim)
    return pl.pallas_call(
        _memkv_kernel,
        grid=(N // tm,),
        in_specs=[blk, full(g_mem), full(w_ck), full(w_cv)],
        out_specs=[blk] * 4,
        out_shape=[jax.ShapeDtypeStruct((N, D), F32), jax.ShapeDtypeStruct((N, D), BF16)] * 2,
        compiler_params=_params(("parallel",)),
        name="memkv",
    )(mem, g_mem, w_ck, w_cv)


def _mid_kernel(x_ref, da_ref, gd_ref, wo_ref, gc_ref, wq_ref, mk_ref, mv_ref, wco_ref, gf_ref, wr_ref, br_ref,
                x2_ref, hf_ref, comb_ref):
    x1 = x_ref[0] + _dot(da_ref[0], wo_ref[0:MIX_HALF, :]) + _dot(gd_ref[0], wo_ref[MIX_HALF:2 * MIX_HALF, :])
    hc = _rms(x1, gc_ref[...]).astype(BF16)
    q = (_dot(hc, wq_ref[...]) * (X_DH ** -0.5)).astype(BF16)
    heads = []
    for h in range(X_HEADS):
        sl = slice(h * X_DH, (h + 1) * X_DH)
        s = _dot_nt(q[:, sl], mk_ref[0, :, sl])
        p = jnp.exp(s - jnp.max(s, axis=-1, keepdims=True))
        l = jnp.sum(p, axis=-1, keepdims=True)
        heads.append((_dot(p.astype(BF16), mv_ref[0, :, sl]) / l).astype(BF16))
    x2 = x1 + _dot(jnp.concatenate(heads, axis=1), wco_ref[...])
    x2_ref[0] = x2
    hf = _rms(x2, gf_ref[...])
    hf_ref[0] = hf.astype(BF16)

    logits = _dot3(hf, wr_ref[...]) + br_ref[...]
    lane_i = lax.broadcasted_iota(jnp.int32, logits.shape, 1)
    lane = lane_i.astype(F32)
    big = float(LANES)
    is_grp = (lane_i >= N_EXPERTS) & (lane_i < N_EXPERTS + N_GROUPS)
    gl = jnp.where(is_grp, logits, NEG_INF)
    gmax = jnp.max(gl, axis=-1, keepdims=True)
    g_top = 1.0 / jnp.sum(jnp.exp(gl - gmax), axis=-1, keepdims=True)
    g_idx = jnp.min(jnp.where(gl == gmax, lane, big), axis=-1, keepdims=True) - float(N_EXPERTS)
    in_grp = (lane_i < N_EXPERTS) & (jnp.floor(lane * (1.0 / E_PER_GROUP)) == g_idx)
    el = jnp.where(in_grp, logits, NEG_INF)
    e1 = jnp.max(el, axis=-1, keepdims=True)
    i1 = jnp.min(jnp.where(el == e1, lane, big), axis=-1, keepdims=True)
    el2 = jnp.where(lane == i1, NEG_INF, el)
    e2 = jnp.max(el2, axis=-1, keepdims=True)
    i2 = jnp.min(jnp.where(el2 == e2, lane, big), axis=-1, keepdims=True)
    r = jnp.exp(e2 - e1)
    w1 = g_top / (1.0 + r)
    w2 = g_top * r / (1.0 + r)
    comb_ref[0] = jnp.where(lane_i == 0, i1, jnp.where(lane_i == 1, i2, jnp.where(
        lane_i == 2, w1, jnp.where(lane_i == 3, w2, 0.0))))


def _mid(x, mix_da, mix_gd, w_out, g_cross, w_cq, mk, mv, w_co, g_ffn, w_router, b_router, tm):
    B, T, D = x.shape
    M = mk.shape[1]
    blk = lambda w: pl.BlockSpec((1, tm, w), lambda b, t: (b, t, 0))
    mem = pl.BlockSpec((1, M, D), lambda b, t: (b, 0, 0))
    full = lambda a: pl.BlockSpec(a.shape, lambda b, t: (0,) * a.ndim)
    return pl.pallas_call(
        _mid_kernel,
        grid=(B, T // tm),
        in_specs=[blk(D), blk(MIX_HALF), blk(MIX_HALF), full(w_out), full(g_cross), full(w_cq), mem, mem,
                  full(w_co), full(g_ffn), full(w_router), full(b_router)],
        out_specs=[blk(D), blk(D), blk(LANES)],
        out_shape=[jax.ShapeDtypeStruct((B, T, D), F32), jax.ShapeDtypeStruct((B, T, D), BF16),
                   jax.ShapeDtypeStruct((B, T, LANES), F32)],
        compiler_params=_params(("parallel", "parallel")),
        name="mid",
    )(x, mix_da, mix_gd, w_out, g_cross, w_cq, mk, mv, w_co, g_ffn, w_router, b_router)


MOE_BLK = 16
EXP_TILE_BLKS = 16


def _slots_per_tile(tm):
    worst = 2 * tm + N_EXPERTS * (MOE_BLK - 1)
    return -(-worst // LANES) * LANES


def _route_kernel(hf_ref, rec_ref, xs_ref, slot_ref, meta_ref, *, tm, sl, n_tiles):
    @pl.when(pl.program_id(0) == n_tiles)
    def _():
        xs_ref[...] = jnp.zeros(xs_ref.shape, BF16)
        slot_ref[...] = jnp.zeros(slot_ref.shape, F32)
        meta_ref[...] = jnp.zeros(meta_ref.shape, F32)

    @pl.when(pl.program_id(0) < n_tiles)
    def _():
        _route_tile(hf_ref, rec_ref, xs_ref, slot_ref, meta_ref, tm, sl)


def _route_tile(hf_ref, rec_ref, xs_ref, slot_ref, meta_ref, tm, sl):
    rec = rec_ref[...]
    lane_i = lax.broadcasted_iota(jnp.int32, rec.shape, 1)
    lane = lane_i.astype(F32)
    i1, i2, w1, w2 = rec[:, 0:1], rec[:, 1:2], rec[:, 2:3], rec[:, 3:4]
    oh1 = jnp.where(lane == i1, 1.0, 0.0)
    oh2 = jnp.where(lane == i2, 1.0, 0.0)
    oh = oh1 + oh2
    nblk = jnp.floor((jnp.sum(oh, axis=0, keepdims=True) + (MOE_BLK - 1.0)) * (1.0 / MOE_BLK))
    ur = lax.broadcasted_iota(jnp.int32, (LANES, LANES), 0)
    uc = lax.broadcasted_iota(jnp.int32, (LANES, LANES), 1)
    start = MOE_BLK * _dot(jnp.broadcast_to(nblk, (2 * SUBLANES, LANES)).astype(BF16),
                           jnp.where(ur < uc, 1.0, 0.0).astype(BF16))[0:1]
    r = lax.broadcasted_iota(jnp.int32, (tm, tm), 0)
    c = lax.broadcasted_iota(jnp.int32, (tm, tm), 1)
    pos = start + _dot(jnp.where(r > c, 1.0, 0.0).astype(BF16), oh.astype(BF16))
    slot1 = jnp.sum(oh1 * pos, axis=-1, keepdims=True)
    slot2 = jnp.sum(oh2 * pos, axis=-1, keepdims=True)
    slot_ref[...] = jnp.where(lane_i == 0, slot1, jnp.where(lane_i == 1, slot2, 0.0))
    meta_ref[0] = jnp.broadcast_to(nblk, (SUBLANES, LANES))

    srow = lax.broadcasted_iota(jnp.int32, (sl, tm), 0).astype(F32)
    perm = (jnp.where(srow == jnp.broadcast_to(slot1, (tm, LANES)).T[0:1, :], 1.0, 0.0)
            + jnp.where(srow == jnp.broadcast_to(slot2, (tm, LANES)).T[0:1, :], 2.0, 0.0)).astype(BF16)

    def parts(w, base):
        hi = w.astype(BF16).astype(F32)
        mid = (w - hi).astype(BF16).astype(F32)
        lo = w - hi - mid
        return jnp.where(lane_i == base, hi, jnp.where(lane_i == base + 1, mid,
                                                       jnp.where(lane_i == base + 2, lo, 0.0)))

    extra = parts(w1, 0) + parts(w2, 3) + jnp.where(lane_i == 6, 1.0, 0.0)
    moved = _dot(perm, jnp.concatenate([hf_ref[...], extra.astype(BF16)], axis=1))
    d = hf_ref.shape[1]
    second = moved[:, d + 6:d + 7] == 2.0
    moved = moved * jnp.where(second, 0.5, 1.0)
    lane_s = lax.broadcasted_iota(jnp.int32, (sl, LANES), 1)
    xs_ref[:, 0:d] = moved[:, 0:d].astype(BF16)
    xs_ref[:, d:d + LANES] = jnp.where(lane_s == 7, jnp.where(second, 1.0, 0.0), moved[:, d:d + LANES]).astype(BF16)


def _route(hf, rec, tm):
    N, D = hf.shape
    nt = N // tm
    sl = _slots_per_tile(tm)
    in_blk = lambda w: pl.BlockSpec((tm, w), lambda i: (jnp.minimum(i, nt - 1), 0))
    blk = lambda rows, w: pl.BlockSpec((rows, w), lambda i: (i, 0))
    return pl.pallas_call(
        functools.partial(_route_kernel, tm=tm, sl=sl, n_tiles=nt),
        grid=(nt + 1,),
        in_specs=[in_blk(D), in_blk(LANES)],
        out_specs=[blk(sl, D + LANES), blk(tm, LANES), pl.BlockSpec((1, SUBLANES, LANES), lambda i: (i, 0, 0))],
        out_shape=[jax.ShapeDtypeStruct(((nt + 1) * sl, D + LANES), BF16),
                   jax.ShapeDtypeStruct(((nt + 1) * tm, LANES), F32),
                   jax.ShapeDtypeStruct((nt + 1, SUBLANES, LANES), F32)],
        compiler_params=_params(("parallel",)),
        name="moe_route",
    )(hf, rec)


def _moe_tables(meta, blocks_per_tile, max_tiles):
    nblk = meta[:, 0, :N_EXPERTS].astype(jnp.int32)
    nt = nblk.shape[0]
    tiles_e = (jnp.sum(nblk, axis=0) + EXP_TILE_BLKS - 1) // EXP_TILE_BLKS
    tile_end = jnp.cumsum(tiles_e)
    first_blk = (tile_end - tiles_e) * EXP_TILE_BLKS
    off = first_blk[None, :] + jnp.cumsum(nblk, axis=0) - nblk
    lend = jnp.cumsum(nblk, axis=1)
    lstart = lend - nblk
    lb = jnp.arange(blocks_per_tile, dtype=jnp.int32)[None, :]
    in_run = (lstart[:, None, :] <= lb[:, :, None]) & (lb[:, :, None] < lend[:, None, :])
    valid = lb < lend[:, -1:]
    dest = jnp.sum(jnp.where(in_run, (off - lstart)[:, None, :], 0), axis=-1) + lb
    flat = jnp.arange(nt, dtype=jnp.int32)[:, None] * blocks_per_tile + lb
    n_sorted = max_tiles * EXP_TILE_BLKS
    src = jnp.full((n_sorted,), -1, jnp.int32).at[jnp.where(valid, dest, n_sorted).reshape(-1)].set(
        flat.reshape(-1), mode="drop")
    g = jnp.arange(max_tiles, dtype=jnp.int32)
    tile_expert = jnp.minimum(jnp.sum(tile_end[None, :] <= g[:, None], axis=-1), N_EXPERTS - 1).astype(jnp.int32)
    return src, tile_expert, tile_end[-1:].astype(jnp.int32)


def _experts_kernel(src_ref, te_ref, nt_ref, xs_hbm, wg_ref, wu_ref, wd_ref, ys_hbm, xbuf, obuf, gsem, ssem, *,
                    zero_blk, spare_blk):
    g = pl.program_id(0)
    n_tiles = nt_ref[0]
    nb = EXP_TILE_BLKS
    d = obuf.shape[-1]

    def gather_copy(tile, slot, i):
        s = src_ref[tile * nb + i]
        return pltpu.make_async_copy(xs_hbm.at[jnp.where(s >= 0, s, zero_blk)], xbuf.at[slot, i], gsem.at[slot])

    def scatter_copy(tile, slot, i):
        s = src_ref[tile * nb + i]
        dst = jnp.where(s >= 0, s, spare_blk + slot * nb + i)
        return pltpu.make_async_copy(obuf.at[slot, i], ys_hbm.at[dst, :, pl.ds(0, d)], ssem.at[slot])

    def gather_start(tile, slot):
        for i in range(nb):
            gather_copy(tile, slot, i).start()

    def gather_wait(tile, slot):
        for i in range(nb):
            gather_copy(tile, slot, i).wait()

    def scatter_start(tile, slot):
        for i in range(nb):
            scatter_copy(tile, slot, i).start()

    def scatter_wait(tile, slot):
        for i in range(nb):
            scatter_copy(tile, slot, i).wait()

    slot = g % 2
    xslot = g % 3

    @pl.when(g == 0)
    def _():
        gather_start(0, 0)

        @pl.when(n_tiles > 1)
        def _():
            gather_start(1, 1)

    @pl.when(g < n_tiles)
    def _():
        @pl.when(g + 2 < n_tiles)
        def _():
            gather_start(g + 2, (g + 2) % 3)

        gather_wait(g, xslot)
        rows = nb * MOE_BLK
        xe = xbuf[xslot].reshape(rows, xbuf.shape[-1])
        x = xe[:, 0:d]
        extra = xe[:, d:d + LANES].astype(F32)
        lane = lax.broadcasted_iota(jnp.int32, extra.shape, 1)
        first_lane = 3.0 * jnp.sum(jnp.where(lane == 7, extra, 0.0), axis=-1, keepdims=True)
        lane_f = lane.astype(F32)
        mine = (lane_f >= first_lane) & (lane_f < first_lane + 3.0)
        ccol = jnp.sum(jnp.where(mine, extra, 0.0), axis=-1, keepdims=True)
        hg = _silu(_dot(x, wg_ref[0])) * _dot(x, wu_ref[0])
        y = _dot((hg * ccol).astype(BF16), wd_ref[0]).astype(BF16)

        @pl.when(g >= 2)
        def _():
            scatter_wait(g - 2, slot)

        obuf[slot] = y.reshape(nb, MOE_BLK, d)
        scatter_start(g, slot)

    @pl.when(g == n_tiles - 1)
    def _():
        scatter_wait(g, slot)

        @pl.when(g >= 1)
        def _():
            scatter_wait(g - 1, 1 - slot)


def _experts(src, tile_expert, n_tiles, xs, w_gate, w_up, w_down, max_tiles, zero_blk):
    n_rows, width = xs.shape
    D = width - LANES
    slab = xs.reshape(n_rows // MOE_BLK, MOE_BLK, width)
    wspec = lambda shape: pl.BlockSpec((1,) + shape, lambda g, src, te, nt: (te[g], 0, 0))
    grid_spec = pltpu.PrefetchScalarGridSpec(
        num_scalar_prefetch=3,
        grid=(max_tiles,),
        in_specs=[pl.BlockSpec(memory_space=pl.ANY), wspec((D, D_EXPERT)), wspec((D, D_EXPERT)), wspec((D_EXPERT, D))],
        out_specs=pl.BlockSpec(memory_space=pl.ANY),
        scratch_shapes=[pltpu.VMEM((3, EXP_TILE_BLKS, MOE_BLK, width), BF16),
                        pltpu.VMEM((2, EXP_TILE_BLKS, MOE_BLK, D), BF16),
                        pltpu.SemaphoreType.DMA((3,)), pltpu.SemaphoreType.DMA((2,))],
    )
    return pl.pallas_call(
        functools.partial(_experts_kernel, zero_blk=zero_blk, spare_blk=zero_blk + 1),
        grid_spec=grid_spec,
        out_shape=jax.ShapeDtypeStruct(slab.shape, BF16),
        input_output_aliases={3: 0},
        compiler_params=_params(("arbitrary",)),
        name="moe_experts",
    )(src, tile_expert, n_tiles, slab, w_gate, w_up, w_down).reshape(n_rows, width)


def _combine_kernel(x2_ref, slot_ref, ys_ref, gfin_ref, y_ref, *, tm, sl):
    slots = slot_ref[...]
    col = lax.broadcasted_iota(jnp.int32, (tm, sl), 1).astype(F32)
    pick = (jnp.where(col == slots[:, 0:1], 1.0, 0.0) + jnp.where(col == slots[:, 1:2], 1.0, 0.0)).astype(BF16)
    y_ref[...] = _rms(x2_ref[...] + _dot(pick, ys_ref[:, 0:x2_ref.shape[1]]), gfin_ref[...])


def _combine(x2, slots, ys, g_final, tm, sl):
    N, D = x2.shape
    blk = lambda rows, w: pl.BlockSpec((rows, w), lambda i: (i, 0))
    return pl.pallas_call(
        functools.partial(_combine_kernel, tm=tm, sl=sl),
        grid=(N // tm,),
        in_specs=[blk(tm, D), blk(tm, LANES), blk(sl, ys.shape[1]), pl.BlockSpec(g_final.shape, lambda i: (0, 0))],
        out_specs=blk(tm, D),
        out_shape=jax.ShapeDtypeStruct((N, D), F32),
        compiler_params=_params(("parallel",)),
        name="moe_combine",
    )(x2, slots, ys, g_final)


def _moe(x2, hf, rec, w_gate, w_up, w_down, g_final, tm):
    N, _ = x2.shape
    nt = N // tm
    sl = _slots_per_tile(tm)
    xs, slots, meta = _route(hf, rec, tm)
    blocks_per_tile = sl // MOE_BLK
    max_tiles = nt * blocks_per_tile // EXP_TILE_BLKS + N_EXPERTS
    src, tile_expert, n_tiles = _moe_tables(meta[:nt], blocks_per_tile, max_tiles)
    ys = _experts(src, tile_expert, n_tiles, xs, w_gate, w_up, w_down, max_tiles, nt * blocks_per_tile)
    return _combine(x2, slots, ys, g_final, tm, sl)


def _pad_rows(a, rows):
    return jnp.pad(a, ((0, rows - a.shape[0]), (0, LANES - a.shape[1])))


def _trunk(x, mk16, mv16, past_k, past_v, conv_prev, s_prev, lam_init, wts):
    B, T, D = x.shape
    tm = min(T, 512)
    if conv_prev is None:
        conv_prev = jnp.zeros((B, CONV_W - 1, GDN_QKV), F32)
    if s_prev is None:
        s_prev = jnp.zeros((B, HEADS, HEAD_W, HEAD_W), F32)
    prev8 = jnp.pad(conv_prev, ((0, 0), (SUBLANES - (CONV_W - 1), 0), (0, 0)))
    (dq, dkf, dk16, dvf, dv16, cq, ck, cv, gz, gates, cnew) = _inproj(
        x, prev8, wts["g_mix"], wts["w_main"], wts["w_gate8"], wts["conv_w8"], wts["gate_par"], tm)
    if past_k is None:
        mix_da = _da_prompt(wts["lam_par"], wts["g_subln"], dq, dk16, dv16, min(T, 512), lam_init)
    else:
        P = past_k.shape[1]
        mix_da = _da_sample(wts["lam_par"], wts["g_subln"], dq, past_k.reshape(B, P * HEADS, HEAD_W),
                            past_v.reshape(B, P * HEADS, HEAD_W), dk16, dv16, lam_init)
    C = min(T, CHUNK)
    mix_gd, s_new = _gdn(cq, ck, cv, gz, gates, s_prev, wts["g_gdn_out"], C)
    x2, hf, comb = _mid(x, mix_da, mix_gd, wts["w_out"], wts["g_cross"], wts["w_cq"], mk16, mv16, wts["w_co"],
                        wts["g_ffn"], wts["w_router"], wts["b_router"], min(T, 512))
    N = B * T
    y = _moe(x2.reshape(N, D), hf.reshape(N, D), comb.reshape(N, LANES), wts["w_gate"], wts["w_up"],
             wts["w_down"], wts["g_final"], min(N, 512))
    new_k = dkf.reshape(1, B, T, HEADS, HEAD_W)
    new_v = dvf.reshape(1, B, T, HEADS, HEAD_W)
    new_conv = cnew[:, SUBLANES - (CONV_W - 1):, :].reshape(1, B, CONV_W - 1, GDN_QKV)
    return y.reshape(B, T, D), new_k, new_v, new_conv, s_new[None]


def kernel(x_prompt, x_sample, cache_diff_k, cache_diff_v, state_gdn_conv, state_gdn, cache_mem_k, cache_mem_v,
           mem_prompt, g_mix, w_in, lam_q1, lam_k1, lam_q2, lam_k2, g_subln, conv_w, a_log, dt_bias, g_gdn_out,
           w_out, g_cross, g_mem, w_ck, w_cv, w_cq, w_co, g_ffn, w_grp, b_grp, w_rt, b_rt, w_gate, w_up, w_down,
           g_final):
    assert g_mix.shape[0] == 1, "single-layer model"
    l = 0
    lam_init = 0.8 - 0.6 * math.exp(-0.3 * l)
    gate_lo = 3 * MIX_HALF + GDN_QKV
    gate_hi = gate_lo + 2 * HEADS
    wi = w_in[l]
    row = lambda a: a.reshape(1, -1).astype(F32)
    wts = dict(
        g_mix=row(g_mix[l]),
        w_main=jnp.concatenate([wi[:, :gate_lo], wi[:, gate_hi:]], axis=1).astype(BF16),
        w_gate8=jnp.pad(wi[:, gate_lo:gate_hi], ((0, 0), (0, LANES - 2 * HEADS))).astype(BF16),
        conv_w8=jnp.pad(conv_w[l], ((0, SUBLANES - CONV_W), (0, 0))),
        gate_par=_pad_rows(jnp.stack([a_log[l], dt_bias[l]]), SUBLANES),
        lam_par=_pad_rows(jnp.stack([lam_q1[l], lam_k1[l], lam_q2[l], lam_k2[l]]), SUBLANES),
        g_subln=row(g_subln[l]), g_gdn_out=row(g_gdn_out[l]),
        w_out=w_out[l].astype(BF16), g_cross=row(g_cross[l]), w_cq=w_cq[l].astype(BF16), w_co=w_co[l].astype(BF16),
        g_ffn=row(g_ffn[l]),
        w_router=jnp.pad(jnp.concatenate([w_rt[l], w_grp[l]], axis=1),
                         ((0, 0), (0, LANES - N_EXPERTS - N_GROUPS))),
        b_router=jnp.pad(jnp.concatenate([b_rt[l], b_grp[l]]), (0, LANES - N_EXPERTS - N_GROUPS)).reshape(1, LANES),
        w_gate=w_gate[l].astype(BF16), w_up=w_up[l].astype(BF16), w_down=w_down[l].astype(BF16),
        g_final=row(g_final),
    )
    B, M, D = mem_prompt.shape
    mkf, mk16, mvf, mv16 = _memkv(mem_prompt.reshape(B * M, D), row(g_mem[l]), w_ck[l].astype(BF16),
                                  w_cv[l].astype(BF16), min(B * M, 512))
    yp, pk, pv, pc, ps = _trunk(x_prompt, mk16.reshape(B, M, D), mv16.reshape(B, M, D), None, None, None, None,
                                lam_init, wts)
    Bs = x_sample.shape[0]
    ys, sk, sv, sc, ss = _trunk(x_sample, cache_mem_k[l].reshape(Bs, M, D).astype(BF16),
                                cache_mem_v[l].reshape(Bs, M, D).astype(BF16), cache_diff_k[l], cache_diff_v[l],
                                state_gdn_conv[l], state_gdn[l], lam_init, wts)
    mem_shape = (1, B, M, X_HEADS, X_DH)
    return (yp, ys, pk, pv, pc, ps, mkf.reshape(mem_shape), mvf.reshape(mem_shape), sk, sv, sc, ss)
```

```python
import functools
import math

import jax
import jax.numpy as jnp
import numpy as np
from jax import lax
from jax.experimental import pallas as pl
from jax.experimental.pallas import tpu as pltpu

F32 = jnp.float32
BF16 = jnp.bfloat16

D_MODEL = 1024
CHUNK = 64
HEADS = 4
HEAD_W = 128
DA_DH = 64
MIX_HALF = HEADS * HEAD_W
GDN_QKV = 3 * MIX_HALF
CONV_W = 4
X_HEADS = 4
X_DH = 256
N_GROUPS = 4
E_PER_GROUP = 8
N_EXPERTS = 32
D_EXPERT = 256
NORM_EPS = 1e-6
NEG_INF = -1e30
LANES = 128
SUBLANES = 8
VMEM_LIMIT = 56 * 1024 * 1024


def _dot(a, b):
    return jnp.dot(a, b, preferred_element_type=F32)


def _dot_nt(a, b):
    return lax.dot_general(a, b, (((1,), (1,)), ((), ())), preferred_element_type=F32)


def _rms(x, g):
    return x * lax.rsqrt(jnp.mean(x * x, axis=-1, keepdims=True) + NORM_EPS) * g


def _sigmoid(x):
    return 1.0 / (1.0 + jnp.exp(-x))


def _silu(x):
    return x * _sigmoid(x)


def _split_bf16(a):
    hi = a.astype(BF16)
    lo = (a - hi.astype(F32)).astype(BF16)
    return hi, lo


def _dot3(a, b):
    ah, al = _split_bf16(a)
    bh, bl = _split_bf16(b)
    return _dot(ah, bh) + (_dot(ah, bl) + _dot(al, bh))


def _params(sem):
    return pltpu.CompilerParams(dimension_semantics=sem, vmem_limit_bytes=VMEM_LIMIT)


def _inproj_kernel(x_ref, prev_ref, gmix_ref, wm_ref, wg_ref, cw_ref, gp_ref,
                   dq_ref, dkf_ref, dkb_ref, dvf_ref, dvb_ref, cq_ref, ck_ref, cv_ref, gz_ref, gate_ref,
                   cnew_ref, xp_scr, *, tm):
    t = pl.program_id(1)
    h = _rms(x_ref[0], gmix_ref[...]).astype(BF16)

    @pl.when(t == 0)
    def _():
        xp_scr[0:SUBLANES, :] = prev_ref[0]

    conv_col0 = 3 * MIX_HALF
    outs = (cq_ref, ck_ref, cv_ref)

    def conv_part(part):
        cs = slice(part * MIX_HALF, (part + 1) * MIX_HALF)
        xp_scr[SUBLANES:SUBLANES + tm, cs] = _dot(
            h, wm_ref[:, conv_col0 + part * MIX_HALF:conv_col0 + (part + 1) * MIX_HALF])
        y = xp_scr[5:5 + tm, cs] * cw_ref[0:1, cs]
        for j in range(1, CONV_W):
            y = y + xp_scr[5 + j:5 + j + tm, cs] * cw_ref[j:j + 1, cs]
        c = _silu(y)
        if part == 2:
            outs[part][0] = c.astype(BF16)
        else:
            scale = (HEAD_W ** -0.5) if part == 0 else 1.0
            for hh in range(HEADS):
                ch = c[:, hh * HEAD_W:(hh + 1) * HEAD_W]
                n = ch * lax.rsqrt(jnp.sum(ch * ch, axis=-1, keepdims=True) + NORM_EPS)
                outs[part][0, :, hh * HEAD_W:(hh + 1) * HEAD_W] = (n * scale if part == 0 else n).astype(BF16)

    conv_part(0)
    dq = _dot(h, wm_ref[:, 0:MIX_HALF])
    dq_ref[0] = (dq * (DA_DH ** -0.5)).astype(BF16)
    conv_part(1)
    dk = _dot(h, wm_ref[:, MIX_HALF:2 * MIX_HALF])
    dkb_ref[0] = dk.astype(BF16)
    conv_part(2)
    dv = _dot(h, wm_ref[:, 2 * MIX_HALF:3 * MIX_HALF])
    dvb_ref[0] = dv.astype(BF16)
    for hh in range(HEADS):
        dkf_ref[0, pl.ds(hh, tm, stride=HEADS), :] = dk[:, hh * HEAD_W:(hh + 1) * HEAD_W]
        dvf_ref[0, pl.ds(hh, tm, stride=HEADS), :] = dv[:, hh * HEAD_W:(hh + 1) * HEAD_W]

    gates = _dot(h, wg_ref[...])
    z = gates + gp_ref[1:2, :]
    softplus = jnp.maximum(z, 0.0) + jnp.log1p(jnp.exp(-jnp.abs(z)))
    g_all = -jnp.exp(gp_ref[0:1, :]) * softplus
    lane = lax.broadcasted_iota(jnp.int32, gates.shape, 1)
    gate_ref[0] = jnp.where(lane < HEADS, g_all, jnp.where(lane < 2 * HEADS, _sigmoid(gates), 0.0))
    gz_ref[0] = _dot(h, wm_ref[:, 3 * MIX_HALF + GDN_QKV:4 * MIX_HALF + GDN_QKV]).astype(BF16)
    carry = xp_scr[tm:tm + SUBLANES, :]
    xp_scr[0:SUBLANES, :] = carry
    cnew_ref[0] = carry


def _inproj(x, prev8, g_mix, w_main, w_gate, conv_w8, gate_par, tm):
    B, T, D = x.shape
    nt = T // tm
    row = lambda w, dt: jax.ShapeDtypeStruct((B, T, w), dt)
    blk = lambda w: pl.BlockSpec((1, tm, w), lambda b, t: (b, t, 0))
    full = lambda a: pl.BlockSpec(a.shape, lambda b, t: (0,) * a.ndim)
    cache_blk = pl.BlockSpec((1, tm * HEADS, HEAD_W), lambda b, t: (b, t, 0))
    cache_rows = jax.ShapeDtypeStruct((B, T * HEADS, HEAD_W), F32)
    return pl.pallas_call(
        functools.partial(_inproj_kernel, tm=tm),
        grid=(B, nt),
        in_specs=[blk(D), pl.BlockSpec((1, SUBLANES, GDN_QKV), lambda b, t: (b, 0, 0)),
                  full(g_mix), full(w_main), full(w_gate), full(conv_w8), full(gate_par)],
        out_specs=[blk(MIX_HALF), cache_blk, blk(MIX_HALF), cache_blk] + [blk(MIX_HALF)] * 5
        + [blk(LANES), pl.BlockSpec((1, SUBLANES, GDN_QKV), lambda b, t: (b, 0, 0))],
        out_shape=[row(MIX_HALF, BF16), cache_rows, row(MIX_HALF, BF16), cache_rows,
                   row(MIX_HALF, BF16), row(MIX_HALF, BF16), row(MIX_HALF, BF16), row(MIX_HALF, BF16),
                   row(MIX_HALF, BF16), row(LANES, F32), jax.ShapeDtypeStruct((B, SUBLANES, GDN_QKV), F32)],
        scratch_shapes=[pltpu.VMEM((tm + SUBLANES, GDN_QKV), F32)],
        compiler_params=_params(("parallel", "arbitrary")),
        name="inproj",
    )(x, prev8, g_mix, w_main, w_gate, conv_w8, gate_par)


def _lam(lam_ref, lam_init):
    s1 = jnp.sum(lam_ref[0:1, :] * lam_ref[1:2, :], axis=-1, keepdims=True)
    s2 = jnp.sum(lam_ref[2:3, :] * lam_ref[3:4, :], axis=-1, keepdims=True)
    return jnp.exp(s1) - jnp.exp(s2) + lam_init


def _stack_q(q):
    lane = lax.broadcasted_iota(jnp.int32, q.shape, 1)
    zero = jnp.zeros_like(q)
    return jnp.concatenate([jnp.where(lane < DA_DH, q, zero), jnp.where(lane >= DA_DH, q, zero)], axis=0)


def _da_finish(acc, l, lam, gs, lam_init, tq):
    o = acc[:tq] / l[:tq] - lam * (acc[tq:] / l[tq:])
    return (_rms(o, gs) * (1.0 - lam_init)).astype(BF16)


def _da_prompt_kernel(lam_ref, gs_ref, q_ref, k_ref, v_ref, o_ref, vx_scr, m_scr, acc_scr, p_scr, al_scr, *,
                      tq, rows, heads, lam_init):
    i = pl.program_id(2)
    cols = lambda hh: slice(hh * HEAD_W, (hh + 1) * HEAD_W)

    @pl.when(i == 0)
    def _():
        for hh in range(heads):
            vx_scr[hh, :, 0:HEAD_W] = v_ref[0, :, cols(hh)]
            vx_scr[hh, :, HEAD_W:2 * HEAD_W] = jnp.ones((vx_scr.shape[1], HEAD_W), BF16)
    qz = [_stack_q(q_ref[0, :, cols(hh)]) for hh in range(heads)]
    m_scr[...] = jnp.full(m_scr.shape, NEG_INF, F32)
    acc_scr[...] = jnp.zeros(acc_scr.shape, F32)

    def keys_seen(r0):
        return min(tq, -(-((r0 % tq) + rows) // LANES) * LANES)

    def scores(hh, j, r0, masked):
        kw = keys_seen(r0) if masked else tq
        kj = k_ref[0, pl.ds(pl.multiple_of(j * tq, tq), kw), cols(hh)]
        s = _dot_nt(qz[hh][r0:r0 + rows], kj)
        if masked:
            r = lax.broadcasted_iota(jnp.int32, s.shape, 0) + (r0 % tq)
            c = lax.broadcasted_iota(jnp.int32, s.shape, 1)
            s = jnp.where((r // CHUNK) >= (c // CHUNK), s, NEG_INF)
        m_prev = m_scr[hh, r0:r0 + rows, :]
        m_new = jnp.maximum(m_prev, jnp.max(s, axis=-1, keepdims=True))
        al_scr[hh, r0:r0 + rows, :] = jnp.exp(m_prev - m_new)
        for c in range(kw // LANES):
            cs = slice(c * LANES, (c + 1) * LANES)
            p_scr[hh, r0:r0 + rows, cs] = jnp.exp(s[:, cs] - m_new).astype(BF16)
        m_scr[hh, r0:r0 + rows, :] = m_new

    def accumulate(hh, j, r0, diagonal):
        kw = keys_seen(r0) if diagonal else tq
        vj = vx_scr[hh, pl.ds(pl.multiple_of(j * tq, tq), kw), :]
        alpha = al_scr[hh, r0:r0 + rows, :]
        acc_scr[hh, r0:r0 + rows, :] = (jnp.concatenate([alpha, alpha], axis=1) * acc_scr[hh, r0:r0 + rows, :]
                                        + _dot(p_scr[hh, r0:r0 + rows, 0:kw], vj))

    def stages(acc_j=None, score_j=None, masked=False, diagonal=False):
        for r0 in range(0, 2 * tq, rows):
            for hh in range(heads):
                if acc_j is not None:
                    accumulate(hh, acc_j, r0, diagonal)
                if score_j is not None:
                    scores(hh, score_j, r0, masked)

    @pl.when(i == 0)
    def _():
        stages(score_j=0, masked=True)

    @pl.when(i > 0)
    def _():
        stages(score_j=0)

        def body(t, carry):
            stages(acc_j=t - 1, score_j=t)
            return carry

        lax.fori_loop(1, i, body, 0)
        stages(acc_j=i - 1, score_j=i, masked=True)

    stages(acc_j=i, diagonal=True)
    lam = _lam(lam_ref, lam_init)
    for hh in range(heads):
        acc = acc_scr[hh]
        o = acc[:, 0:HEAD_W] / acc[:, HEAD_W:2 * HEAD_W]
        o = o[0:tq] - lam * o[tq:2 * tq]
        o_ref[0, :, cols(hh)] = (_rms(o, gs_ref[...]) * (1.0 - lam_init)).astype(BF16)


def _da_prompt(lam_par, g_subln, q, k, v, tq, lam_init):
    B, T, _ = q.shape
    hg = 4
    kv_spec = pl.BlockSpec((1, T, hg * HEAD_W), lambda b, h, i: (b, 0, h))
    q_spec = pl.BlockSpec((1, tq, hg * HEAD_W), lambda b, h, i: (b, i, h))
    full = lambda a: pl.BlockSpec(a.shape, lambda b, h, i: (0,) * a.ndim)
    return pl.pallas_call(
        functools.partial(_da_prompt_kernel, tq=tq, rows=min(2 * tq, 128), heads=hg, lam_init=lam_init),
        grid=(B, HEADS // hg, T // tq),
        in_specs=[full(lam_par), full(g_subln), q_spec, kv_spec, kv_spec],
        out_specs=q_spec,
        out_shape=jax.ShapeDtypeStruct((B, T, MIX_HALF), BF16),
        scratch_shapes=[pltpu.VMEM((hg, T, 2 * HEAD_W), BF16), pltpu.VMEM((hg, 2 * tq, LANES), F32),
                        pltpu.VMEM((hg, 2 * tq, 2 * HEAD_W), F32), pltpu.VMEM((hg, 2 * tq, tq), BF16),
                        pltpu.VMEM((hg, 2 * tq, LANES), F32)],
        compiler_params=_params(("parallel", "parallel", "arbitrary")),
        name="diffattn_prompt",
    )(lam_par, g_subln, q, k, v)


def _da_sample_kernel(lam_ref, gs_ref, q_ref, pk_ref, pv_ref, k_ref, v_ref, o_ref, *, tq, past, lam_init):
    def visible(shape, k_off):
        r = lax.broadcasted_iota(jnp.int32, shape, 0)
        c = lax.broadcasted_iota(jnp.int32, shape, 1) + k_off
        r = jnp.where(r >= tq, r - tq, r) + past
        return (r // CHUNK) >= (c // CHUNK)

    for h in range(HEADS):
        sl = slice(h * HEAD_W, (h + 1) * HEAD_W)
        head_rows = pl.ds(h, past, stride=HEADS)
        qz = _stack_q(q_ref[0, :, sl])
        s_old = _dot_nt(qz, pk_ref[0, head_rows, :].astype(BF16))
        s_new = _dot_nt(qz, k_ref[0, :, sl])
        s_old = jnp.where(visible(s_old.shape, 0), s_old, NEG_INF)
        s_new = jnp.where(visible(s_new.shape, past), s_new, NEG_INF)
        m = jnp.maximum(jnp.max(s_old, axis=-1, keepdims=True), jnp.max(s_new, axis=-1, keepdims=True))
        p_old = jnp.exp(s_old - m)
        p_new = jnp.exp(s_new - m)
        l = jnp.sum(p_old, axis=-1, keepdims=True) + jnp.sum(p_new, axis=-1, keepdims=True)
        acc = (_dot(p_old.astype(BF16), pv_ref[0, head_rows, :].astype(BF16))
               + _dot(p_new.astype(BF16), v_ref[0, :, sl]))
        o_ref[0, :, sl] = _da_finish(acc, l, _lam(lam_ref, lam_init), gs_ref[...], lam_init, tq)


def _da_sample(lam_par, g_subln, q, past_k, past_v, k, v, lam_init):
    B, T, _ = q.shape
    P = past_k.shape[1] // HEADS
    new_spec = pl.BlockSpec((1, T, MIX_HALF), lambda b: (b, 0, 0))
    past_spec = pl.BlockSpec((1, P * HEADS, HEAD_W), lambda b: (b, 0, 0))
    full = lambda a: pl.BlockSpec(a.shape, lambda b: (0,) * a.ndim)
    return pl.pallas_call(
        functools.partial(_da_sample_kernel, tq=T, past=P, lam_init=lam_init),
        grid=(B,),
        in_specs=[full(lam_par), full(g_subln), new_spec, past_spec, past_spec, new_spec, new_spec],
        out_specs=new_spec,
        out_shape=jax.ShapeDtypeStruct((B, T, MIX_HALF), BF16),
        compiler_params=_params(("parallel",)),
        name="diffattn_sample",
    )(lam_par, g_subln, q, past_k, past_v, k, v)


def _gdn_prep_kernel(q_ref, k_ref, v_ref, gate_ref, u_ref, w_ref, qd_ref, kt_ref, ai_ref, eg_ref, *, C, G):
    HC = HEADS * C
    row = lax.broadcasted_iota(jnp.int32, (C, C), 0)
    col = lax.broadcasted_iota(jnp.int32, (C, C), 1)
    causal = row >= col
    strict = row > col
    arow = lax.broadcasted_iota(jnp.int32, (C, HC), 0)
    acol = lax.broadcasted_iota(jnp.int32, (C, HC), 1)
    eye_all = (arow == (acol % C)).astype(F32)
    brow = lax.broadcasted_iota(jnp.int32, (HC, HC), 0)
    bcolm = lax.broadcasted_iota(jnp.int32, (HC, HC), 1)
    same_head = (brow // C) == (bcolm // C)
    prow = lax.broadcasted_iota(jnp.int32, (LANES, C), 0)
    pcol = lax.broadcasted_iota(jnp.int32, (LANES, C), 1)
    tri = jnp.where((prow >= pcol) & (prow < C), 1.0, 0.0).astype(BF16)

    def cumsum_rows(x):
        hi = x.astype(BF16)
        mid = (x - hi.astype(F32)).astype(BF16)
        lo = (x - hi.astype(F32) - mid.astype(F32)).astype(BF16)
        return _dot(tri, hi) + (_dot(tri, mid) + _dot(tri, lo))

    def block_diag(m_all):
        return jnp.where(same_head, jnp.concatenate([m_all] * HEADS, axis=0), jnp.zeros((), m_all.dtype))

    gcs, gcts, ps, xs = [], [], [], []
    for g in range(G):
        rs = slice(g * C, (g + 1) * C)
        gates = gate_ref[0, rs, :]
        gc_pad = cumsum_rows(gates)
        gcts.append(gc_pad.T)
        gcs.append(gc_pad[0:C, :])
        eg_ref[0, g * SUBLANES:(g + 1) * SUBLANES, :] = jnp.broadcast_to(
            jnp.exp(gc_pad[C - 1:C, :]), (SUBLANES, LANES))

    def decay_of(g, h):
        return jnp.exp(jnp.where(causal, gcs[g][:, h:h + 1] - gcts[g][h:h + 1, 0:C], NEG_INF))

    for g in range(G):
        rs = slice(g * C, (g + 1) * C)
        a_heads = []
        for h in range(HEADS):
            k = k_ref[0, rs, h * HEAD_W:(h + 1) * HEAD_W]
            kb = k.astype(F32) * gate_ref[0, rs, HEADS + h:HEADS + h + 1]
            q = q_ref[0, rs, h * HEAD_W:(h + 1) * HEAD_W]
            both = _dot_nt(jnp.concatenate([kb.astype(BF16), q], axis=0), k)
            decay = decay_of(g, h)
            a_heads.append(jnp.where(strict, -(both[0:C] * decay), 0.0))
            ai_ref[0, rs, h * C:(h + 1) * C] = (both[C:2 * C] * decay).astype(BF16)
        ps.append(jnp.concatenate(a_heads, axis=1))
        xs.append(eye_all + ps[g])
    def times(lefts, right_diag):
        bh, bl = right_diag
        top = _dot(jnp.concatenate([part for split in lefts for part in split], axis=0), bh)
        bot = _dot(jnp.concatenate([split[0] for split in lefts], axis=0), bl)
        return [top[2 * n * C:(2 * n + 1) * C] + top[(2 * n + 1) * C:(2 * n + 2) * C] + bot[n * C:(n + 1) * C]
                for n in range(len(lefts))]

    n_factors = int(math.log2(C))
    p_split = [_split_bf16(ps[g]) for g in range(G)]
    for g in range(G):
        p_split[g] = _split_bf16(times([p_split[g]], tuple(block_diag(part) for part in p_split[g]))[0])
    for k in range(1, n_factors):
        for g in range(G):
            diag = tuple(block_diag(part) for part in p_split[g])
            if k < n_factors - 1:
                x_prod, p_next = times([_split_bf16(xs[g]), p_split[g]], diag)
                p_split[g] = _split_bf16(p_next)
            else:
                x_prod, = times([_split_bf16(xs[g])], diag)
            xs[g] = xs[g] + x_prod

    for g in range(G):
        rs = slice(g * C, (g + 1) * C)
        for h in range(HEADS):
            sl = slice(h * HEAD_W, (h + 1) * HEAD_W)
            q = q_ref[0, rs, sl].astype(F32)
            v = v_ref[0, rs, sl].astype(F32)
            k = k_ref[0, rs, sl].astype(F32)
            gcol = gcs[g][:, h:h + 1]
            bcol = gate_ref[0, rs, HEADS + h:HEADS + h + 1]
            glast = gcs[g][C - 1:C, h:h + 1]
            egc = jnp.exp(gcol)
            t16 = xs[g][:, h * C:(h + 1) * C].astype(BF16)
            uw = _dot(t16, jnp.concatenate([v * bcol, k * bcol * egc], axis=1).astype(BF16))
            u_ref[0, rs, sl] = uw[:, 0:HEAD_W]
            w_ref[0, rs, sl] = uw[:, HEAD_W:2 * HEAD_W].astype(BF16)
            qd_ref[0, rs, sl] = (q * egc).astype(BF16)
            kt_ref[0, rs, sl] = (k * jnp.exp(glast - gcol)).astype(BF16)


def _gdn_scan_kernel(u_ref, w_ref, qd_ref, kt_ref, ai_ref, eg_ref, gz_ref, s0_ref, gout_ref, o_ref, sfin_ref, s_scr,
                     *, C, GB, n_chunks):
    c = pl.program_id(1)

    @pl.when(c == 0)
    def _():
        s_scr[...] = s0_ref[...]

    chains = [(b, h) for b in range(GB) for h in range(HEADS)]
    cols = lambda h: slice(h * HEAD_W, (h + 1) * HEAD_W)
    s16 = {bh: s_scr[bh[0], bh[1]].astype(BF16) for bh in chains}
    v_new = {(b, h): (u_ref[b, :, cols(h)] - _dot(w_ref[b, :, cols(h)], s16[b, h])).astype(BF16) for b, h in chains}
    o = {(b, h): _dot(qd_ref[b, :, cols(h)], s16[b, h]) + _dot(ai_ref[b, :, h * C:(h + 1) * C], v_new[b, h])
         for b, h in chains}
    for b, h in chains:
        s_scr[b, h] = s_scr[b, h] * eg_ref[b, 0:1, h:h + 1] + lax.dot_general(
            kt_ref[b, :, cols(h)], v_new[b, h], (((0,), (0,)), ((), ())), preferred_element_type=F32)
    for b, h in chains:
        o_ref[b, :, cols(h)] = (_rms(o[b, h], gout_ref[...]) * _silu(gz_ref[b, :, cols(h)].astype(F32))).astype(BF16)

    @pl.when(c == n_chunks - 1)
    def _():
        sfin_ref[...] = s_scr[...]


def _gdn(q, k, v, gz, gates, s0, g_out, C):
    B, T, _ = q.shape
    nc = T // C
    G = min(nc, 8)
    GB = next(n for n in (8, 4, 2, 1) if B % n == 0)
    HC = HEADS * C
    pblk = lambda w: pl.BlockSpec((1, G * C, w), lambda b, c: (b, c, 0))
    row = lambda w, dt: jax.ShapeDtypeStruct((B, T, w), dt)
    u, w, qd, kt, ai, eg = pl.pallas_call(
        functools.partial(_gdn_prep_kernel, C=C, G=G),
        grid=(B, nc // G),
        in_specs=[pblk(MIX_HALF)] * 3 + [pblk(LANES)],
        out_specs=[pblk(MIX_HALF)] * 4 + [pblk(HC), pl.BlockSpec((1, G * SUBLANES, LANES), lambda b, c: (b, c, 0))],
        out_shape=[row(MIX_HALF, F32), row(MIX_HALF, BF16), row(MIX_HALF, BF16), row(MIX_HALF, BF16), row(HC, BF16),
                   jax.ShapeDtypeStruct((B, nc * SUBLANES, LANES), F32)],
        compiler_params=_params(("parallel", "parallel")),
        name="gdn_prep",
    )(q, k, v, gates)
    sblk = lambda w: pl.BlockSpec((GB, C, w), lambda b, c: (b, c, 0))
    st = pl.BlockSpec((GB, HEADS, HEAD_W, HEAD_W), lambda b, c: (b, 0, 0, 0))
    return pl.pallas_call(
        functools.partial(_gdn_scan_kernel, C=C, GB=GB, n_chunks=nc),
        grid=(B // GB, nc),
        in_specs=[sblk(MIX_HALF)] * 4 + [sblk(HC), pl.BlockSpec((GB, SUBLANES, LANES), lambda b, c: (b, c, 0)),
                                         sblk(MIX_HALF), st, pl.BlockSpec(g_out.shape, lambda b, c: (0, 0))],
        out_specs=[sblk(MIX_HALF), st],
        out_shape=[jax.ShapeDtypeStruct((B, T, MIX_HALF), BF16),
                   jax.ShapeDtypeStruct((B, HEADS, HEAD_W, HEAD_W), F32)],
        scratch_shapes=[pltpu.VMEM((GB, HEADS, HEAD_W, HEAD_W), F32)],
        compiler_params=_params(("parallel", "arbitrary")),
        name="gdn_scan",
    )(u, w, qd, kt, ai, eg, gz, s0, g_out)


def _memkv_kernel(m_ref, g_ref, wk_ref, wv_ref, kf_ref, kb_ref, vf_ref, vb_ref):
    m = _rms(m_ref[...], g_ref[...]).astype(BF16)
    mk = _dot(m, wk_ref[...])
    mv = _dot(m, wv_ref[...])
    kf_ref[...] = mk
    kb_ref[...] = mk.astype(BF16)
    vf_ref[...] = mv
    vb_ref[...] = mv.astype(BF16)


def _memkv(mem, g_mem, w_ck, w_cv, tm):
    N, D = mem.shape
    blk = pl.BlockSpec((tm, D), lambda i: (i, 0))
    full = lambda a: pl.BlockSpec(a.shape, lambda i: (0,) * a.ndim)
    return pl.pallas_call(
        _memkv_kernel,
        grid=(N // tm,),
        in_specs=[blk, full(g_mem), full(w_ck), full(w_cv)],
        out_specs=[blk] * 4,
        out_shape=[jax.ShapeDtypeStruct((N, D), F32), jax.ShapeDtypeStruct((N, D), BF16)] * 2,
        compiler_params=_params(("parallel",)),
        name="memkv",
    )(mem, g_mem, w_ck, w_cv)


def _mid_kernel(x_ref, da_ref, gd_ref, wo_ref, gc_ref, wq_ref, mk_ref, mv_ref, wco_ref, gf_ref, wr_ref, br_ref,
                x2_ref, hf_ref, comb_ref):
    x1 = x_ref[0] + _dot(da_ref[0], wo_ref[0:MIX_HALF, :]) + _dot(gd_ref[0], wo_ref[MIX_HALF:2 * MIX_HALF, :])
    hc = _rms(x1, gc_ref[...]).astype(BF16)
    q = (_dot(hc, wq_ref[...]) * (X_DH ** -0.5)).astype(BF16)
    heads = []
    for h in range(X_HEADS):
        sl = slice(h * X_DH, (h + 1) * X_DH)
        s = _dot_nt(q[:, sl], mk_ref[0, :, sl])
        p = jnp.exp(s - jnp.max(s, axis=-1, keepdims=True))
        l = jnp.sum(p, axis=-1, keepdims=True)
        heads.append((_dot(p.astype(BF16), mv_ref[0, :, sl]) / l).astype(BF16))
    x2 = x1 + _dot(jnp.concatenate(heads, axis=1), wco_ref[...])
    x2_ref[0] = x2
    hf = _rms(x2, gf_ref[...])
    hf_ref[0] = hf.astype(BF16)

    logits = _dot3(hf, wr_ref[...]) + br_ref[...]
    lane_i = lax.broadcasted_iota(jnp.int32, logits.shape, 1)
    lane = lane_i.astype(F32)
    big = float(LANES)
    is_grp = (lane_i >= N_EXPERTS) & (lane_i < N_EXPERTS + N_GROUPS)
    gl = jnp.where(is_grp, logits, NEG_INF)
    gmax = jnp.max(gl, axis=-1, keepdims=True)
    g_top = 1.0 / jnp.sum(jnp.exp(gl - gmax), axis=-1, keepdims=True)
    g_idx = jnp.min(jnp.where(gl == gmax, lane, big), axis=-1, keepdims=True) - float(N_EXPERTS)
    in_grp = (lane_i < N_EXPERTS) & (jnp.floor(lane * (1.0 / E_PER_GROUP)) == g_idx)
    el = jnp.where(in_grp, logits, NEG_INF)
    e1 = jnp.max(el, axis=-1, keepdims=True)
    i1 = jnp.min(jnp.where(el == e1, lane, big), axis=-1, keepdims=True)
    el2 = jnp.where(lane == i1, NEG_INF, el)
    e2 = jnp.max(el2, axis=-1, keepdims=True)
    i2 = jnp.min(jnp.where(el2 == e2, lane, big), axis=-1, keepdims=True)
    r = jnp.exp(e2 - e1)
    w1 = g_top / (1.0 + r)
    w2 = g_top * r / (1.0 + r)
    comb_ref[0] = jnp.where(lane_i == 0, i1, jnp.where(lane_i == 1, i2, jnp.where(
        lane_i == 2, w1, jnp.where(lane_i == 3, w2, 0.0))))


def _mid(x, mix_da, mix_gd, w_out, g_cross, w_cq, mk, mv, w_co, g_ffn, w_router, b_router, tm):
    B, T, D = x.shape
    M = mk.shape[1]
    blk = lambda w: pl.BlockSpec((1, tm, w), lambda b, t: (b, t, 0))
    mem = pl.BlockSpec((1, M, D), lambda b, t: (b, 0, 0))
    full = lambda a: pl.BlockSpec(a.shape, lambda b, t: (0,) * a.ndim)
    return pl.pallas_call(
        _mid_kernel,
        grid=(B, T // tm),
        in_specs=[blk(D), blk(MIX_HALF), blk(MIX_HALF), full(w_out), full(g_cross), full(w_cq), mem, mem,
                  full(w_co), full(g_ffn), full(w_router), full(b_router)],
        out_specs=[blk(D), blk(D), blk(LANES)],
        out_shape=[jax.ShapeDtypeStruct((B, T, D), F32), jax.ShapeDtypeStruct((B, T, D), BF16),
                   jax.ShapeDtypeStruct((B, T, LANES), F32)],
        compiler_params=_params(("parallel", "parallel")),
        name="mid",
    )(x, mix_da, mix_gd, w_out, g_cross, w_cq, mk, mv, w_co, g_ffn, w_router, b_router)


MOE_BLK = 16
EXP_TILE_BLKS = 16


def _slots_per_tile(tm):
    worst = 2 * tm + N_EXPERTS * (MOE_BLK - 1)
    return -(-worst // LANES) * LANES


def _route_kernel(hf_ref, rec_ref, xs_ref, slot_ref, meta_ref, *, tm, sl, n_tiles):
    @pl.when(pl.program_id(0) == n_tiles)
    def _():
        xs_ref[...] = jnp.zeros(xs_ref.shape, BF16)
        slot_ref[...] = jnp.zeros(slot_ref.shape, F32)
        meta_ref[...] = jnp.zeros(meta_ref.shape, F32)

    @pl.when(pl.program_id(0) < n_tiles)
    def _():
        _route_tile(hf_ref, rec_ref, xs_ref, slot_ref, meta_ref, tm, sl)


def _route_tile(hf_ref, rec_ref, xs_ref, slot_ref, meta_ref, tm, sl):
    rec = rec_ref[...]
    lane_i = lax.broadcasted_iota(jnp.int32, rec.shape, 1)
    lane = lane_i.astype(F32)
    i1, i2, w1, w2 = rec[:, 0:1], rec[:, 1:2], rec[:, 2:3], rec[:, 3:4]
    oh1 = jnp.where(lane == i1, 1.0, 0.0)
    oh2 = jnp.where(lane == i2, 1.0, 0.0)
    oh = oh1 + oh2
    nblk = jnp.floor((jnp.sum(oh, axis=0, keepdims=True) + (MOE_BLK - 1.0)) * (1.0 / MOE_BLK))
    ur = lax.broadcasted_iota(jnp.int32, (LANES, LANES), 0)
    uc = lax.broadcasted_iota(jnp.int32, (LANES, LANES), 1)
    start = MOE_BLK * _dot(jnp.broadcast_to(nblk, (2 * SUBLANES, LANES)).astype(BF16),
                           jnp.where(ur < uc, 1.0, 0.0).astype(BF16))[0:1]
    r = lax.broadcasted_iota(jnp.int32, (tm, tm), 0)
    c = lax.broadcasted_iota(jnp.int32, (tm, tm), 1)
    pos = start + _dot(jnp.where(r > c, 1.0, 0.0).astype(BF16), oh.astype(BF16))
    slot1 = jnp.sum(oh1 * pos, axis=-1, keepdims=True)
    slot2 = jnp.sum(oh2 * pos, axis=-1, keepdims=True)
    slot_ref[...] = jnp.where(lane_i == 0, slot1, jnp.where(lane_i == 1, slot2, 0.0))
    meta_ref[0] = jnp.broadcast_to(nblk, (SUBLANES, LANES))

    srow = lax.broadcasted_iota(jnp.int32, (sl, tm), 0).astype(F32)
    perm = (jnp.where(srow == jnp.broadcast_to(slot1, (tm, LANES)).T[0:1, :], 1.0, 0.0)
            + jnp.where(srow == jnp.broadcast_to(slot2, (tm, LANES)).T[0:1, :], 2.0, 0.0)).astype(BF16)

    def parts(w, base):
        hi = w.astype(BF16).astype(F32)
        mid = (w - hi).astype(BF16).astype(F32)
        lo = w - hi - mid
        return jnp.where(lane_i == base, hi, jnp.where(lane_i == base + 1, mid,
                                                       jnp.where(lane_i == base + 2, lo, 0.0)))

    extra = parts(w1, 0) + parts(w2, 3) + jnp.where(lane_i == 6, 1.0, 0.0)
    moved = _dot(perm, jnp.concatenate([hf_ref[...], extra.astype(BF16)], axis=1))
    d = hf_ref.shape[1]
    second = moved[:, d + 6:d + 7] == 2.0
    moved = moved * jnp.where(second, 0.5, 1.0)
    lane_s = lax.broadcasted_iota(jnp.int32, (sl, LANES), 1)
    xs_ref[:, 0:d] = moved[:, 0:d].astype(BF16)
    xs_ref[:, d:d + LANES] = jnp.where(lane_s == 7, jnp.where(second, 1.0, 0.0), moved[:, d:d + LANES]).astype(BF16)


def _route(hf, rec, tm):
    N, D = hf.shape
    nt = N // tm
    sl = _slots_per_tile(tm)
    in_blk = lambda w: pl.BlockSpec((tm, w), lambda i: (jnp.minimum(i, nt - 1), 0))
    blk = lambda rows, w: pl.BlockSpec((rows, w), lambda i: (i, 0))
    return pl.pallas_call(
        functools.partial(_route_kernel, tm=tm, sl=sl, n_tiles=nt),
        grid=(nt + 1,),
        in_specs=[in_blk(D), in_blk(LANES)],
        out_specs=[blk(sl, D + LANES), blk(tm, LANES), pl.BlockSpec((1, SUBLANES, LANES), lambda i: (i, 0, 0))],
        out_shape=[jax.ShapeDtypeStruct(((nt + 1) * sl, D + LANES), BF16),
                   jax.ShapeDtypeStruct(((nt + 1) * tm, LANES), F32),
                   jax.ShapeDtypeStruct((nt + 1, SUBLANES, LANES), F32)],
        compiler_params=_params(("parallel",)),
        name="moe_route",
    )(hf, rec)


def _moe_tables(meta, blocks_per_tile, max_tiles):
    nblk = meta[:, 0, :N_EXPERTS].astype(jnp.int32)
    nt = nblk.shape[0]
    tiles_e = (jnp.sum(nblk, axis=0) + EXP_TILE_BLKS - 1) // EXP_TILE_BLKS
    tile_end = jnp.cumsum(tiles_e)
    first_blk = (tile_end - tiles_e) * EXP_TILE_BLKS
    off = first_blk[None, :] + jnp.cumsum(nblk, axis=0) - nblk
    lend = jnp.cumsum(nblk, axis=1)
    lstart = lend - nblk
    lb = jnp.arange(blocks_per_tile, dtype=jnp.int32)[None, :]
    in_run = (lstart[:, None, :] <= lb[:, :, None]) & (lb[:, :, None] < lend[:, None, :])
    valid = lb < lend[:, -1:]
    dest = jnp.sum(jnp.where(in_run, (off - lstart)[:, None, :], 0), axis=-1) + lb
    flat = jnp.arange(nt, dtype=jnp.int32)[:, None] * blocks_per_tile + lb
    n_sorted = max_tiles * EXP_TILE_BLKS
    src = jnp.full((n_sorted,), -1, jnp.int32).at[jnp.where(valid, dest, n_sorted).reshape(-1)].set(
        flat.reshape(-1), mode="drop")
    g = jnp.arange(max_tiles, dtype=jnp.int32)
    tile_expert = jnp.minimum(jnp.sum(tile_end[None, :] <= g[:, None], axis=-1), N_EXPERTS - 1).astype(jnp.int32)
    return src, tile_expert, tile_end[-1:].astype(jnp.int32)


def _experts_kernel(src_ref, te_ref, nt_ref, xs_hbm, wg_ref, wu_ref, wd_ref, ys_hbm, xbuf, obuf, gsem, ssem, *,
                    zero_blk, spare_blk):
    g = pl.program_id(0)
    n_tiles = nt_ref[0]
    nb = EXP_TILE_BLKS
    d = obuf.shape[-1]

    def gather_copy(tile, slot, i):
        s = src_ref[tile * nb + i]
        return pltpu.make_async_copy(xs_hbm.at[jnp.where(s >= 0, s, zero_blk)], xbuf.at[slot, i], gsem.at[slot])

    def scatter_copy(tile, slot, i):
        s = src_ref[tile * nb + i]
        dst = jnp.where(s >= 0, s, spare_blk + slot * nb + i)
        return pltpu.make_async_copy(obuf.at[slot, i], ys_hbm.at[dst, :, pl.ds(0, d)], ssem.at[slot])

    def gather_start(tile, slot):
        for i in range(nb):
            gather_copy(tile, slot, i).start()

    def gather_wait(tile, slot):
        for i in range(nb):
            gather_copy(tile, slot, i).wait()

    def scatter_start(tile, slot):
        for i in range(nb):
            scatter_copy(tile, slot, i).start()

    def scatter_wait(tile, slot):
        for i in range(nb):
            scatter_copy(tile, slot, i).wait()

    slot = g % 2
    xslot = g % 3

    @pl.when(g == 0)
    def _():
        gather_start(0, 0)

        @pl.when(n_tiles > 1)
        def _():
            gather_start(1, 1)

    @pl.when(g < n_tiles)
    def _():
        @pl.when(g + 2 < n_tiles)
        def _():
            gather_start(g + 2, (g + 2) % 3)

        gather_wait(g, xslot)
        rows = nb * MOE_BLK
        xe = xbuf[xslot].reshape(rows, xbuf.shape[-1])
        x = xe[:, 0:d]
        extra = xe[:, d:d + LANES].astype(F32)
        lane = lax.broadcasted_iota(jnp.int32, extra.shape, 1)
        first_lane = 3.0 * jnp.sum(jnp.where(lane == 7, extra, 0.0), axis=-1, keepdims=True)
        lane_f = lane.astype(F32)
        mine = (lane_f >= first_lane) & (lane_f < first_lane + 3.0)
        ccol = jnp.sum(jnp.where(mine, extra, 0.0), axis=-1, keepdims=True)
        hg = _silu(_dot(x, wg_ref[0])) * _dot(x, wu_ref[0])
        y = _dot((hg * ccol).astype(BF16), wd_ref[0]).astype(BF16)

        @pl.when(g >= 2)
        def _():
            scatter_wait(g - 2, slot)

        obuf[slot] = y.reshape(nb, MOE_BLK, d)
        scatter_start(g, slot)

    @pl.when(g == n_tiles - 1)
    def _():
        scatter_wait(g, slot)

        @pl.when(g >= 1)
        def _():
            scatter_wait(g - 1, 1 - slot)


def _experts(src, tile_expert, n_tiles, xs, w_gate, w_up, w_down, max_tiles, zero_blk):
    n_rows, width = xs.shape
    D = width - LANES
    slab = xs.reshape(n_rows // MOE_BLK, MOE_BLK, width)
    wspec = lambda shape: pl.BlockSpec((1,) + shape, lambda g, src, te, nt: (te[g], 0, 0))
    grid_spec = pltpu.PrefetchScalarGridSpec(
        num_scalar_prefetch=3,
        grid=(max_tiles,),
        in_specs=[pl.BlockSpec(memory_space=pl.ANY), wspec((D, D_EXPERT)), wspec((D, D_EXPERT)), wspec((D_EXPERT, D))],
        out_specs=pl.BlockSpec(memory_space=pl.ANY),
        scratch_shapes=[pltpu.VMEM((3, EXP_TILE_BLKS, MOE_BLK, width), BF16),
                        pltpu.VMEM((2, EXP_TILE_BLKS, MOE_BLK, D), BF16),
                        pltpu.SemaphoreType.DMA((3,)), pltpu.SemaphoreType.DMA((2,))],
    )
    return pl.pallas_call(
        functools.partial(_experts_kernel, zero_blk=zero_blk, spare_blk=zero_blk + 1),
        grid_spec=grid_spec,
        out_shape=jax.ShapeDtypeStruct(slab.shape, BF16),
        input_output_aliases={3: 0},
        compiler_params=_params(("arbitrary",)),
        name="moe_experts",
    )(src, tile_expert, n_tiles, slab, w_gate, w_up, w_down).reshape(n_rows, width)


def _combine_kernel(x2_ref, slot_ref, ys_ref, gfin_ref, y_ref, *, tm, sl):
    slots = slot_ref[...]
    col = lax.broadcasted_iota(jnp.int32, (tm, sl), 1).astype(F32)
    pick = (jnp.where(col == slots[:, 0:1], 1.0, 0.0) + jnp.where(col == slots[:, 1:2], 1.0, 0.0)).astype(BF16)
    y_ref[...] = _rms(x2_ref[...] + _dot(pick, ys_ref[:, 0:x2_ref.shape[1]]), gfin_ref[...])


def _combine(x2, slots, ys, g_final, tm, sl):
    N, D = x2.shape
    blk = lambda rows, w: pl.BlockSpec((rows, w), lambda i: (i, 0))
    return pl.pallas_call(
        functools.partial(_combine_kernel, tm=tm, sl=sl),
        grid=(N // tm,),
        in_specs=[blk(tm, D), blk(tm, LANES), blk(sl, D), pl.BlockSpec(g_final.shape, lambda i: (0, 0))],
        out_specs=blk(tm, D),
        out_shape=jax.ShapeDtypeStruct((N, D), F32),
        compiler_params=_params(("parallel",)),
        name="moe_combine",
    )(x2, slots, ys, g_final)


def _moe(x2, hf, rec, w_gate, w_up, w_down, g_final, tm):
    N, _ = x2.shape
    nt = N // tm
    sl = _slots_per_tile(tm)
    xs, slots, meta = _route(hf, rec, tm)
    blocks_per_tile = sl // MOE_BLK
    max_tiles = nt * blocks_per_tile // EXP_TILE_BLKS + N_EXPERTS
    src, tile_expert, n_tiles = _moe_tables(meta[:nt], blocks_per_tile, max_tiles)
    ys = _experts(src, tile_expert, n_tiles, xs, w_gate, w_up, w_down, max_tiles, nt * blocks_per_tile)
    return _combine(x2, slots, ys, g_final, tm, sl)


def _pad_rows(a, rows):
    return jnp.pad(a, ((0, rows - a.shape[0]), (0, LANES - a.shape[1])))


def _trunk(x, mk16, mv16, past_k, past_v, conv_prev, s_prev, lam_init, wts):
    B, T, D = x.shape
    tm = min(T, 512)
    if conv_prev is None:
        conv_prev = jnp.zeros((B, CONV_W - 1, GDN_QKV), F32)
    if s_prev is None:
        s_prev = jnp.zeros((B, HEADS, HEAD_W, HEAD_W), F32)
    prev8 = jnp.pad(conv_prev, ((0, 0), (SUBLANES - (CONV_W - 1), 0), (0, 0)))
    (dq, dkf, dk16, dvf, dv16, cq, ck, cv, gz, gates, cnew) = _inproj(
        x, prev8, wts["g_mix"], wts["w_main"], wts["w_gate8"], wts["conv_w8"], wts["gate_par"], tm)
    if past_k is None:
        mix_da = _da_prompt(wts["lam_par"], wts["g_subln"], dq, dk16, dv16, min(T, 512), lam_init)
    else:
        P = past_k.shape[1]
        mix_da = _da_sample(wts["lam_par"], wts["g_subln"], dq, past_k.reshape(B, P * HEADS, HEAD_W),
                            past_v.reshape(B, P * HEADS, HEAD_W), dk16, dv16, lam_init)
    C = min(T, CHUNK)
    mix_gd, s_new = _gdn(cq, ck, cv, gz, gates, s_prev, wts["g_gdn_out"], C)
    x2, hf, comb = _mid(x, mix_da, mix_gd, wts["w_out"], wts["g_cross"], wts["w_cq"], mk16, mv16, wts["w_co"],
                        wts["g_ffn"], wts["w_router"], wts["b_router"], min(T, 512))
    N = B * T
    y = _moe(x2.reshape(N, D), hf.reshape(N, D), comb.reshape(N, LANES), wts["w_gate"], wts["w_up"],
             wts["w_down"], wts["g_final"], min(N, 512))
    new_k = dkf.reshape(1, B, T, HEADS, HEAD_W)
    new_v = dvf.reshape(1, B, T, HEADS, HEAD_W)
    new_conv = cnew[:, SUBLANES - (CONV_W - 1):, :].reshape(1, B, CONV_W - 1, GDN_QKV)
    return y.reshape(B, T, D), new_k, new_v, new_conv, s_new[None]


def kernel(x_prompt, x_sample, cache_diff_k, cache_diff_v, state_gdn_conv, state_gdn, cache_mem_k, cache_mem_v,
           mem_prompt, g_mix, w_in, lam_q1, lam_k1, lam_q2, lam_k2, g_subln, conv_w, a_log, dt_bias, g_gdn_out,
           w_out, g_cross, g_mem, w_ck, w_cv, w_cq, w_co, g_ffn, w_grp, b_grp, w_rt, b_rt, w_gate, w_up, w_down,
           g_final):
    assert g_mix.shape[0] == 1, "single-layer model"
    l = 0
    lam_init = 0.8 - 0.6 * math.exp(-0.3 * l)
    gate_lo = 3 * MIX_HALF + GDN_QKV
    gate_hi = gate_lo + 2 * HEADS
    wi = w_in[l]
    row = lambda a: a.reshape(1, -1).astype(F32)
    wts = dict(
        g_mix=row(g_mix[l]),
        w_main=jnp.concatenate([wi[:, :gate_lo], wi[:, gate_hi:]], axis=1).astype(BF16),
        w_gate8=jnp.pad(wi[:, gate_lo:gate_hi], ((0, 0), (0, LANES - 2 * HEADS))).astype(BF16),
        conv_w8=jnp.pad(conv_w[l], ((0, SUBLANES - CONV_W), (0, 0))),
        gate_par=_pad_rows(jnp.stack([a_log[l], dt_bias[l]]), SUBLANES),
        lam_par=_pad_rows(jnp.stack([lam_q1[l], lam_k1[l], lam_q2[l], lam_k2[l]]), SUBLANES),
        g_subln=row(g_subln[l]), g_gdn_out=row(g_gdn_out[l]),
        w_out=w_out[l].astype(BF16), g_cross=row(g_cross[l]), w_cq=w_cq[l].astype(BF16), w_co=w_co[l].astype(BF16),
        g_ffn=row(g_ffn[l]),
        w_router=jnp.pad(jnp.concatenate([w_rt[l], w_grp[l]], axis=1),
                         ((0, 0), (0, LANES - N_EXPERTS - N_GROUPS))),
        b_router=jnp.pad(jnp.concatenate([b_rt[l], b_grp[l]]), (0, LANES - N_EXPERTS - N_GROUPS)).reshape(1, LANES),
        w_gate=w_gate[l].astype(BF16), w_up=w_up[l].astype(BF16), w_down=w_down[l].astype(BF16),
        g_final=row(g_final),
    )
    B, M, D = mem_prompt.shape
    mkf, mk16, mvf, mv16 = _memkv(mem_prompt.reshape(B * M, D), row(g_mem[l]), w_ck[l].astype(BF16),
                                  w_cv[l].astype(BF16), min(B * M, 512))
    yp, pk, pv, pc, ps = _trunk(x_prompt, mk16.reshape(B, M, D), mv16.reshape(B, M, D), None, None, None, None,
                                lam_init, wts)
    Bs = x_sample.shape[0]
    ys, sk, sv, sc, ss = _trunk(x_sample, cache_mem_k[l].reshape(Bs, M, D).astype(BF16),
                                cache_mem_v[l].reshape(Bs, M, D).astype(BF16), cache_diff_k[l], cache_diff_v[l],
                                state_gdn_conv[l], state_gdn[l], lam_init, wts)
    mem_shape = (1, B, M, X_HEADS, X_DH)
    return (yp, ys, pk, pv, pc, ps, mkf.reshape(mem_shape), mvf.reshape(mem_shape), sk, sv, sc, ss)
```

```python
import functools
import math

import jax
import jax.numpy as jnp
import numpy as np
from jax import lax
from jax.experimental import pallas as pl
from jax.experimental.pallas import tpu as pltpu

F32 = jnp.float32
BF16 = jnp.bfloat16

D_MODEL = 1024
CHUNK = 64
HEADS = 4
HEAD_W = 128
DA_DH = 64
MIX_HALF = HEADS * HEAD_W
GDN_QKV = 3 * MIX_HALF
CONV_W = 4
X_HEADS = 4
X_DH = 256
N_GROUPS = 4
E_PER_GROUP = 8
N_EXPERTS = 32
D_EXPERT = 256
NORM_EPS = 1e-6
NEG_INF = -1e30
LANES = 128
SUBLANES = 8
VMEM_LIMIT = 56 * 1024 * 1024


def _dot(a, b):
    return jnp.dot(a, b, preferred_element_type=F32)


def _dot_nt(a, b):
    return lax.dot_general(a, b, (((1,), (1,)), ((), ())), preferred_element_type=F32)


def _rms(x, g):
    return x * lax.rsqrt(jnp.mean(x * x, axis=-1, keepdims=True) + NORM_EPS) * g


def _sigmoid(x):
    return 1.0 / (1.0 + jnp.exp(-x))


def _silu(x):
    return x * _sigmoid(x)


def _split_bf16(a):
    hi = a.astype(BF16)
    lo = (a - hi.astype(F32)).astype(BF16)
    return hi, lo


def _dot3(a, b):
    ah, al = _split_bf16(a)
    bh, bl = _split_bf16(b)
    return _dot(ah, bh) + (_dot(ah, bl) + _dot(al, bh))


def _params(sem):
    return pltpu.CompilerParams(dimension_semantics=sem, vmem_limit_bytes=VMEM_LIMIT)


def _inproj_kernel(x_ref, prev_ref, gmix_ref, wm_ref, wg_ref, cw_ref, gp_ref,
                   dq_ref, dkf_ref, dkb_ref, dvf_ref, dvb_ref, cq_ref, ck_ref, cv_ref, gz_ref, gate_ref,
                   cnew_ref, xp_scr, *, tm):
    t = pl.program_id(1)
    h = _rms(x_ref[0], gmix_ref[...]).astype(BF16)

    @pl.when(t == 0)
    def _():
        xp_scr[0:SUBLANES, :] = prev_ref[0]

    conv_col0 = 3 * MIX_HALF
    outs = (cq_ref, ck_ref, cv_ref)

    def conv_part(part):
        cs = slice(part * MIX_HALF, (part + 1) * MIX_HALF)
        xp_scr[SUBLANES:SUBLANES + tm, cs] = _dot(
            h, wm_ref[:, conv_col0 + part * MIX_HALF:conv_col0 + (part + 1) * MIX_HALF])
        y = xp_scr[5:5 + tm, cs] * cw_ref[0:1, cs]
        for j in range(1, CONV_W):
            y = y + xp_scr[5 + j:5 + j + tm, cs] * cw_ref[j:j + 1, cs]
        c = _silu(y)
        if part == 2:
            outs[part][0] = c
        else:
            scale = (HEAD_W ** -0.5) if part == 0 else 1.0
            for hh in range(HEADS):
                ch = c[:, hh * HEAD_W:(hh + 1) * HEAD_W]
                n = ch * lax.rsqrt(jnp.sum(ch * ch, axis=-1, keepdims=True) + NORM_EPS)
                outs[part][0, :, hh * HEAD_W:(hh + 1) * HEAD_W] = n * scale if part == 0 else n

    conv_part(0)
    dq = _dot(h, wm_ref[:, 0:MIX_HALF])
    dq_ref[0] = (dq * (DA_DH ** -0.5)).astype(BF16)
    conv_part(1)
    dk = _dot(h, wm_ref[:, MIX_HALF:2 * MIX_HALF])
    dkb_ref[0] = dk.astype(BF16)
    conv_part(2)
    dv = _dot(h, wm_ref[:, 2 * MIX_HALF:3 * MIX_HALF])
    dvb_ref[0] = dv.astype(BF16)
    for hh in range(HEADS):
        dkf_ref[0, pl.ds(hh, tm, stride=HEADS), :] = dk[:, hh * HEAD_W:(hh + 1) * HEAD_W]
        dvf_ref[0, pl.ds(hh, tm, stride=HEADS), :] = dv[:, hh * HEAD_W:(hh + 1) * HEAD_W]

    gates = _dot(h, wg_ref[...])
    z = gates + gp_ref[1:2, :]
    softplus = jnp.maximum(z, 0.0) + jnp.log1p(jnp.exp(-jnp.abs(z)))
    g_all = -jnp.exp(gp_ref[0:1, :]) * softplus
    lane = lax.broadcasted_iota(jnp.int32, gates.shape, 1)
    gate_ref[0] = jnp.where(lane < HEADS, g_all, jnp.where(lane < 2 * HEADS, _sigmoid(gates), 0.0))
    gz_ref[0] = _dot(h, wm_ref[:, 3 * MIX_HALF + GDN_QKV:4 * MIX_HALF + GDN_QKV])
    carry = xp_scr[tm:tm + SUBLANES, :]
    xp_scr[0:SUBLANES, :] = carry
    cnew_ref[0] = carry


def _inproj(x, prev8, g_mix, w_main, w_gate, conv_w8, gate_par, tm):
    B, T, D = x.shape
    nt = T // tm
    row = lambda w, dt: jax.ShapeDtypeStruct((B, T, w), dt)
    blk = lambda w: pl.BlockSpec((1, tm, w), lambda b, t: (b, t, 0))
    full = lambda a: pl.BlockSpec(a.shape, lambda b, t: (0,) * a.ndim)
    cache_blk = pl.BlockSpec((1, tm * HEADS, HEAD_W), lambda b, t: (b, t, 0))
    cache_rows = jax.ShapeDtypeStruct((B, T * HEADS, HEAD_W), F32)
    return pl.pallas_call(
        functools.partial(_inproj_kernel, tm=tm),
        grid=(B, nt),
        in_specs=[blk(D), pl.BlockSpec((1, SUBLANES, GDN_QKV), lambda b, t: (b, 0, 0)),
                  full(g_mix), full(w_main), full(w_gate), full(conv_w8), full(gate_par)],
        out_specs=[blk(MIX_HALF), cache_blk, blk(MIX_HALF), cache_blk] + [blk(MIX_HALF)] * 5
        + [blk(LANES), pl.BlockSpec((1, SUBLANES, GDN_QKV), lambda b, t: (b, 0, 0))],
        out_shape=[row(MIX_HALF, BF16), cache_rows, row(MIX_HALF, BF16), cache_rows,
                   row(MIX_HALF, BF16), row(MIX_HALF, F32), row(MIX_HALF, F32), row(MIX_HALF, F32),
                   row(MIX_HALF, F32), row(LANES, F32), jax.ShapeDtypeStruct((B, SUBLANES, GDN_QKV), F32)],
        scratch_shapes=[pltpu.VMEM((tm + SUBLANES, GDN_QKV), F32)],
        compiler_params=_params(("parallel", "arbitrary")),
        name="inproj",
    )(x, prev8, g_mix, w_main, w_gate, conv_w8, gate_par)


def _lam(lam_ref, lam_init):
    s1 = jnp.sum(lam_ref[0:1, :] * lam_ref[1:2, :], axis=-1, keepdims=True)
    s2 = jnp.sum(lam_ref[2:3, :] * lam_ref[3:4, :], axis=-1, keepdims=True)
    return jnp.exp(s1) - jnp.exp(s2) + lam_init


def _stack_q(q):
    lane = lax.broadcasted_iota(jnp.int32, q.shape, 1)
    zero = jnp.zeros_like(q)
    return jnp.concatenate([jnp.where(lane < DA_DH, q, zero), jnp.where(lane >= DA_DH, q, zero)], axis=0)


def _da_finish(acc, l, lam, gs, lam_init, tq):
    o = acc[:tq] / l[:tq] - lam * (acc[tq:] / l[tq:])
    return (_rms(o, gs) * (1.0 - lam_init)).astype(BF16)


def _da_prompt_kernel(lam_ref, gs_ref, q_ref, k_ref, v_ref, o_ref, vx_scr, m_scr, acc_scr, p_scr, al_scr, *,
                      tq, rows, heads, lam_init):
    i = pl.program_id(2)
    cols = lambda hh: slice(hh * HEAD_W, (hh + 1) * HEAD_W)

    @pl.when(i == 0)
    def _():
        for hh in range(heads):
            vx_scr[hh, :, 0:HEAD_W] = v_ref[0, :, cols(hh)]
            vx_scr[hh, :, HEAD_W:2 * HEAD_W] = jnp.ones((vx_scr.shape[1], HEAD_W), BF16)
    qz = [_stack_q(q_ref[0, :, cols(hh)]) for hh in range(heads)]
    m_scr[...] = jnp.full(m_scr.shape, NEG_INF, F32)
    acc_scr[...] = jnp.zeros(acc_scr.shape, F32)

    def keys_seen(r0):
        return min(tq, -(-((r0 % tq) + rows) // LANES) * LANES)

    def scores(hh, j, r0, masked):
        kw = keys_seen(r0) if masked else tq
        kj = k_ref[0, pl.ds(pl.multiple_of(j * tq, tq), kw), cols(hh)]
        s = _dot_nt(qz[hh][r0:r0 + rows], kj)
        if masked:
            r = lax.broadcasted_iota(jnp.int32, s.shape, 0) + (r0 % tq)
            c = lax.broadcasted_iota(jnp.int32, s.shape, 1)
            s = jnp.where((r // CHUNK) >= (c // CHUNK), s, NEG_INF)
        m_prev = m_scr[hh, r0:r0 + rows, :]
        m_new = jnp.maximum(m_prev, jnp.max(s, axis=-1, keepdims=True))
        al_scr[hh, r0:r0 + rows, :] = jnp.exp(m_prev - m_new)
        for c in range(kw // LANES):
            cs = slice(c * LANES, (c + 1) * LANES)
            p_scr[hh, r0:r0 + rows, cs] = jnp.exp(s[:, cs] - m_new).astype(BF16)
        m_scr[hh, r0:r0 + rows, :] = m_new

    def accumulate(hh, j, r0, diagonal):
        kw = keys_seen(r0) if diagonal else tq
        vj = vx_scr[hh, pl.ds(pl.multiple_of(j * tq, tq), kw), :]
        alpha = al_scr[hh, r0:r0 + rows, :]
        acc_scr[hh, r0:r0 + rows, :] = (jnp.concatenate([alpha, alpha], axis=1) * acc_scr[hh, r0:r0 + rows, :]
                                        + _dot(p_scr[hh, r0:r0 + rows, 0:kw], vj))

    def stages(acc_j=None, score_j=None, masked=False, diagonal=False):
        for r0 in range(0, 2 * tq, rows):
            for hh in range(heads):
                if acc_j is not None:
                    accumulate(hh, acc_j, r0, diagonal)
                if score_j is not None:
                    scores(hh, score_j, r0, masked)

    @pl.when(i == 0)
    def _():
        stages(score_j=0, masked=True)

    @pl.when(i > 0)
    def _():
        stages(score_j=0)

        def body(t, carry):
            stages(acc_j=t - 1, score_j=t)
            return carry

        lax.fori_loop(1, i, body, 0)
        stages(acc_j=i - 1, score_j=i, masked=True)

    stages(acc_j=i, diagonal=True)
    lam = _lam(lam_ref, lam_init)
    for hh in range(heads):
        acc = acc_scr[hh]
        o = acc[:, 0:HEAD_W] / acc[:, HEAD_W:2 * HEAD_W]
        o = o[0:tq] - lam * o[tq:2 * tq]
        o_ref[0, :, cols(hh)] = (_rms(o, gs_ref[...]) * (1.0 - lam_init)).astype(BF16)


def _da_prompt(lam_par, g_subln, q, k, v, tq, lam_init):
    B, T, _ = q.shape
    hg = 4
    kv_spec = pl.BlockSpec((1, T, hg * HEAD_W), lambda b, h, i: (b, 0, h))
    q_spec = pl.BlockSpec((1, tq, hg * HEAD_W), lambda b, h, i: (b, i, h))
    full = lambda a: pl.BlockSpec(a.shape, lambda b, h, i: (0,) * a.ndim)
    return pl.pallas_call(
        functools.partial(_da_prompt_kernel, tq=tq, rows=min(2 * tq, 128), heads=hg, lam_init=lam_init),
        grid=(B, HEADS // hg, T // tq),
        in_specs=[full(lam_par), full(g_subln), q_spec, kv_spec, kv_spec],
        out_specs=q_spec,
        out_shape=jax.ShapeDtypeStruct((B, T, MIX_HALF), BF16),
        scratch_shapes=[pltpu.VMEM((hg, T, 2 * HEAD_W), BF16), pltpu.VMEM((hg, 2 * tq, LANES), F32),
                        pltpu.VMEM((hg, 2 * tq, 2 * HEAD_W), F32), pltpu.VMEM((hg, 2 * tq, tq), BF16),
                        pltpu.VMEM((hg, 2 * tq, LANES), F32)],
        compiler_params=_params(("parallel", "parallel", "arbitrary")),
        name="diffattn_prompt",
    )(lam_par, g_subln, q, k, v)


def _da_sample_kernel(lam_ref, gs_ref, q_ref, pk_ref, pv_ref, k_ref, v_ref, o_ref, *, tq, past, lam_init):
    def visible(shape, k_off):
        r = lax.broadcasted_iota(jnp.int32, shape, 0)
        c = lax.broadcasted_iota(jnp.int32, shape, 1) + k_off
        r = jnp.where(r >= tq, r - tq, r) + past
        return (r // CHUNK) >= (c // CHUNK)

    for h in range(HEADS):
        sl = slice(h * HEAD_W, (h + 1) * HEAD_W)
        head_rows = pl.ds(h, past, stride=HEADS)
        qz = _stack_q(q_ref[0, :, sl])
        s_old = _dot_nt(qz, pk_ref[0, head_rows, :].astype(BF16))
        s_new = _dot_nt(qz, k_ref[0, :, sl])
        s_old = jnp.where(visible(s_old.shape, 0), s_old, NEG_INF)
        s_new = jnp.where(visible(s_new.shape, past), s_new, NEG_INF)
        m = jnp.maximum(jnp.max(s_old, axis=-1, keepdims=True), jnp.max(s_new, axis=-1, keepdims=True))
        p_old = jnp.exp(s_old - m)
        p_new = jnp.exp(s_new - m)
        l = jnp.sum(p_old, axis=-1, keepdims=True) + jnp.sum(p_new, axis=-1, keepdims=True)
        acc = (_dot(p_old.astype(BF16), pv_ref[0, head_rows, :].astype(BF16))
               + _dot(p_new.astype(BF16), v_ref[0, :, sl]))
        o_ref[0, :, sl] = _da_finish(acc, l, _lam(lam_ref, lam_init), gs_ref[...], lam_init, tq)


def _da_sample(lam_par, g_subln, q, past_k, past_v, k, v, lam_init):
    B, T, _ = q.shape
    P = past_k.shape[1] // HEADS
    new_spec = pl.BlockSpec((1, T, MIX_HALF), lambda b: (b, 0, 0))
    past_spec = pl.BlockSpec((1, P * HEADS, HEAD_W), lambda b: (b, 0, 0))
    full = lambda a: pl.BlockSpec(a.shape, lambda b: (0,) * a.ndim)
    return pl.pallas_call(
        functools.partial(_da_sample_kernel, tq=T, past=P, lam_init=lam_init),
        grid=(B,),
        in_specs=[full(lam_par), full(g_subln), new_spec, past_spec, past_spec, new_spec, new_spec],
        out_specs=new_spec,
        out_shape=jax.ShapeDtypeStruct((B, T, MIX_HALF), BF16),
        compiler_params=_params(("parallel",)),
        name="diffattn_sample",
    )(lam_par, g_subln, q, past_k, past_v, k, v)


def _gdn_prep_kernel(q_ref, k_ref, v_ref, gate_ref, u_ref, w_ref, qd_ref, kt_ref, ai_ref, eg_ref, *, C, G):
    HC = HEADS * C
    row = lax.broadcasted_iota(jnp.int32, (C, C), 0)
    col = lax.broadcasted_iota(jnp.int32, (C, C), 1)
    causal = row >= col
    strict = row > col
    arow = lax.broadcasted_iota(jnp.int32, (C, HC), 0)
    acol = lax.broadcasted_iota(jnp.int32, (C, HC), 1)
    eye_all = (arow == (acol % C)).astype(F32)
    brow = lax.broadcasted_iota(jnp.int32, (HC, HC), 0)
    bcolm = lax.broadcasted_iota(jnp.int32, (HC, HC), 1)
    same_head = (brow // C) == (bcolm // C)
    prow = lax.broadcasted_iota(jnp.int32, (LANES, C), 0)
    pcol = lax.broadcasted_iota(jnp.int32, (LANES, C), 1)
    tri = jnp.where((prow >= pcol) & (prow < C), 1.0, 0.0).astype(BF16)

    def cumsum_rows(x):
        hi = x.astype(BF16)
        mid = (x - hi.astype(F32)).astype(BF16)
        lo = (x - hi.astype(F32) - mid.astype(F32)).astype(BF16)
        return _dot(tri, hi) + (_dot(tri, mid) + _dot(tri, lo))

    def block_diag(m_all):
        return jnp.where(same_head, jnp.concatenate([m_all] * HEADS, axis=0), jnp.zeros((), m_all.dtype))

    gcs, gcts, ps, xs = [], [], [], []
    for g in range(G):
        rs = slice(g * C, (g + 1) * C)
        gates = gate_ref[0, rs, :]
        gc_pad = cumsum_rows(gates)
        gcts.append(gc_pad.T)
        gcs.append(gc_pad[0:C, :])
        eg_ref[0, g * SUBLANES:(g + 1) * SUBLANES, :] = jnp.broadcast_to(
            jnp.exp(gc_pad[C - 1:C, :]), (SUBLANES, LANES))

    def decay_of(g, h):
        return jnp.exp(jnp.where(causal, gcs[g][:, h:h + 1] - gcts[g][h:h + 1, 0:C], NEG_INF))

    for g in range(G):
        rs = slice(g * C, (g + 1) * C)
        a_heads = []
        for h in range(HEADS):
            k = k_ref[0, rs, h * HEAD_W:(h + 1) * HEAD_W]
            kb = k * gate_ref[0, rs, HEADS + h:HEADS + h + 1]
            q = q_ref[0, rs, h * HEAD_W:(h + 1) * HEAD_W]
            both = _dot_nt(jnp.concatenate([kb.astype(BF16), q.astype(BF16)], axis=0), k.astype(BF16))
            decay = decay_of(g, h)
            a_heads.append(jnp.where(strict, -(both[0:C] * decay), 0.0))
            ai_ref[0, rs, h * C:(h + 1) * C] = (both[C:2 * C] * decay).astype(BF16)
        ps.append(jnp.concatenate(a_heads, axis=1))
        xs.append(eye_all + ps[g])
    def times(lefts, right_diag):
        bh, bl = right_diag
        top = _dot(jnp.concatenate([part for split in lefts for part in split], axis=0), bh)
        bot = _dot(jnp.concatenate([split[0] for split in lefts], axis=0), bl)
        return [top[2 * n * C:(2 * n + 1) * C] + top[(2 * n + 1) * C:(2 * n + 2) * C] + bot[n * C:(n + 1) * C]
                for n in range(len(lefts))]

    n_factors = int(math.log2(C))
    p_split = [_split_bf16(ps[g]) for g in range(G)]
    for g in range(G):
        p_split[g] = _split_bf16(times([p_split[g]], tuple(block_diag(part) for part in p_split[g]))[0])
    for k in range(1, n_factors):
        for g in range(G):
            diag = tuple(block_diag(part) for part in p_split[g])
            if k < n_factors - 1:
                x_prod, p_next = times([_split_bf16(xs[g]), p_split[g]], diag)
                p_split[g] = _split_bf16(p_next)
            else:
                x_prod, = times([_split_bf16(xs[g])], diag)
            xs[g] = xs[g] + x_prod

    for g in range(G):
        rs = slice(g * C, (g + 1) * C)
        for h in range(HEADS):
            sl = slice(h * HEAD_W, (h + 1) * HEAD_W)
            q = q_ref[0, rs, sl]
            v = v_ref[0, rs, sl]
            k = k_ref[0, rs, sl]
            gcol = gcs[g][:, h:h + 1]
            bcol = gate_ref[0, rs, HEADS + h:HEADS + h + 1]
            glast = gcs[g][C - 1:C, h:h + 1]
            egc = jnp.exp(gcol)
            t16 = xs[g][:, h * C:(h + 1) * C].astype(BF16)
            uw = _dot(t16, jnp.concatenate([v * bcol, k * bcol * egc], axis=1).astype(BF16))
            u_ref[0, rs, sl] = uw[:, 0:HEAD_W]
            w_ref[0, rs, sl] = uw[:, HEAD_W:2 * HEAD_W].astype(BF16)
            qd_ref[0, rs, sl] = (q * egc).astype(BF16)
            kt_ref[0, rs, sl] = (k * jnp.exp(glast - gcol)).astype(BF16)


def _gdn_scan_kernel(u_ref, w_ref, qd_ref, kt_ref, ai_ref, eg_ref, gz_ref, s0_ref, gout_ref, o_ref, sfin_ref, s_scr,
                     *, C, GB, n_chunks):
    c = pl.program_id(1)

    @pl.when(c == 0)
    def _():
        s_scr[...] = s0_ref[...]

    chains = [(b, h) for b in range(GB) for h in range(HEADS)]
    cols = lambda h: slice(h * HEAD_W, (h + 1) * HEAD_W)
    s16 = {bh: s_scr[bh[0], bh[1]].astype(BF16) for bh in chains}
    v_new = {(b, h): (u_ref[b, :, cols(h)] - _dot(w_ref[b, :, cols(h)], s16[b, h])).astype(BF16) for b, h in chains}
    o = {(b, h): _dot(qd_ref[b, :, cols(h)], s16[b, h]) + _dot(ai_ref[b, :, h * C:(h + 1) * C], v_new[b, h])
         for b, h in chains}
    for b, h in chains:
        s_scr[b, h] = s_scr[b, h] * eg_ref[b, 0:1, h:h + 1] + lax.dot_general(
            kt_ref[b, :, cols(h)], v_new[b, h], (((0,), (0,)), ((), ())), preferred_element_type=F32)
    for b, h in chains:
        o_ref[b, :, cols(h)] = (_rms(o[b, h], gout_ref[...]) * _silu(gz_ref[b, :, cols(h)])).astype(BF16)

    @pl.when(c == n_chunks - 1)
    def _():
        sfin_ref[...] = s_scr[...]


def _gdn(q, k, v, gz, gates, s0, g_out, C):
    B, T, _ = q.shape
    nc = T // C
    G = min(nc, 8)
    GB = next(n for n in (8, 4, 2, 1) if B % n == 0)
    HC = HEADS * C
    pblk = lambda w: pl.BlockSpec((1, G * C, w), lambda b, c: (b, c, 0))
    row = lambda w, dt: jax.ShapeDtypeStruct((B, T, w), dt)
    u, w, qd, kt, ai, eg = pl.pallas_call(
        functools.partial(_gdn_prep_kernel, C=C, G=G),
        grid=(B, nc // G),
        in_specs=[pblk(MIX_HALF)] * 3 + [pblk(LANES)],
        out_specs=[pblk(MIX_HALF)] * 4 + [pblk(HC), pl.BlockSpec((1, G * SUBLANES, LANES), lambda b, c: (b, c, 0))],
        out_shape=[row(MIX_HALF, F32), row(MIX_HALF, BF16), row(MIX_HALF, BF16), row(MIX_HALF, BF16), row(HC, BF16),
                   jax.ShapeDtypeStruct((B, nc * SUBLANES, LANES), F32)],
        compiler_params=_params(("parallel", "parallel")),
        name="gdn_prep",
    )(q, k, v, gates)
    sblk = lambda w: pl.BlockSpec((GB, C, w), lambda b, c: (b, c, 0))
    st = pl.BlockSpec((GB, HEADS, HEAD_W, HEAD_W), lambda b, c: (b, 0, 0, 0))
    return pl.pallas_call(
        functools.partial(_gdn_scan_kernel, C=C, GB=GB, n_chunks=nc),
        grid=(B // GB, nc),
        in_specs=[sblk(MIX_HALF)] * 4 + [sblk(HC), pl.BlockSpec((GB, SUBLANES, LANES), lambda b, c: (b, c, 0)),
                                         sblk(MIX_HALF), st, pl.BlockSpec(g_out.shape, lambda b, c: (0, 0))],
        out_specs=[sblk(MIX_HALF), st],
        out_shape=[jax.ShapeDtypeStruct((B, T, MIX_HALF), BF16),
                   jax.ShapeDtypeStruct((B, HEADS, HEAD_W, HEAD_W), F32)],
        scratch_shapes=[pltpu.VMEM((GB, HEADS, HEAD_W, HEAD_W), F32)],
        compiler_params=_params(("parallel", "arbitrary")),
        name="gdn_scan",
    )(u, w, qd, kt, ai, eg, gz, s0, g_out)


def _memkv_kernel(m_ref, g_ref, wk_ref, wv_ref, kf_ref, kb_ref, vf_ref, vb_ref):
    m = _rms(m_ref[...], g_ref[...]).astype(BF16)
    mk = _dot(m, wk_ref[...])
    mv = _dot(m, wv_ref[...])
    kf_ref[...] = mk
    kb_ref[...] = mk.astype(BF16)
    vf_ref[...] = mv
    vb_ref[...] = mv.astype(BF16)


def _memkv(mem, g_mem, w_ck, w_cv, tm):
    N, D = mem.shape
    blk = pl.BlockSpec((tm, D), lambda i: (i, 0))
    full = lambda a: pl.BlockSpec(a.shape, lambda i: (0,) * a.ndim)
    return pl.pallas_call(
        _memkv_kernel,
        grid=(N // tm,),
        in_specs=[blk, full(g_mem), full(w_ck), full(w_cv)],
        out_specs=[blk] * 4,
        out_shape=[jax.ShapeDtypeStruct((N, D), F32), jax.ShapeDtypeStruct((N, D), BF16)] * 2,
        compiler_params=_params(("parallel",)),
        name="memkv",
    )(mem, g_mem, w_ck, w_cv)


def _mid_kernel(x_ref, da_ref, gd_ref, wo_ref, gc_ref, wq_ref, mk_ref, mv_ref, wco_ref, gf_ref, wr_ref, br_ref,
                x2_ref, hf_ref, comb_ref):
    x1 = x_ref[0] + _dot(da_ref[0], wo_ref[0:MIX_HALF, :]) + _dot(gd_ref[0], wo_ref[MIX_HALF:2 * MIX_HALF, :])
    hc = _rms(x1, gc_ref[...]).astype(BF16)
    q = (_dot(hc, wq_ref[...]) * (X_DH ** -0.5)).astype(BF16)
    heads = []
    for h in range(X_HEADS):
        sl = slice(h * X_DH, (h + 1) * X_DH)
        s = _dot_nt(q[:, sl], mk_ref[0, :, sl])
        p = jnp.exp(s - jnp.max(s, axis=-1, keepdims=True))
        l = jnp.sum(p, axis=-1, keepdims=True)
        heads.append((_dot(p.astype(BF16), mv_ref[0, :, sl]) / l).astype(BF16))
    x2 = x1 + _dot(jnp.concatenate(heads, axis=1), wco_ref[...])
    x2_ref[0] = x2
    hf = _rms(x2, gf_ref[...])
    hf_ref[0] = hf.astype(BF16)

    logits = _dot3(hf, wr_ref[...]) + br_ref[...]
    lane_i = lax.broadcasted_iota(jnp.int32, logits.shape, 1)
    lane = lane_i.astype(F32)
    big = float(LANES)
    is_grp = (lane_i >= N_EXPERTS) & (lane_i < N_EXPERTS + N_GROUPS)
    gl = jnp.where(is_grp, logits, NEG_INF)
    gmax = jnp.max(gl, axis=-1, keepdims=True)
    g_top = 1.0 / jnp.sum(jnp.exp(gl - gmax), axis=-1, keepdims=True)
    g_idx = jnp.min(jnp.where(gl == gmax, lane, big), axis=-1, keepdims=True) - float(N_EXPERTS)
    in_grp = (lane_i < N_EXPERTS) & (jnp.floor(lane * (1.0 / E_PER_GROUP)) == g_idx)
    el = jnp.where(in_grp, logits, NEG_INF)
    e1 = jnp.max(el, axis=-1, keepdims=True)
    i1 = jnp.min(jnp.where(el == e1, lane, big), axis=-1, keepdims=True)
    el2 = jnp.where(lane == i1, NEG_INF, el)
    e2 = jnp.max(el2, axis=-1, keepdims=True)
    i2 = jnp.min(jnp.where(el2 == e2, lane, big), axis=-1, keepdims=True)
    r = jnp.exp(e2 - e1)
    w1 = g_top / (1.0 + r)
    w2 = g_top * r / (1.0 + r)
    comb_ref[0] = jnp.where(lane_i == 0, i1, jnp.where(lane_i == 1, i2, jnp.where(
        lane_i == 2, w1, jnp.where(lane_i == 3, w2, 0.0))))


def _mid(x, mix_da, mix_gd, w_out, g_cross, w_cq, mk, mv, w_co, g_ffn, w_router, b_router, tm):
    B, T, D = x.shape
    M = mk.shape[1]
    blk = lambda w: pl.BlockSpec((1, tm, w), lambda b, t: (b, t, 0))
    mem = pl.BlockSpec((1, M, D), lambda b, t: (b, 0, 0))
    full = lambda a: pl.BlockSpec(a.shape, lambda b, t: (0,) * a.ndim)
    return pl.pallas_call(
        _mid_kernel,
        grid=(B, T // tm),
        in_specs=[blk(D), blk(MIX_HALF), blk(MIX_HALF), full(w_out), full(g_cross), full(w_cq), mem, mem,
                  full(w_co), full(g_ffn), full(w_router), full(b_router)],
        out_specs=[blk(D), blk(D), blk(LANES)],
        out_shape=[jax.ShapeDtypeStruct((B, T, D), F32), jax.ShapeDtypeStruct((B, T, D), BF16),
                   jax.ShapeDtypeStruct((B, T, LANES), F32)],
        compiler_params=_params(("parallel", "parallel")),
        name="mid",
    )(x, mix_da, mix_gd, w_out, g_cross, w_cq, mk, mv, w_co, g_ffn, w_router, b_router)


MOE_BLK = 16
EXP_TILE_BLKS = 16


def _slots_per_tile(tm):
    worst = 2 * tm + N_EXPERTS * (MOE_BLK - 1)
    return -(-worst // LANES) * LANES


def _route_kernel(hf_ref, rec_ref, xs_ref, slot_ref, meta_ref, *, tm, sl, n_tiles):
    @pl.when(pl.program_id(0) == n_tiles)
    def _():
        xs_ref[...] = jnp.zeros(xs_ref.shape, BF16)
        slot_ref[...] = jnp.zeros(slot_ref.shape, F32)
        meta_ref[...] = jnp.zeros(meta_ref.shape, F32)

    @pl.when(pl.program_id(0) < n_tiles)
    def _():
        _route_tile(hf_ref, rec_ref, xs_ref, slot_ref, meta_ref, tm, sl)


def _route_tile(hf_ref, rec_ref, xs_ref, slot_ref, meta_ref, tm, sl):
    rec = rec_ref[...]
    lane_i = lax.broadcasted_iota(jnp.int32, rec.shape, 1)
    lane = lane_i.astype(F32)
    i1, i2, w1, w2 = rec[:, 0:1], rec[:, 1:2], rec[:, 2:3], rec[:, 3:4]
    oh1 = jnp.where(lane == i1, 1.0, 0.0)
    oh2 = jnp.where(lane == i2, 1.0, 0.0)
    oh = oh1 + oh2
    nblk = jnp.floor((jnp.sum(oh, axis=0, keepdims=True) + (MOE_BLK - 1.0)) * (1.0 / MOE_BLK))
    ur = lax.broadcasted_iota(jnp.int32, (LANES, LANES), 0)
    uc = lax.broadcasted_iota(jnp.int32, (LANES, LANES), 1)
    start = MOE_BLK * _dot(jnp.broadcast_to(nblk, (2 * SUBLANES, LANES)).astype(BF16),
                           jnp.where(ur < uc, 1.0, 0.0).astype(BF16))[0:1]
    r = lax.broadcasted_iota(jnp.int32, (tm, tm), 0)
    c = lax.broadcasted_iota(jnp.int32, (tm, tm), 1)
    pos = start + _dot(jnp.where(r > c, 1.0, 0.0).astype(BF16), oh.astype(BF16))
    slot1 = jnp.sum(oh1 * pos, axis=-1, keepdims=True)
    slot2 = jnp.sum(oh2 * pos, axis=-1, keepdims=True)
    slot_ref[...] = jnp.where(lane_i == 0, slot1, jnp.where(lane_i == 1, slot2, 0.0))
    meta_ref[0] = jnp.broadcast_to(nblk, (SUBLANES, LANES))

    srow = lax.broadcasted_iota(jnp.int32, (sl, tm), 0).astype(F32)
    perm = (jnp.where(srow == jnp.broadcast_to(slot1, (tm, LANES)).T[0:1, :], 1.0, 0.0)
            + jnp.where(srow == jnp.broadcast_to(slot2, (tm, LANES)).T[0:1, :], 2.0, 0.0)).astype(BF16)

    def parts(w, base):
        hi = w.astype(BF16).astype(F32)
        mid = (w - hi).astype(BF16).astype(F32)
        lo = w - hi - mid
        return jnp.where(lane_i == base, hi, jnp.where(lane_i == base + 1, mid,
                                                       jnp.where(lane_i == base + 2, lo, 0.0)))

    extra = parts(w1, 0) + parts(w2, 3) + jnp.where(lane_i == 6, 1.0, 0.0)
    moved = _dot(perm, jnp.concatenate([hf_ref[...], extra.astype(BF16)], axis=1))
    d = hf_ref.shape[1]
    second = moved[:, d + 6:d + 7] == 2.0
    moved = moved * jnp.where(second, 0.5, 1.0)
    lane_s = lax.broadcasted_iota(jnp.int32, (sl, LANES), 1)
    xs_ref[:, 0:d] = moved[:, 0:d].astype(BF16)
    xs_ref[:, d:d + LANES] = jnp.where(lane_s == 7, jnp.where(second, 1.0, 0.0), moved[:, d:d + LANES]).astype(BF16)


def _route(hf, rec, tm):
    N, D = hf.shape
    nt = N // tm
    sl = _slots_per_tile(tm)
    in_blk = lambda w: pl.BlockSpec((tm, w), lambda i: (jnp.minimum(i, nt - 1), 0))
    blk = lambda rows, w: pl.BlockSpec((rows, w), lambda i: (i, 0))
    return pl.pallas_call(
        functools.partial(_route_kernel, tm=tm, sl=sl, n_tiles=nt),
        grid=(nt + 1,),
        in_specs=[in_blk(D), in_blk(LANES)],
        out_specs=[blk(sl, D + LANES), blk(tm, LANES), pl.BlockSpec((1, SUBLANES, LANES), lambda i: (i, 0, 0))],
        out_shape=[jax.ShapeDtypeStruct(((nt + 1) * sl, D + LANES), BF16),
                   jax.ShapeDtypeStruct(((nt + 1) * tm, LANES), F32),
                   jax.ShapeDtypeStruct((nt + 1, SUBLANES, LANES), F32)],
        compiler_params=_params(("parallel",)),
        name="moe_route",
    )(hf, rec)


def _moe_tables(meta, blocks_per_tile, max_tiles):
    nblk = meta[:, 0, :N_EXPERTS].astype(jnp.int32)
    nt = nblk.shape[0]
    tiles_e = (jnp.sum(nblk, axis=0) + EXP_TILE_BLKS - 1) // EXP_TILE_BLKS
    tile_end = jnp.cumsum(tiles_e)
    first_blk = (tile_end - tiles_e) * EXP_TILE_BLKS
    off = first_blk[None, :] + jnp.cumsum(nblk, axis=0) - nblk
    lend = jnp.cumsum(nblk, axis=1)
    lstart = lend - nblk
    lb = jnp.arange(blocks_per_tile, dtype=jnp.int32)[None, :]
    in_run = (lstart[:, None, :] <= lb[:, :, None]) & (lb[:, :, None] < lend[:, None, :])
    valid = lb < lend[:, -1:]
    dest = jnp.sum(jnp.where(in_run, (off - lstart)[:, None, :], 0), axis=-1) + lb
    flat = jnp.arange(nt, dtype=jnp.int32)[:, None] * blocks_per_tile + lb
    n_sorted = max_tiles * EXP_TILE_BLKS
    src = jnp.full((n_sorted,), -1, jnp.int32).at[jnp.where(valid, dest, n_sorted).reshape(-1)].set(
        flat.reshape(-1), mode="drop")
    g = jnp.arange(max_tiles, dtype=jnp.int32)
    tile_expert = jnp.minimum(jnp.sum(tile_end[None, :] <= g[:, None], axis=-1), N_EXPERTS - 1).astype(jnp.int32)
    return src, tile_expert, tile_end[-1:].astype(jnp.int32)


def _experts_kernel(src_ref, te_ref, nt_ref, xs_hbm, wg_ref, wu_ref, wd_ref, ys_hbm, xbuf, obuf, gsem, ssem, *,
                    zero_blk, spare_blk):
    g = pl.program_id(0)
    n_tiles = nt_ref[0]
    nb = EXP_TILE_BLKS
    d = obuf.shape[-1]

    def gather_copy(tile, slot, i):
        s = src_ref[tile * nb + i]
        return pltpu.make_async_copy(xs_hbm.at[jnp.where(s >= 0, s, zero_blk)], xbuf.at[slot, i], gsem.at[slot])

    def scatter_copy(tile, slot, i):
        s = src_ref[tile * nb + i]
        dst = jnp.where(s >= 0, s, spare_blk + slot * nb + i)
        return pltpu.make_async_copy(obuf.at[slot, i], ys_hbm.at[dst, :, pl.ds(0, d)], ssem.at[slot])

    def gather_start(tile, slot):
        for i in range(nb):
            gather_copy(tile, slot, i).start()

    def gather_wait(tile, slot):
        for i in range(nb):
            gather_copy(tile, slot, i).wait()

    def scatter_start(tile, slot):
        for i in range(nb):
            scatter_copy(tile, slot, i).start()

    def scatter_wait(tile, slot):
        for i in range(nb):
            scatter_copy(tile, slot, i).wait()

    slot = g % 2
    xslot = g % 3

    @pl.when(g == 0)
    def _():
        gather_start(0, 0)

        @pl.when(n_tiles > 1)
        def _():
            gather_start(1, 1)

    @pl.when(g < n_tiles)
    def _():
        @pl.when(g + 2 < n_tiles)
        def _():
            gather_start(g + 2, (g + 2) % 3)

        gather_wait(g, xslot)
        rows = nb * MOE_BLK
        xe = xbuf[xslot].reshape(rows, xbuf.shape[-1])
        x = xe[:, 0:d]
        extra = xe[:, d:d + LANES].astype(F32)
        lane = lax.broadcasted_iota(jnp.int32, extra.shape, 1)
        first_lane = 3.0 * jnp.sum(jnp.where(lane == 7, extra, 0.0), axis=-1, keepdims=True)
        lane_f = lane.astype(F32)
        mine = (lane_f >= first_lane) & (lane_f < first_lane + 3.0)
        ccol = jnp.sum(jnp.where(mine, extra, 0.0), axis=-1, keepdims=True)
        hg = _silu(_dot(x, wg_ref[0])) * _dot(x, wu_ref[0])
        y = _dot((hg * ccol).astype(BF16), wd_ref[0]).astype(BF16)

        @pl.when(g >= 2)
        def _():
            scatter_wait(g - 2, slot)

        obuf[slot] = y.reshape(nb, MOE_BLK, d)
        scatter_start(g, slot)

    @pl.when(g == n_tiles - 1)
    def _():
        scatter_wait(g, slot)

        @pl.when(g >= 1)
        def _():
            scatter_wait(g - 1, 1 - slot)


def _experts(src, tile_expert, n_tiles, xs, w_gate, w_up, w_down, max_tiles, zero_blk):
    n_rows, width = xs.shape
    D = width - LANES
    slab = xs.reshape(n_rows // MOE_BLK, MOE_BLK, width)
    wspec = lambda shape: pl.BlockSpec((1,) + shape, lambda g, src, te, nt: (te[g], 0, 0))
    grid_spec = pltpu.PrefetchScalarGridSpec(
        num_scalar_prefetch=3,
        grid=(max_tiles,),
        in_specs=[pl.BlockSpec(memory_space=pl.ANY), wspec((D, D_EXPERT)), wspec((D, D_EXPERT)), wspec((D_EXPERT, D))],
        out_specs=pl.BlockSpec(memory_space=pl.ANY),
        scratch_shapes=[pltpu.VMEM((3, EXP_TILE_BLKS, MOE_BLK, width), BF16),
                        pltpu.VMEM((2, EXP_TILE_BLKS, MOE_BLK, D), BF16),
                        pltpu.SemaphoreType.DMA((3,)), pltpu.SemaphoreType.DMA((2,))],
    )
    return pl.pallas_call(
        functools.partial(_experts_kernel, zero_blk=zero_blk, spare_blk=zero_blk + 1),
        grid_spec=grid_spec,
        out_shape=jax.ShapeDtypeStruct(slab.shape, BF16),
        input_output_aliases={3: 0},
        compiler_params=_params(("arbitrary",)),
        name="moe_experts",
    )(src, tile_expert, n_tiles, slab, w_gate, w_up, w_down).reshape(n_rows, width)


def _combine_kernel(x2_ref, slot_ref, ys_ref, gfin_ref, y_ref, *, tm, sl):
    slots = slot_ref[...]
    col = lax.broadcasted_iota(jnp.int32, (tm, sl), 1).astype(F32)
    pick = (jnp.where(col == slots[:, 0:1], 1.0, 0.0) + jnp.where(col == slots[:, 1:2], 1.0, 0.0)).astype(BF16)
    y_ref[...] = _rms(x2_ref[...] + _dot(pick, ys_ref[:, 0:x2_ref.shape[1]]), gfin_ref[...])


def _combine(x2, slots, ys, g_final, tm, sl):
    N, D = x2.shape
    blk = lambda rows, w: pl.BlockSpec((rows, w), lambda i: (i, 0))
    return pl.pallas_call(
        functools.partial(_combine_kernel, tm=tm, sl=sl),
        grid=(N // tm,),
        in_specs=[blk(tm, D), blk(tm, LANES), blk(sl, D), pl.BlockSpec(g_final.shape, lambda i: (0, 0))],
        out_specs=blk(tm, D),
        out_shape=jax.ShapeDtypeStruct((N, D), F32),
        compiler_params=_params(("parallel",)),
        name="moe_combine",
    )(x2, slots, ys, g_final)


def _moe(x2, hf, rec, w_gate, w_up, w_down, g_final, tm):
    N, _ = x2.shape
    nt = N // tm
    sl = _slots_per_tile(tm)
    xs, slots, meta = _route(hf, rec, tm)
    blocks_per_tile = sl // MOE_BLK
    max_tiles = nt * blocks_per_tile // EXP_TILE_BLKS + N_EXPERTS
    src, tile_expert, n_tiles = _moe_tables(meta[:nt], blocks_per_tile, max_tiles)
    ys = _experts(src, tile_expert, n_tiles, xs, w_gate, w_up, w_down, max_tiles, nt * blocks_per_tile)
    return _combine(x2, slots, ys, g_final, tm, sl)


def _pad_rows(a, rows):
    return jnp.pad(a, ((0, rows - a.shape[0]), (0, LANES - a.shape[1])))


def _trunk(x, mk16, mv16, past_k, past_v, conv_prev, s_prev, lam_init, wts):
    B, T, D = x.shape
    tm = min(T, 512)
    if conv_prev is None:
        conv_prev = jnp.zeros((B, CONV_W - 1, GDN_QKV), F32)
    if s_prev is None:
        s_prev = jnp.zeros((B, HEADS, HEAD_W, HEAD_W), F32)
    prev8 = jnp.pad(conv_prev, ((0, 0), (SUBLANES - (CONV_W - 1), 0), (0, 0)))
    (dq, dkf, dk16, dvf, dv16, cq, ck, cv, gz, gates, cnew) = _inproj(
        x, prev8, wts["g_mix"], wts["w_main"], wts["w_gate8"], wts["conv_w8"], wts["gate_par"], tm)
    if past_k is None:
        mix_da = _da_prompt(wts["lam_par"], wts["g_subln"], dq, dk16, dv16, min(T, 512), lam_init)
    else:
        P = past_k.shape[1]
        mix_da = _da_sample(wts["lam_par"], wts["g_subln"], dq, past_k.reshape(B, P * HEADS, HEAD_W),
                            past_v.reshape(B, P * HEADS, HEAD_W), dk16, dv16, lam_init)
    C = min(T, CHUNK)
    mix_gd, s_new = _gdn(cq, ck, cv, gz, gates, s_prev, wts["g_gdn_out"], C)
    x2, hf, comb = _mid(x, mix_da, mix_gd, wts["w_out"], wts["g_cross"], wts["w_cq"], mk16, mv16, wts["w_co"],
                        wts["g_ffn"], wts["w_router"], wts["b_router"], min(T, 512))
    N = B * T
    y = _moe(x2.reshape(N, D), hf.reshape(N, D), comb.reshape(N, LANES), wts["w_gate"], wts["w_up"],
             wts["w_down"], wts["g_final"], min(N, 512))
    new_k = dkf.reshape(1, B, T, HEADS, HEAD_W)
    new_v = dvf.reshape(1, B, T, HEADS, HEAD_W)
    new_conv = cnew[:, SUBLANES - (CONV_W - 1):, :].reshape(1, B, CONV_W - 1, GDN_QKV)
    return y.reshape(B, T, D), new_k, new_v, new_conv, s_new[None]


def kernel(x_prompt, x_sample, cache_diff_k, cache_diff_v, state_gdn_conv, state_gdn, cache_mem_k, cache_mem_v,
           mem_prompt, g_mix, w_in, lam_q1, lam_k1, lam_q2, lam_k2, g_subln, conv_w, a_log, dt_bias, g_gdn_out,
           w_out, g_cross, g_mem, w_ck, w_cv, w_cq, w_co, g_ffn, w_grp, b_grp, w_rt, b_rt, w_gate, w_up, w_down,
           g_final):
    assert g_mix.shape[0] == 1, "single-layer model"
    l = 0
    lam_init = 0.8 - 0.6 * math.exp(-0.3 * l)
    gate_lo = 3 * MIX_HALF + GDN_QKV
    gate_hi = gate_lo + 2 * HEADS
    wi = w_in[l]
    row = lambda a: a.reshape(1, -1).astype(F32)
    wts = dict(
        g_mix=row(g_mix[l]),
        w_main=jnp.concatenate([wi[:, :gate_lo], wi[:, gate_hi:]], axis=1).astype(BF16),
        w_gate8=jnp.pad(wi[:, gate_lo:gate_hi], ((0, 0), (0, LANES - 2 * HEADS))).astype(BF16),
        conv_w8=jnp.pad(conv_w[l], ((0, SUBLANES - CONV_W), (0, 0))),
        gate_par=_pad_rows(jnp.stack([a_log[l], dt_bias[l]]), SUBLANES),
        lam_par=_pad_rows(jnp.stack([lam_q1[l], lam_k1[l], lam_q2[l], lam_k2[l]]), SUBLANES),
        g_subln=row(g_subln[l]), g_gdn_out=row(g_gdn_out[l]),
        w_out=w_out[l].astype(BF16), g_cross=row(g_cross[l]), w_cq=w_cq[l].astype(BF16), w_co=w_co[l].astype(BF16),
        g_ffn=row(g_ffn[l]),
        w_router=jnp.pad(jnp.concatenate([w_rt[l], w_grp[l]], axis=1),
                         ((0, 0), (0, LANES - N_EXPERTS - N_GROUPS))),
        b_router=jnp.pad(jnp.concatenate([b_rt[l], b_grp[l]]), (0, LANES - N_EXPERTS - N_GROUPS)).reshape(1, LANES),
        w_gate=w_gate[l].astype(BF16), w_up=w_up[l].astype(BF16), w_down=w_down[l].astype(BF16),
        g_final=row(g_final),
    )
    B, M, D = mem_prompt.shape
    mkf, mk16, mvf, mv16 = _memkv(mem_prompt.reshape(B * M, D), row(g_mem[l]), w_ck[l].astype(BF16),
                                  w_cv[l].astype(BF16), min(B * M, 512))
    yp, pk, pv, pc, ps = _trunk(x_prompt, mk16.reshape(B, M, D), mv16.reshape(B, M, D), None, None, None, None,
                                lam_init, wts)
    Bs = x_sample.shape[0]
    ys, sk, sv, sc, ss = _trunk(x_sample, cache_mem_k[l].reshape(Bs, M, D).astype(BF16),
                                cache_mem_v[l].reshape(Bs, M, D).astype(BF16), cache_diff_k[l], cache_diff_v[l],
                                state_gdn_conv[l], state_gdn[l], lam_init, wts)
    mem_shape = (1, B, M, X_HEADS, X_DH)
    return (yp, ys, pk, pv, pc, ps, mkf.reshape(mem_shape), mvf.reshape(mem_shape), sk, sv, sc, ss)
```

```python
import functools
import math

import jax
import jax.numpy as jnp
import numpy as np
from jax import lax
from jax.experimental import pallas as pl
from jax.experimental.pallas import tpu as pltpu

F32 = jnp.float32
BF16 = jnp.bfloat16

D_MODEL = 1024
CHUNK = 64
HEADS = 4
HEAD_W = 128
DA_DH = 64
MIX_HALF = HEADS * HEAD_W
GDN_QKV = 3 * MIX_HALF
CONV_W = 4
X_HEADS = 4
X_DH = 256
N_GROUPS = 4
E_PER_GROUP = 8
N_EXPERTS = 32
D_EXPERT = 256
NORM_EPS = 1e-6
NEG_INF = -1e30
LANES = 128
SUBLANES = 8
VMEM_LIMIT = 56 * 1024 * 1024


def _dot(a, b):
    return jnp.dot(a, b, preferred_element_type=F32)


def _dot_nt(a, b):
    return lax.dot_general(a, b, (((1,), (1,)), ((), ())), preferred_element_type=F32)


def _rms(x, g):
    return x * lax.rsqrt(jnp.mean(x * x, axis=-1, keepdims=True) + NORM_EPS) * g


def _sigmoid(x):
    return 1.0 / (1.0 + jnp.exp(-x))


def _silu(x):
    return x * _sigmoid(x)


def _split_bf16(a):
    hi = a.astype(BF16)
    lo = (a - hi.astype(F32)).astype(BF16)
    return hi, lo


def _dot3(a, b):
    ah, al = _split_bf16(a)
    bh, bl = _split_bf16(b)
    return _dot(ah, bh) + (_dot(ah, bl) + _dot(al, bh))


def _params(sem):
    return pltpu.CompilerParams(dimension_semantics=sem, vmem_limit_bytes=VMEM_LIMIT)


def _inproj_kernel(x_ref, prev_ref, gmix_ref, wm_ref, wg_ref, cw_ref, gp_ref,
                   dq_ref, dkf_ref, dkb_ref, dvf_ref, dvb_ref, cq_ref, ck_ref, cv_ref, gz_ref, gate_ref,
                   cnew_ref, xp_scr, *, tm):
    t = pl.program_id(1)
    h = _rms(x_ref[0], gmix_ref[...]).astype(BF16)

    @pl.when(t == 0)
    def _():
        xp_scr[0:SUBLANES, :] = prev_ref[0]

    conv_col0 = 3 * MIX_HALF
    outs = (cq_ref, ck_ref, cv_ref)

    def conv_part(part):
        cs = slice(part * MIX_HALF, (part + 1) * MIX_HALF)
        xp_scr[SUBLANES:SUBLANES + tm, cs] = _dot(
            h, wm_ref[:, conv_col0 + part * MIX_HALF:conv_col0 + (part + 1) * MIX_HALF])
        y = xp_scr[5:5 + tm, cs] * cw_ref[0:1, cs]
        for j in range(1, CONV_W):
            y = y + xp_scr[5 + j:5 + j + tm, cs] * cw_ref[j:j + 1, cs]
        c = _silu(y)
        if part == 2:
            outs[part][0] = c
        else:
            scale = (HEAD_W ** -0.5) if part == 0 else 1.0
            for hh in range(HEADS):
                ch = c[:, hh * HEAD_W:(hh + 1) * HEAD_W]
                n = ch * lax.rsqrt(jnp.sum(ch * ch, axis=-1, keepdims=True) + NORM_EPS)
                outs[part][0, :, hh * HEAD_W:(hh + 1) * HEAD_W] = n * scale if part == 0 else n

    conv_part(0)
    dq = _dot(h, wm_ref[:, 0:MIX_HALF])
    dq_ref[0] = (dq * (DA_DH ** -0.5)).astype(BF16)
    conv_part(1)
    dk = _dot(h, wm_ref[:, MIX_HALF:2 * MIX_HALF])
    dkb_ref[0] = dk.astype(BF16)
    conv_part(2)
    dv = _dot(h, wm_ref[:, 2 * MIX_HALF:3 * MIX_HALF])
    dvb_ref[0] = dv.astype(BF16)
    for hh in range(HEADS):
        dkf_ref[0, pl.ds(hh, tm, stride=HEADS), :] = dk[:, hh * HEAD_W:(hh + 1) * HEAD_W]
        dvf_ref[0, pl.ds(hh, tm, stride=HEADS), :] = dv[:, hh * HEAD_W:(hh + 1) * HEAD_W]

    gates = _dot(h, wg_ref[...])
    z = gates + gp_ref[1:2, :]
    softplus = jnp.maximum(z, 0.0) + jnp.log1p(jnp.exp(-jnp.abs(z)))
    g_all = -jnp.exp(gp_ref[0:1, :]) * softplus
    lane = lax.broadcasted_iota(jnp.int32, gates.shape, 1)
    gate_ref[0] = jnp.where(lane < HEADS, g_all, jnp.where(lane < 2 * HEADS, _sigmoid(gates), 0.0))
    gz_ref[0] = _dot(h, wm_ref[:, 3 * MIX_HALF + GDN_QKV:4 * MIX_HALF + GDN_QKV])
    carry = xp_scr[tm:tm + SUBLANES, :]
    xp_scr[0:SUBLANES, :] = carry
    cnew_ref[0] = carry


def _inproj(x, prev8, g_mix, w_main, w_gate, conv_w8, gate_par, tm):
    B, T, D = x.shape
    nt = T // tm
    row = lambda w, dt: jax.ShapeDtypeStruct((B, T, w), dt)
    blk = lambda w: pl.BlockSpec((1, tm, w), lambda b, t: (b, t, 0))
    full = lambda a: pl.BlockSpec(a.shape, lambda b, t: (0,) * a.ndim)
    cache_blk = pl.BlockSpec((1, tm * HEADS, HEAD_W), lambda b, t: (b, t, 0))
    cache_rows = jax.ShapeDtypeStruct((B, T * HEADS, HEAD_W), F32)
    return pl.pallas_call(
        functools.partial(_inproj_kernel, tm=tm),
        grid=(B, nt),
        in_specs=[blk(D), pl.BlockSpec((1, SUBLANES, GDN_QKV), lambda b, t: (b, 0, 0)),
                  full(g_mix), full(w_main), full(w_gate), full(conv_w8), full(gate_par)],
        out_specs=[blk(MIX_HALF), cache_blk, blk(MIX_HALF), cache_blk] + [blk(MIX_HALF)] * 5
        + [blk(LANES), pl.BlockSpec((1, SUBLANES, GDN_QKV), lambda b, t: (b, 0, 0))],
        out_shape=[row(MIX_HALF, BF16), cache_rows, row(MIX_HALF, BF16), cache_rows,
                   row(MIX_HALF, BF16), row(MIX_HALF, F32), row(MIX_HALF, F32), row(MIX_HALF, F32),
                   row(MIX_HALF, F32), row(LANES, F32), jax.ShapeDtypeStruct((B, SUBLANES, GDN_QKV), F32)],
        scratch_shapes=[pltpu.VMEM((tm + SUBLANES, GDN_QKV), F32)],
        compiler_params=_params(("parallel", "arbitrary")),
        name="inproj",
    )(x, prev8, g_mix, w_main, w_gate, conv_w8, gate_par)


def _lam(lam_ref, lam_init):
    s1 = jnp.sum(lam_ref[0:1, :] * lam_ref[1:2, :], axis=-1, keepdims=True)
    s2 = jnp.sum(lam_ref[2:3, :] * lam_ref[3:4, :], axis=-1, keepdims=True)
    return jnp.exp(s1) - jnp.exp(s2) + lam_init


def _stack_q(q):
    lane = lax.broadcasted_iota(jnp.int32, q.shape, 1)
    zero = jnp.zeros_like(q)
    return jnp.concatenate([jnp.where(lane < DA_DH, q, zero), jnp.where(lane >= DA_DH, q, zero)], axis=0)


def _da_finish(acc, l, lam, gs, lam_init, tq):
    o = acc[:tq] / l[:tq] - lam * (acc[tq:] / l[tq:])
    return (_rms(o, gs) * (1.0 - lam_init)).astype(BF16)


def _da_prompt_kernel(lam_ref, gs_ref, q_ref, k_ref, v_ref, o_ref, vx_scr, m_scr, acc_scr, p_scr, al_scr, *,
                      tq, rows, heads, lam_init):
    i = pl.program_id(2)
    cols = lambda hh: slice(hh * HEAD_W, (hh + 1) * HEAD_W)

    @pl.when(i == 0)
    def _():
        for hh in range(heads):
            vx_scr[hh, :, 0:HEAD_W] = v_ref[0, :, cols(hh)]
            vx_scr[hh, :, HEAD_W:2 * HEAD_W] = jnp.ones((vx_scr.shape[1], HEAD_W), BF16)
    qz = [_stack_q(q_ref[0, :, cols(hh)]) for hh in range(heads)]
    m_scr[...] = jnp.full(m_scr.shape, NEG_INF, F32)
    acc_scr[...] = jnp.zeros(acc_scr.shape, F32)

    def keys_seen(r0):
        return min(tq, -(-((r0 % tq) + rows) // LANES) * LANES)

    def scores(hh, j, r0, masked):
        kw = keys_seen(r0) if masked else tq
        kj = k_ref[0, pl.ds(pl.multiple_of(j * tq, tq), kw), cols(hh)]
        s = _dot_nt(qz[hh][r0:r0 + rows], kj)
        if masked:
            r = lax.broadcasted_iota(jnp.int32, s.shape, 0) + (r0 % tq)
            c = lax.broadcasted_iota(jnp.int32, s.shape, 1)
            s = jnp.where((r // CHUNK) >= (c // CHUNK), s, NEG_INF)
        m_prev = m_scr[hh, r0:r0 + rows, :]
        m_new = jnp.maximum(m_prev, jnp.max(s, axis=-1, keepdims=True))
        al_scr[hh, r0:r0 + rows, :] = jnp.exp(m_prev - m_new)
        for c in range(kw // LANES):
            cs = slice(c * LANES, (c + 1) * LANES)
            p_scr[hh, r0:r0 + rows, cs] = jnp.exp(s[:, cs] - m_new).astype(BF16)
        m_scr[hh, r0:r0 + rows, :] = m_new

    def accumulate(hh, j, r0, diagonal):
        kw = keys_seen(r0) if diagonal else tq
        vj = vx_scr[hh, pl.ds(pl.multiple_of(j * tq, tq), kw), :]
        alpha = al_scr[hh, r0:r0 + rows, :]
        acc_scr[hh, r0:r0 + rows, :] = (jnp.concatenate([alpha, alpha], axis=1) * acc_scr[hh, r0:r0 + rows, :]
                                        + _dot(p_scr[hh, r0:r0 + rows, 0:kw], vj))

    def stages(acc_j=None, score_j=None, masked=False, diagonal=False):
        for r0 in range(0, 2 * tq, rows):
            for hh in range(heads):
                if acc_j is not None:
                    accumulate(hh, acc_j, r0, diagonal)
                if score_j is not None:
                    scores(hh, score_j, r0, masked)

    @pl.when(i == 0)
    def _():
        stages(score_j=0, masked=True)

    @pl.when(i > 0)
    def _():
        stages(score_j=0)

        def body(t, carry):
            stages(acc_j=t - 1, score_j=t)
            return carry

        lax.fori_loop(1, i, body, 0)
        stages(acc_j=i - 1, score_j=i, masked=True)

    stages(acc_j=i, diagonal=True)
    lam = _lam(lam_ref, lam_init)
    for hh in range(heads):
        acc = acc_scr[hh]
        o = acc[:, 0:HEAD_W] / acc[:, HEAD_W:2 * HEAD_W]
        o = o[0:tq] - lam * o[tq:2 * tq]
        o_ref[0, :, cols(hh)] = (_rms(o, gs_ref[...]) * (1.0 - lam_init)).astype(BF16)


def _da_prompt(lam_par, g_subln, q, k, v, tq, lam_init):
    B, T, _ = q.shape
    hg = 4
    kv_spec = pl.BlockSpec((1, T, hg * HEAD_W), lambda b, h, i: (b, 0, h))
    q_spec = pl.BlockSpec((1, tq, hg * HEAD_W), lambda b, h, i: (b, i, h))
    full = lambda a: pl.BlockSpec(a.shape, lambda b, h, i: (0,) * a.ndim)
    return pl.pallas_call(
        functools.partial(_da_prompt_kernel, tq=tq, rows=min(2 * tq, 128), heads=hg, lam_init=lam_init),
        grid=(B, HEADS // hg, T // tq),
        in_specs=[full(lam_par), full(g_subln), q_spec, kv_spec, kv_spec],
        out_specs=q_spec,
        out_shape=jax.ShapeDtypeStruct((B, T, MIX_HALF), BF16),
        scratch_shapes=[pltpu.VMEM((hg, T, 2 * HEAD_W), BF16), pltpu.VMEM((hg, 2 * tq, LANES), F32),
                        pltpu.VMEM((hg, 2 * tq, 2 * HEAD_W), F32), pltpu.VMEM((hg, 2 * tq, tq), BF16),
                        pltpu.VMEM((hg, 2 * tq, LANES), F32)],
        compiler_params=_params(("parallel", "parallel", "arbitrary")),
        name="diffattn_prompt",
    )(lam_par, g_subln, q, k, v)


def _da_sample_kernel(lam_ref, gs_ref, q_ref, pk_ref, pv_ref, k_ref, v_ref, o_ref, *, tq, past, lam_init):
    def visible(shape, k_off):
        r = lax.broadcasted_iota(jnp.int32, shape, 0)
        c = lax.broadcasted_iota(jnp.int32, shape, 1) + k_off
        r = jnp.where(r >= tq, r - tq, r) + past
        return (r // CHUNK) >= (c // CHUNK)

    for h in range(HEADS):
        sl = slice(h * HEAD_W, (h + 1) * HEAD_W)
        head_rows = pl.ds(h, past, stride=HEADS)
        qz = _stack_q(q_ref[0, :, sl])
        s_old = _dot_nt(qz, pk_ref[0, head_rows, :].astype(BF16))
        s_new = _dot_nt(qz, k_ref[0, :, sl])
        s_old = jnp.where(visible(s_old.shape, 0), s_old, NEG_INF)
        s_new = jnp.where(visible(s_new.shape, past), s_new, NEG_INF)
        m = jnp.maximum(jnp.max(s_old, axis=-1, keepdims=True), jnp.max(s_new, axis=-1, keepdims=True))
        p_old = jnp.exp(s_old - m)
        p_new = jnp.exp(s_new - m)
        l = jnp.sum(p_old, axis=-1, keepdims=True) + jnp.sum(p_new, axis=-1, keepdims=True)
        acc = (_dot(p_old.astype(BF16), pv_ref[0, head_rows, :].astype(BF16))
               + _dot(p_new.astype(BF16), v_ref[0, :, sl]))
        o_ref[0, :, sl] = _da_finish(acc, l, _lam(lam_ref, lam_init), gs_ref[...], lam_init, tq)


def _da_sample(lam_par, g_subln, q, past_k, past_v, k, v, lam_init):
    B, T, _ = q.shape
    P = past_k.shape[1] // HEADS
    new_spec = pl.BlockSpec((1, T, MIX_HALF), lambda b: (b, 0, 0))
    past_spec = pl.BlockSpec((1, P * HEADS, HEAD_W), lambda b: (b, 0, 0))
    full = lambda a: pl.BlockSpec(a.shape, lambda b: (0,) * a.ndim)
    return pl.pallas_call(
        functools.partial(_da_sample_kernel, tq=T, past=P, lam_init=lam_init),
        grid=(B,),
        in_specs=[full(lam_par), full(g_subln), new_spec, past_spec, past_spec, new_spec, new_spec],
        out_specs=new_spec,
        out_shape=jax.ShapeDtypeStruct((B, T, MIX_HALF), BF16),
        compiler_params=_params(("parallel",)),
        name="diffattn_sample",
    )(lam_par, g_subln, q, past_k, past_v, k, v)


def _gdn_prep_kernel(q_ref, k_ref, v_ref, gate_ref, u_ref, w_ref, qd_ref, kt_ref, ai_ref, eg_ref, *, C, G):
    HC = HEADS * C
    row = lax.broadcasted_iota(jnp.int32, (C, C), 0)
    col = lax.broadcasted_iota(jnp.int32, (C, C), 1)
    causal = row >= col
    strict = row > col
    arow = lax.broadcasted_iota(jnp.int32, (C, HC), 0)
    acol = lax.broadcasted_iota(jnp.int32, (C, HC), 1)
    eye_all = (arow == (acol % C)).astype(F32)
    brow = lax.broadcasted_iota(jnp.int32, (HC, HC), 0)
    bcolm = lax.broadcasted_iota(jnp.int32, (HC, HC), 1)
    same_head = (brow // C) == (bcolm // C)
    prow = lax.broadcasted_iota(jnp.int32, (LANES, C), 0)
    pcol = lax.broadcasted_iota(jnp.int32, (LANES, C), 1)
    tri = jnp.where((prow >= pcol) & (prow < C), 1.0, 0.0).astype(BF16)

    def cumsum_rows(x):
        hi = x.astype(BF16)
        mid = (x - hi.astype(F32)).astype(BF16)
        lo = (x - hi.astype(F32) - mid.astype(F32)).astype(BF16)
        return _dot(tri, hi) + (_dot(tri, mid) + _dot(tri, lo))

    def block_diag(m_all):
        return jnp.where(same_head, jnp.concatenate([m_all] * HEADS, axis=0), jnp.zeros((), m_all.dtype))

    gcs, gcts, ps, xs = [], [], [], []
    for g in range(G):
        rs = slice(g * C, (g + 1) * C)
        gates = gate_ref[0, rs, :]
        gc_pad = cumsum_rows(gates)
        gcts.append(gc_pad.T)
        gcs.append(gc_pad[0:C, :])
        eg_ref[0, g * SUBLANES:(g + 1) * SUBLANES, :] = jnp.broadcast_to(
            jnp.exp(gc_pad[C - 1:C, :]), (SUBLANES, LANES))

    def decay_of(g, h):
        return jnp.exp(jnp.where(causal, gcs[g][:, h:h + 1] - gcts[g][h:h + 1, 0:C], NEG_INF))

    for g in range(G):
        rs = slice(g * C, (g + 1) * C)
        a_heads = []
        for h in range(HEADS):
            k = k_ref[0, rs, h * HEAD_W:(h + 1) * HEAD_W]
            kb = k * gate_ref[0, rs, HEADS + h:HEADS + h + 1]
            q = q_ref[0, rs, h * HEAD_W:(h + 1) * HEAD_W]
            both = _dot_nt(jnp.concatenate([kb.astype(BF16), q.astype(BF16)], axis=0), k.astype(BF16))
            decay = decay_of(g, h)
            a_heads.append(jnp.where(strict, -(both[0:C] * decay), 0.0))
            ai_ref[0, rs, h * C:(h + 1) * C] = (both[C:2 * C] * decay).astype(BF16)
        ps.append(jnp.concatenate(a_heads, axis=1))
        xs.append(eye_all + ps[g])
    def times(lefts, right_diag):
        bh, bl = right_diag
        top = _dot(jnp.concatenate([part for split in lefts for part in split], axis=0), bh)
        bot = _dot(jnp.concatenate([split[0] for split in lefts], axis=0), bl)
        return [top[2 * n * C:(2 * n + 1) * C] + top[(2 * n + 1) * C:(2 * n + 2) * C] + bot[n * C:(n + 1) * C]
                for n in range(len(lefts))]

    n_factors = int(math.log2(C))
    p_split = [_split_bf16(ps[g]) for g in range(G)]
    for g in range(G):
        p_split[g] = _split_bf16(times([p_split[g]], tuple(block_diag(part) for part in p_split[g]))[0])
    for k in range(1, n_factors):
        for g in range(G):
            diag = tuple(block_diag(part) for part in p_split[g])
            if k < n_factors - 1:
                x_prod, p_next = times([_split_bf16(xs[g]), p_split[g]], diag)
                p_split[g] = _split_bf16(p_next)
            else:
                x_prod, = times([_split_bf16(xs[g])], diag)
            xs[g] = xs[g] + x_prod

    for g in range(G):
        rs = slice(g * C, (g + 1) * C)
        for h in range(HEADS):
            sl = slice(h * HEAD_W, (h + 1) * HEAD_W)
            q = q_ref[0, rs, sl]
            v = v_ref[0, rs, sl]
            k = k_ref[0, rs, sl]
            gcol = gcs[g][:, h:h + 1]
            bcol = gate_ref[0, rs, HEADS + h:HEADS + h + 1]
            glast = gcs[g][C - 1:C, h:h + 1]
            egc = jnp.exp(gcol)
            t16 = xs[g][:, h * C:(h + 1) * C].astype(BF16)
            uw = _dot(t16, jnp.concatenate([v * bcol, k * bcol * egc], axis=1).astype(BF16))
            u_ref[0, rs, sl] = uw[:, 0:HEAD_W]
            w_ref[0, rs, sl] = uw[:, HEAD_W:2 * HEAD_W].astype(BF16)
            qd_ref[0, rs, sl] = (q * egc).astype(BF16)
            kt_ref[0, rs, sl] = (k * jnp.exp(glast - gcol)).astype(BF16)


def _gdn_scan_kernel(u_ref, w_ref, qd_ref, kt_ref, ai_ref, eg_ref, gz_ref, s0_ref, gout_ref, o_ref, sfin_ref, s_scr,
                     *, C, GB, n_chunks):
    c = pl.program_id(1)

    @pl.when(c == 0)
    def _():
        s_scr[...] = s0_ref[...]

    chains = [(b, h) for b in range(GB) for h in range(HEADS)]
    cols = lambda h: slice(h * HEAD_W, (h + 1) * HEAD_W)
    s16 = {bh: s_scr[bh[0], bh[1]].astype(BF16) for bh in chains}
    v_new = {(b, h): (u_ref[b, :, cols(h)] - _dot(w_ref[b, :, cols(h)], s16[b, h])).astype(BF16) for b, h in chains}
    o = {(b, h): _dot(qd_ref[b, :, cols(h)], s16[b, h]) + _dot(ai_ref[b, :, h * C:(h + 1) * C], v_new[b, h])
         for b, h in chains}
    for b, h in chains:
        s_scr[b, h] = s_scr[b, h] * eg_ref[b, 0:1, h:h + 1] + lax.dot_general(
            kt_ref[b, :, cols(h)], v_new[b, h], (((0,), (0,)), ((), ())), preferred_element_type=F32)
    for b, h in chains:
        o_ref[b, :, cols(h)] = (_rms(o[b, h], gout_ref[...]) * _silu(gz_ref[b, :, cols(h)])).astype(BF16)

    @pl.when(c == n_chunks - 1)
    def _():
        sfin_ref[...] = s_scr[...]


def _gdn(q, k, v, gz, gates, s0, g_out, C):
    B, T, _ = q.shape
    nc = T // C
    G = min(nc, 8)
    GB = next(n for n in (8, 4, 2, 1) if B % n == 0)
    HC = HEADS * C
    pblk = lambda w: pl.BlockSpec((1, G * C, w), lambda b, c: (b, c, 0))
    row = lambda w, dt: jax.ShapeDtypeStruct((B, T, w), dt)
    u, w, qd, kt, ai, eg = pl.pallas_call(
        functools.partial(_gdn_prep_kernel, C=C, G=G),
        grid=(B, nc // G),
        in_specs=[pblk(MIX_HALF)] * 3 + [pblk(LANES)],
        out_specs=[pblk(MIX_HALF)] * 4 + [pblk(HC), pl.BlockSpec((1, G * SUBLANES, LANES), lambda b, c: (b, c, 0))],
        out_shape=[row(MIX_HALF, F32), row(MIX_HALF, BF16), row(MIX_HALF, BF16), row(MIX_HALF, BF16), row(HC, BF16),
                   jax.ShapeDtypeStruct((B, nc * SUBLANES, LANES), F32)],
        compiler_params=_params(("parallel", "parallel")),
        name="gdn_prep",
    )(q, k, v, gates)
    sblk = lambda w: pl.BlockSpec((GB, C, w), lambda b, c: (b, c, 0))
    st = pl.BlockSpec((GB, HEADS, HEAD_W, HEAD_W), lambda b, c: (b, 0, 0, 0))
    return pl.pallas_call(
        functools.partial(_gdn_scan_kernel, C=C, GB=GB, n_chunks=nc),
        grid=(B // GB, nc),
        in_specs=[sblk(MIX_HALF)] * 4 + [sblk(HC), pl.BlockSpec((GB, SUBLANES, LANES), lambda b, c: (b, c, 0)),
                                         sblk(MIX_HALF), st, pl.BlockSpec(g_out.shape, lambda b, c: (0, 0))],
        out_specs=[sblk(MIX_HALF), st],
        out_shape=[jax.ShapeDtypeStruct((B, T, MIX_HALF), BF16),
                   jax.ShapeDtypeStruct((B, HEADS, HEAD_W, HEAD_W), F32)],
        scratch_shapes=[pltpu.VMEM((GB, HEADS, HEAD_W, HEAD_W), F32)],
        compiler_params=_params(("parallel", "arbitrary")),
        name="gdn_scan",
    )(u, w, qd, kt, ai, eg, gz, s0, g_out)


def _memkv_kernel(m_ref, g_ref, wk_ref, wv_ref, kf_ref, kb_ref, vf_ref, vb_ref):
    m = _rms(m_ref[...], g_ref[...]).astype(BF16)
    mk = _dot(m, wk_ref[...])
    mv = _dot(m, wv_ref[...])
    kf_ref[...] = mk
    kb_ref[...] = mk.astype(BF16)
    vf_ref[...] = mv
    vb_ref[...] = mv.astype(BF16)


def _memkv(mem, g_mem, w_ck, w_cv, tm):
    N, D = mem.shape
    blk = pl.BlockSpec((tm, D), lambda i: (i, 0))
    full = lambda a: pl.BlockSpec(a.shape, lambda i: (0,) * a.ndim)
    return pl.pallas_call(
        _memkv_kernel,
        grid=(N // tm,),
        in_specs=[blk, full(g_mem), full(w_ck), full(w_cv)],
        out_specs=[blk] * 4,
        out_shape=[jax.ShapeDtypeStruct((N, D), F32), jax.ShapeDtypeStruct((N, D), BF16)] * 2,
        compiler_params=_params(("parallel",)),
        name="memkv",
    )(mem, g_mem, w_ck, w_cv)


def _mid_kernel(x_ref, da_ref, gd_ref, wo_ref, gc_ref, wq_ref, mk_ref, mv_ref, wco_ref, gf_ref, wr_ref, br_ref,
                x2_ref, hf_ref, comb_ref):
    x1 = x_ref[0] + _dot(da_ref[0], wo_ref[0:MIX_HALF, :]) + _dot(gd_ref[0], wo_ref[MIX_HALF:2 * MIX_HALF, :])
    hc = _rms(x1, gc_ref[...]).astype(BF16)
    q = (_dot(hc, wq_ref[...]) * (X_DH ** -0.5)).astype(BF16)
    heads = []
    for h in range(X_HEADS):
        sl = slice(h * X_DH, (h + 1) * X_DH)
        s = _dot_nt(q[:, sl], mk_ref[0, :, sl])
        p = jnp.exp(s - jnp.max(s, axis=-1, keepdims=True))
        l = jnp.sum(p, axis=-1, keepdims=True)
        heads.append((_dot(p.astype(BF16), mv_ref[0, :, sl]) / l).astype(BF16))
    x2 = x1 + _dot(jnp.concatenate(heads, axis=1), wco_ref[...])
    x2_ref[0] = x2
    hf = _rms(x2, gf_ref[...])
    hf_ref[0] = hf.astype(BF16)

    logits = _dot3(hf, wr_ref[...]) + br_ref[...]
    lane_i = lax.broadcasted_iota(jnp.int32, logits.shape, 1)
    lane = lane_i.astype(F32)
    big = float(LANES)
    is_grp = (lane_i >= N_EXPERTS) & (lane_i < N_EXPERTS + N_GROUPS)
    gl = jnp.where(is_grp, logits, NEG_INF)
    gmax = jnp.max(gl, axis=-1, keepdims=True)
    g_top = 1.0 / jnp.sum(jnp.exp(gl - gmax), axis=-1, keepdims=True)
    g_idx = jnp.min(jnp.where(gl == gmax, lane, big), axis=-1, keepdims=True) - float(N_EXPERTS)
    in_grp = (lane_i < N_EXPERTS) & (jnp.floor(lane * (1.0 / E_PER_GROUP)) == g_idx)
    el = jnp.where(in_grp, logits, NEG_INF)
    e1 = jnp.max(el, axis=-1, keepdims=True)
    i1 = jnp.min(jnp.where(el == e1, lane, big), axis=-1, keepdims=True)
    el2 = jnp.where(lane == i1, NEG_INF, el)
    e2 = jnp.max(el2, axis=-1, keepdims=True)
    i2 = jnp.min(jnp.where(el2 == e2, lane, big), axis=-1, keepdims=True)
    r = jnp.exp(e2 - e1)
    w1 = g_top / (1.0 + r)
    w2 = g_top * r / (1.0 + r)
    comb_ref[0] = jnp.where(lane_i == 0, i1, jnp.where(lane_i == 1, i2, jnp.where(
        lane_i == 2, w1, jnp.where(lane_i == 3, w2, 0.0))))


def _mid(x, mix_da, mix_gd, w_out, g_cross, w_cq, mk, mv, w_co, g_ffn, w_router, b_router, tm):
    B, T, D = x.shape
    M = mk.shape[1]
    blk = lambda w: pl.BlockSpec((1, tm, w), lambda b, t: (b, t, 0))
    mem = pl.BlockSpec((1, M, D), lambda b, t: (b, 0, 0))
    full = lambda a: pl.BlockSpec(a.shape, lambda b, t: (0,) * a.ndim)
    return pl.pallas_call(
        _mid_kernel,
        grid=(B, T // tm),
        in_specs=[blk(D), blk(MIX_HALF), blk(MIX_HALF), full(w_out), full(g_cross), full(w_cq), mem, mem,
                  full(w_co), full(g_ffn), full(w_router), full(b_router)],
        out_specs=[blk(D), blk(D), blk(LANES)],
        out_shape=[jax.ShapeDtypeStruct((B, T, D), F32), jax.ShapeDtypeStruct((B, T, D), BF16),
                   jax.ShapeDtypeStruct((B, T, LANES), F32)],
        compiler_params=_params(("parallel", "parallel")),
        name="mid",
    )(x, mix_da, mix_gd, w_out, g_cross, w_cq, mk, mv, w_co, g_ffn, w_router, b_router)


MOE_BLK = 16
EXP_TILE_BLKS = (32, 16)


def _slots_per_tile(tm):
    worst = 2 * tm + N_EXPERTS * (MOE_BLK - 1)
    return -(-worst // LANES) * LANES


def _route_kernel(hf_ref, rec_ref, xs_ref, slot_ref, meta_ref, *, tm, sl, n_tiles):
    @pl.when(pl.program_id(0) == n_tiles)
    def _():
        xs_ref[...] = jnp.zeros(xs_ref.shape, BF16)
        slot_ref[...] = jnp.zeros(slot_ref.shape, F32)
        meta_ref[...] = jnp.zeros(meta_ref.shape, F32)

    @pl.when(pl.program_id(0) < n_tiles)
    def _():
        _route_tile(hf_ref, rec_ref, xs_ref, slot_ref, meta_ref, tm, sl)


def _route_tile(hf_ref, rec_ref, xs_ref, slot_ref, meta_ref, tm, sl):
    rec = rec_ref[...]
    lane_i = lax.broadcasted_iota(jnp.int32, rec.shape, 1)
    lane = lane_i.astype(F32)
    i1, i2, w1, w2 = rec[:, 0:1], rec[:, 1:2], rec[:, 2:3], rec[:, 3:4]
    oh1 = jnp.where(lane == i1, 1.0, 0.0)
    oh2 = jnp.where(lane == i2, 1.0, 0.0)
    oh = oh1 + oh2
    nblk = jnp.floor((jnp.sum(oh, axis=0, keepdims=True) + (MOE_BLK - 1.0)) * (1.0 / MOE_BLK))
    ur = lax.broadcasted_iota(jnp.int32, (LANES, LANES), 0)
    uc = lax.broadcasted_iota(jnp.int32, (LANES, LANES), 1)
    start = MOE_BLK * _dot(jnp.broadcast_to(nblk, (2 * SUBLANES, LANES)).astype(BF16),
                           jnp.where(ur < uc, 1.0, 0.0).astype(BF16))[0:1]
    r = lax.broadcasted_iota(jnp.int32, (tm, tm), 0)
    c = lax.broadcasted_iota(jnp.int32, (tm, tm), 1)
    pos = start + _dot(jnp.where(r > c, 1.0, 0.0).astype(BF16), oh.astype(BF16))
    slot1 = jnp.sum(oh1 * pos, axis=-1, keepdims=True)
    slot2 = jnp.sum(oh2 * pos, axis=-1, keepdims=True)
    slot_ref[...] = jnp.where(lane_i == 0, slot1, jnp.where(lane_i == 1, slot2, 0.0))
    meta_ref[0] = jnp.broadcast_to(nblk, (SUBLANES, LANES))

    srow = lax.broadcasted_iota(jnp.int32, (sl, tm), 0).astype(F32)
    perm = (jnp.where(srow == jnp.broadcast_to(slot1, (tm, LANES)).T[0:1, :], 1.0, 0.0)
            + jnp.where(srow == jnp.broadcast_to(slot2, (tm, LANES)).T[0:1, :], 2.0, 0.0)).astype(BF16)

    def parts(w, base):
        hi = w.astype(BF16).astype(F32)
        mid = (w - hi).astype(BF16).astype(F32)
        lo = w - hi - mid
        return jnp.where(lane_i == base, hi, jnp.where(lane_i == base + 1, mid,
                                                       jnp.where(lane_i == base + 2, lo, 0.0)))

    extra = parts(w1, 0) + parts(w2, 3) + jnp.where(lane_i == 6, 1.0, 0.0)
    moved = _dot(perm, jnp.concatenate([hf_ref[...], extra.astype(BF16)], axis=1))
    d = hf_ref.shape[1]
    second = moved[:, d + 6:d + 7] == 2.0
    moved = moved * jnp.where(second, 0.5, 1.0)
    lane_s = lax.broadcasted_iota(jnp.int32, (sl, LANES), 1)
    xs_ref[:, 0:d] = moved[:, 0:d].astype(BF16)
    xs_ref[:, d:d + LANES] = jnp.where(lane_s == 7, jnp.where(second, 1.0, 0.0), moved[:, d:d + LANES]).astype(BF16)


def _route(hf, rec, tm):
    N, D = hf.shape
    nt = N // tm
    sl = _slots_per_tile(tm)
    in_blk = lambda w: pl.BlockSpec((tm, w), lambda i: (jnp.minimum(i, nt - 1), 0))
    blk = lambda rows, w: pl.BlockSpec((rows, w), lambda i: (i, 0))
    return pl.pallas_call(
        functools.partial(_route_kernel, tm=tm, sl=sl, n_tiles=nt),
        grid=(nt + 1,),
        in_specs=[in_blk(D), in_blk(LANES)],
        out_specs=[blk(sl, D + LANES), blk(tm, LANES), pl.BlockSpec((1, SUBLANES, LANES), lambda i: (i, 0, 0))],
        out_shape=[jax.ShapeDtypeStruct(((nt + 1) * sl, D + LANES), BF16),
                   jax.ShapeDtypeStruct(((nt + 1) * tm, LANES), F32),
                   jax.ShapeDtypeStruct((nt + 1, SUBLANES, LANES), F32)],
        compiler_params=_params(("parallel",)),
        name="moe_route",
    )(hf, rec)


def _moe_tables(meta, blocks_per_tile, max_tiles, nb):
    nblk = meta[:, 0, :N_EXPERTS].astype(jnp.int32)
    nt = nblk.shape[0]
    tiles_e = (jnp.sum(nblk, axis=0) + nb - 1) // nb
    tile_end = jnp.cumsum(tiles_e)
    first_blk = (tile_end - tiles_e) * nb
    off = first_blk[None, :] + jnp.cumsum(nblk, axis=0) - nblk
    lend = jnp.cumsum(nblk, axis=1)
    lstart = lend - nblk
    lb = jnp.arange(blocks_per_tile, dtype=jnp.int32)[None, :]
    in_run = (lstart[:, None, :] <= lb[:, :, None]) & (lb[:, :, None] < lend[:, None, :])
    valid = lb < lend[:, -1:]
    dest = jnp.sum(jnp.where(in_run, (off - lstart)[:, None, :], 0), axis=-1) + lb
    flat = jnp.arange(nt, dtype=jnp.int32)[:, None] * blocks_per_tile + lb
    n_sorted = max_tiles * nb
    src = jnp.full((n_sorted,), -1, jnp.int32).at[jnp.where(valid, dest, n_sorted).reshape(-1)].set(
        flat.reshape(-1), mode="drop")
    g = jnp.arange(max_tiles, dtype=jnp.int32)
    tile_expert = jnp.minimum(jnp.sum(tile_end[None, :] <= g[:, None], axis=-1), N_EXPERTS - 1).astype(jnp.int32)
    return src, tile_expert, tile_end[-1:].astype(jnp.int32)


def _experts_kernel(src_ref, te_ref, nt_ref, xs_hbm, wg_ref, wu_ref, wd_ref, ys_hbm, xbuf, obuf, gsem, ssem, *,
                    zero_blk, spare_blk):
    g = pl.program_id(0)
    n_tiles = nt_ref[0]
    nb = obuf.shape[1]
    d = obuf.shape[-1]

    def gather_copy(tile, slot, i):
        s = src_ref[tile * nb + i]
        return pltpu.make_async_copy(xs_hbm.at[jnp.where(s >= 0, s, zero_blk)], xbuf.at[slot, i], gsem.at[slot])

    def scatter_copy(tile, slot, i):
        s = src_ref[tile * nb + i]
        dst = jnp.where(s >= 0, s, spare_blk + slot * nb + i)
        return pltpu.make_async_copy(obuf.at[slot, i], ys_hbm.at[dst, :, pl.ds(0, d)], ssem.at[slot])

    def gather_start(tile, slot):
        for i in range(nb):
            gather_copy(tile, slot, i).start()

    def gather_wait(tile, slot):
        for i in range(nb):
            gather_copy(tile, slot, i).wait()

    def scatter_start(tile, slot):
        for i in range(nb):
            scatter_copy(tile, slot, i).start()

    def scatter_wait(tile, slot):
        for i in range(nb):
            scatter_copy(tile, slot, i).wait()

    slot = g % 2
    xslot = g % 3

    @pl.when(g == 0)
    def _():
        gather_start(0, 0)

        @pl.when(n_tiles > 1)
        def _():
            gather_start(1, 1)

    @pl.when(g < n_tiles)
    def _():
        @pl.when(g + 2 < n_tiles)
        def _():
            gather_start(g + 2, (g + 2) % 3)

        gather_wait(g, xslot)
        rows = nb * MOE_BLK
        xe = xbuf[xslot].reshape(rows, xbuf.shape[-1])
        x = xe[:, 0:d]
        extra = xe[:, d:d + LANES].astype(F32)
        lane = lax.broadcasted_iota(jnp.int32, extra.shape, 1)
        first_lane = 3.0 * jnp.sum(jnp.where(lane == 7, extra, 0.0), axis=-1, keepdims=True)
        lane_f = lane.astype(F32)
        mine = (lane_f >= first_lane) & (lane_f < first_lane + 3.0)
        ccol = jnp.sum(jnp.where(mine, extra, 0.0), axis=-1, keepdims=True)
        hg = _silu(_dot(x, wg_ref[0])) * _dot(x, wu_ref[0])
        y = _dot((hg * ccol).astype(BF16), wd_ref[0]).astype(BF16)

        @pl.when(g >= 2)
        def _():
            scatter_wait(g - 2, slot)

        obuf[slot] = y.reshape(nb, MOE_BLK, d)
        scatter_start(g, slot)

    @pl.when(g == n_tiles - 1)
    def _():
        scatter_wait(g, slot)

        @pl.when(g >= 1)
        def _():
            scatter_wait(g - 1, 1 - slot)


def _experts(src, tile_expert, n_tiles, xs, w_gate, w_up, w_down, max_tiles, zero_blk, nb):
    n_rows, width = xs.shape
    assert zero_blk + 1 + 2 * nb <= n_rows // MOE_BLK, "spare tile too small for the empty-slot targets"
    D = width - LANES
    slab = xs.reshape(n_rows // MOE_BLK, MOE_BLK, width)
    wspec = lambda shape: pl.BlockSpec((1,) + shape, lambda g, src, te, nt: (te[g], 0, 0))
    grid_spec = pltpu.PrefetchScalarGridSpec(
        num_scalar_prefetch=3,
        grid=(max_tiles,),
        in_specs=[pl.BlockSpec(memory_space=pl.ANY), wspec((D, D_EXPERT)), wspec((D, D_EXPERT)), wspec((D_EXPERT, D))],
        out_specs=pl.BlockSpec(memory_space=pl.ANY),
        scratch_shapes=[pltpu.VMEM((3, nb, MOE_BLK, width), BF16),
                        pltpu.VMEM((2, nb, MOE_BLK, D), BF16),
                        pltpu.SemaphoreType.DMA((3,)), pltpu.SemaphoreType.DMA((2,))],
    )
    return pl.pallas_call(
        functools.partial(_experts_kernel, zero_blk=zero_blk, spare_blk=zero_blk + 1),
        grid_spec=grid_spec,
        out_shape=jax.ShapeDtypeStruct(slab.shape, BF16),
        input_output_aliases={3: 0},
        compiler_params=_params(("arbitrary",)),
        name="moe_experts",
    )(src, tile_expert, n_tiles, slab, w_gate, w_up, w_down).reshape(n_rows, width)


def _combine_kernel(x2_ref, slot_ref, ys_ref, gfin_ref, y_ref, *, tm, sl):
    slots = slot_ref[...]
    col = lax.broadcasted_iota(jnp.int32, (tm, sl), 1).astype(F32)
    pick = (jnp.where(col == slots[:, 0:1], 1.0, 0.0) + jnp.where(col == slots[:, 1:2], 1.0, 0.0)).astype(BF16)
    y_ref[...] = _rms(x2_ref[...] + _dot(pick, ys_ref[:, 0:x2_ref.shape[1]]), gfin_ref[...])


def _combine(x2, slots, ys, g_final, tm, sl):
    N, D = x2.shape
    blk = lambda rows, w: pl.BlockSpec((rows, w), lambda i: (i, 0))
    return pl.pallas_call(
        functools.partial(_combine_kernel, tm=tm, sl=sl),
        grid=(N // tm,),
        in_specs=[blk(tm, D), blk(tm, LANES), blk(sl, D), pl.BlockSpec(g_final.shape, lambda i: (0, 0))],
        out_specs=blk(tm, D),
        out_shape=jax.ShapeDtypeStruct((N, D), F32),
        compiler_params=_params(("parallel",)),
        name="moe_combine",
    )(x2, slots, ys, g_final)


def _moe(x2, hf, rec, w_gate, w_up, w_down, g_final, tm):
    N, _ = x2.shape
    nt = N // tm
    sl = _slots_per_tile(tm)
    xs, slots, meta = _route(hf, rec, tm)
    blocks_per_tile = sl // MOE_BLK
    nb = next(n for n in EXP_TILE_BLKS if 1 + 2 * n <= blocks_per_tile)
    max_tiles = -(-nt * blocks_per_tile // nb) + N_EXPERTS
    src, tile_expert, n_tiles = _moe_tables(meta[:nt], blocks_per_tile, max_tiles, nb)
    ys = _experts(src, tile_expert, n_tiles, xs, w_gate, w_up, w_down, max_tiles, nt * blocks_per_tile, nb)
    return _combine(x2, slots, ys, g_final, tm, sl)


def _pad_rows(a, rows):
    return jnp.pad(a, ((0, rows - a.shape[0]), (0, LANES - a.shape[1])))


def _trunk(x, mk16, mv16, past_k, past_v, conv_prev, s_prev, lam_init, wts):
    B, T, D = x.shape
    tm = min(T, 512)
    if conv_prev is None:
        conv_prev = jnp.zeros((B, CONV_W - 1, GDN_QKV), F32)
    if s_prev is None:
        s_prev = jnp.zeros((B, HEADS, HEAD_W, HEAD_W), F32)
    prev8 = jnp.pad(conv_prev, ((0, 0), (SUBLANES - (CONV_W - 1), 0), (0, 0)))
    (dq, dkf, dk16, dvf, dv16, cq, ck, cv, gz, gates, cnew) = _inproj(
        x, prev8, wts["g_mix"], wts["w_main"], wts["w_gate8"], wts["conv_w8"], wts["gate_par"], tm)
    if past_k is None:
        mix_da = _da_prompt(wts["lam_par"], wts["g_subln"], dq, dk16, dv16, min(T, 512), lam_init)
    else:
        P = past_k.shape[1]
        mix_da = _da_sample(wts["lam_par"], wts["g_subln"], dq, past_k.reshape(B, P * HEADS, HEAD_W),
                            past_v.reshape(B, P * HEADS, HEAD_W), dk16, dv16, lam_init)
    C = min(T, CHUNK)
    mix_gd, s_new = _gdn(cq, ck, cv, gz, gates, s_prev, wts["g_gdn_out"], C)
    x2, hf, comb = _mid(x, mix_da, mix_gd, wts["w_out"], wts["g_cross"], wts["w_cq"], mk16, mv16, wts["w_co"],
                        wts["g_ffn"], wts["w_router"], wts["b_router"], min(T, 512))
    N = B * T
    y = _moe(x2.reshape(N, D), hf.reshape(N, D), comb.reshape(N, LANES), wts["w_gate"], wts["w_up"],
             wts["w_down"], wts["g_final"], min(N, 512))
    new_k = dkf.reshape(1, B, T, HEADS, HEAD_W)
    new_v = dvf.reshape(1, B, T, HEADS, HEAD_W)
    new_conv = cnew[:, SUBLANES - (CONV_W - 1):, :].reshape(1, B, CONV_W - 1, GDN_QKV)
    return y.reshape(B, T, D), new_k, new_v, new_conv, s_new[None]


def kernel(x_prompt, x_sample, cache_diff_k, cache_diff_v, state_gdn_conv, state_gdn, cache_mem_k, cache_mem_v,
           mem_prompt, g_mix, w_in, lam_q1, lam_k1, lam_q2, lam_k2, g_subln, conv_w, a_log, dt_bias, g_gdn_out,
           w_out, g_cross, g_mem, w_ck, w_cv, w_cq, w_co, g_ffn, w_grp, b_grp, w_rt, b_rt, w_gate, w_up, w_down,
           g_final):
    assert g_mix.shape[0] == 1, "single-layer model"
    l = 0
    lam_init = 0.8 - 0.6 * math.exp(-0.3 * l)
    gate_lo = 3 * MIX_HALF + GDN_QKV
    gate_hi = gate_lo + 2 * HEADS
    wi = w_in[l]
    row = lambda a: a.reshape(1, -1).astype(F32)
    wts = dict(
        g_mix=row(g_mix[l]),
        w_main=jnp.concatenate([wi[:, :gate_lo], wi[:, gate_hi:]], axis=1).astype(BF16),
        w_gate8=jnp.pad(wi[:, gate_lo:gate_hi], ((0, 0), (0, LANES - 2 * HEADS))).astype(BF16),
        conv_w8=jnp.pad(conv_w[l], ((0, SUBLANES - CONV_W), (0, 0))),
        gate_par=_pad_rows(jnp.stack([a_log[l], dt_bias[l]]), SUBLANES),
        lam_par=_pad_rows(jnp.stack([lam_q1[l], lam_k1[l], lam_q2[l], lam_k2[l]]), SUBLANES),
        g_subln=row(g_subln[l]), g_gdn_out=row(g_gdn_out[l]),
        w_out=w_out[l].astype(BF16), g_cross=row(g_cross[l]), w_cq=w_cq[l].astype(BF16), w_co=w_co[l].astype(BF16),
        g_ffn=row(g_ffn[l]),
        w_router=jnp.pad(jnp.concatenate([w_rt[l], w_grp[l]], axis=1),
                         ((0, 0), (0, LANES - N_EXPERTS - N_GROUPS))),
        b_router=jnp.pad(jnp.concatenate([b_rt[l], b_grp[l]]), (0, LANES - N_EXPERTS - N_GROUPS)).reshape(1, LANES),
        w_gate=w_gate[l].astype(BF16), w_up=w_up[l].astype(BF16), w_down=w_down[l].astype(BF16),
        g_final=row(g_final),
    )
    B, M, D = mem_prompt.shape
    mkf, mk16, mvf, mv16 = _memkv(mem_prompt.reshape(B * M, D), row(g_mem[l]), w_ck[l].astype(BF16),
                                  w_cv[l].astype(BF16), min(B * M, 512))
    yp, pk, pv, pc, ps = _trunk(x_prompt, mk16.reshape(B, M, D), mv16.reshape(B, M, D), None, None, None, None,
                                lam_init, wts)
    Bs = x_sample.shape[0]
    ys, sk, sv, sc, ss = _trunk(x_sample, cache_mem_k[l].reshape(Bs, M, D).astype(BF16),
                                cache_mem_v[l].reshape(Bs, M, D).astype(BF16), cache_diff_k[l], cache_diff_v[l],
                                state_gdn_conv[l], state_gdn[l], lam_init, wts)
    mem_shape = (1, B, M, X_HEADS, X_DH)
    return (yp, ys, pk, pv, pc, ps, mkf.reshape(mem_shape), mvf.reshape(mem_shape), sk, sv, sc, ss)
```
